```python
import jax, jax.numpy as jnp
from jax import lax
import numpy as np

D_MODEL = 2048
BATCH = 4
SEQ = 8192
DEPTH = 1
DEC_BATCH = 1
DEC_SEQ = 16384
PAST_LEN = 128

N_FOURIER_GROUPS = 4
FOURIER_GROUP_DIM = D_MODEL // 8
FOURIER_WIDTH = N_FOURIER_GROUPS * FOURIER_GROUP_DIM
HEAD_DIM = 128
N_HEADS = D_MODEL // (2 * HEAD_DIM)
N_KV_HEADS = N_HEADS // 4
Q_PER_KV = N_HEADS // N_KV_HEADS
ATTN_WIDTH = N_HEADS * HEAD_DIM
KV_WIDTH = N_KV_HEADS * HEAD_DIM
ROT_DIM = HEAD_DIM // 4
ROPE_THETA = 500000.0
WINDOW = 128
BLOCK = 128
IN_WIDTH = FOURIER_WIDTH + ATTN_WIDTH + 2 * KV_WIDTH + 2 * D_MODEL
N_GROUPS = 8
EXPERTS_PER_GROUP = 8
N_EXPERTS = N_GROUPS * EXPERTS_PER_GROUP
TOP_K = 2
D_FF_EXPERT = D_MODEL // 4
MOE_BLOCK = 128
EPS = 1e-6

kernel_name = "hybrid_fnet_swa_hmoe_adaln_encoder"


def rms_norm(x, g):
    xf = x.astype(jnp.float32)
    y = xf * lax.rsqrt(jnp.mean(xf * xf, axis=-1, keepdims=True) + EPS)
    return (y * g.astype(jnp.float32)).astype(x.dtype)


def modulate(h, shift, scale):
    return h * (1 + scale[:, None, :]) + shift[:, None, :]


def partial_rope(x):
    S = x.shape[1]
    inv_freq = ROPE_THETA ** (-jnp.arange(0, ROT_DIM, 2, dtype=jnp.float32) / ROT_DIM)
    ang = jnp.arange(S, dtype=jnp.float32)[:, None] * inv_freq[None, :]
    cos = jnp.cos(ang)[None, :, None, :]
    sin = jnp.sin(ang)[None, :, None, :]
    xr = x[..., :ROT_DIM].astype(jnp.float32)
    x1, x2 = xr[..., :ROT_DIM // 2], xr[..., ROT_DIM // 2:]
    rot = jnp.concatenate([x1 * cos - x2 * sin, x2 * cos + x1 * sin], axis=-1).astype(x.dtype)
    return jnp.concatenate([rot, x[..., ROT_DIM:]], axis=-1)


def fourier_mix(u):
    B, S, _ = u.shape
    ug = u.astype(jnp.float32).reshape(B, S, N_FOURIER_GROUPS, FOURIER_GROUP_DIM)
    ug = jnp.transpose(ug, (0, 2, 1, 3))
    f = jnp.fft.fft2(ug, axes=(-2, -1), norm="ortho").real
    return jnp.transpose(f, (0, 2, 1, 3)).reshape(B, S, FOURIER_WIDTH).astype(u.dtype)


def window_attention(q, k, v, sink):
    B, S = q.shape[0], q.shape[1]
    nb = S // BLOCK
    pad = ((0, 0), (BLOCK, BLOCK), (0, 0), (0, 0))
    kp = jnp.pad(k, pad)
    vp = jnp.pad(v, pad)
    scale = HEAD_DIM ** -0.5
    sink_f = sink.astype(jnp.float32).reshape(N_KV_HEADS, Q_PER_KV)[None, :, :, None]

    def one_block(b):
        start = b * BLOCK
        qb = lax.dynamic_slice_in_dim(q, start, BLOCK, axis=1).astype(jnp.float32)
        kb = lax.dynamic_slice_in_dim(kp, start, 3 * BLOCK, axis=1).astype(jnp.float32)
        vb = lax.dynamic_slice_in_dim(vp, start, 3 * BLOCK, axis=1).astype(jnp.float32)
        s = jnp.einsum('bqkgd,bskd->bkgqs', qb, kb) * scale
        qpos = start + jnp.arange(BLOCK)
        kpos = start - BLOCK + jnp.arange(3 * BLOCK)
        valid = ((jnp.abs(qpos[:, None] - kpos[None, :]) <= WINDOW)
                 & (kpos >= 0)[None, :] & (kpos < S)[None, :])
        s = jnp.where(valid, s, -jnp.inf)
        m = jnp.maximum(jnp.max(s, axis=-1), sink_f)
        p = jnp.exp(s - m[..., None])
        denom = jnp.sum(p, axis=-1) + jnp.exp(sink_f - m)
        o = jnp.einsum('bkgqs,bskd->bqkgd', p, vb)
        o = o / jnp.transpose(denom, (0, 3, 1, 2))[..., None]
        return o.astype(q.dtype)

    out = lax.map(one_block, jnp.arange(nb))
    return jnp.moveaxis(out, 0, 1).reshape(B, S, ATTN_WIDTH)


def mixer(h, w_in, q_norm_g, k_norm_g, sink, w_fourier_out, w_attn_out, w_out):
    B, S, _ = h.shape
    z = h @ w_in
    c1 = FOURIER_WIDTH
    c2 = c1 + ATTN_WIDTH
    c3 = c2 + KV_WIDTH
    c4 = c3 + KV_WIDTH
    u_f, q, k, v, gates = jnp.split(z, [c1, c2, c3, c4], axis=-1)
    y_f = fourier_mix(u_f) @ w_fourier_out
    q = partial_rope(rms_norm(q.reshape(B, S, N_HEADS, HEAD_DIM), q_norm_g))
    k = partial_rope(rms_norm(k.reshape(B, S, N_KV_HEADS, HEAD_DIM), k_norm_g))
    v = v.reshape(B, S, N_KV_HEADS, HEAD_DIM)
    o = window_attention(q.reshape(B, S, N_KV_HEADS, Q_PER_KV, HEAD_DIM), k, v, sink)
    y_a = o @ w_attn_out
    g_f, g_a = jnp.split(jax.nn.sigmoid(gates), 2, axis=-1)
    return (g_f * y_f + g_a * y_a) @ w_out


def hier_moe(h, w_group, b_group, w_expert, b_expert, w1, w3, w2):
    B, S, D = h.shape
    T = B * S
    xt = h.reshape(T, D)
    p_group = jax.nn.softmax((xt @ w_group).astype(jnp.float32) + b_group.astype(jnp.float32), axis=-1)
    p_g_top, g_idx = lax.top_k(p_group, 1)
    e_logits = ((xt @ w_expert).astype(jnp.float32) + b_expert.astype(jnp.float32))
    e_logits = e_logits.reshape(T, N_GROUPS, EXPERTS_PER_GROUP)
    idx = jnp.broadcast_to(g_idx[:, :, None], (T, 1, EXPERTS_PER_GROUP))
    e_logits = jnp.take_along_axis(e_logits, idx, axis=1)[:, 0]
    p_top, e_local = lax.top_k(jax.nn.softmax(e_logits, axis=-1), TOP_K)
    w_tok = p_g_top * p_top / jnp.sum(p_top, axis=-1, keepdims=True)
    e_glob = g_idx * EXPERTS_PER_GROUP + e_local

    A = T * TOP_K
    flat_e = e_glob.reshape(A).astype(jnp.int32)
    flat_w = w_tok.reshape(A)
    order = jnp.argsort(flat_e)
    se = flat_e[order]
    tok = (order // TOP_K).astype(jnp.int32)
    sw = flat_w[order]
    counts = jnp.zeros((N_EXPERTS,), jnp.int32).at[flat_e].add(1)
    starts = jnp.cumsum(counts) - counts
    padded = (counts + MOE_BLOCK - 1) // MOE_BLOCK * MOE_BLOCK
    pad_end = jnp.cumsum(padded)
    pad_start = pad_end - padded
    dest = pad_start[se] + jnp.arange(A, dtype=jnp.int32) - starts[se]
    n_blocks = (A + N_EXPERTS * (MOE_BLOCK - 1) + MOE_BLOCK - 1) // MOE_BLOCK
    P = n_blocks * MOE_BLOCK
    row_tok = jnp.full((P,), T, jnp.int32).at[dest].set(tok)
    block_e = jnp.minimum(
        jnp.searchsorted(pad_end, jnp.arange(n_blocks, dtype=jnp.int32) * MOE_BLOCK, side='right'),
        N_EXPERTS - 1)
    x_pad = jnp.concatenate([xt, jnp.zeros((1, D), xt.dtype)], axis=0)
    xb = x_pad[row_tok].reshape(n_blocks, MOE_BLOCK, D)

    def expert_block(args):
        xblk, e = args
        hid = jax.nn.silu(xblk @ w1[e]) * (xblk @ w3[e])
        return hid @ w2[e]

    yb = lax.map(expert_block, (xb, block_e)).reshape(P, D)
    y_assign = yb[dest] * sw[:, None].astype(yb.dtype)
    return jax.ops.segment_sum(y_assign, tok, num_segments=T).reshape(B, S, D)


def encoder_layer(x, c, w_ada, b_ada, norm1_g, w_in, q_norm_g, k_norm_g, sink,
                  w_fourier_out, w_attn_out, w_out, norm2_g, w_group, b_group,
                  w_expert, b_expert, w1, w3, w2):
    mod = jax.nn.silu(c) @ w_ada + b_ada
    sh1, sc1, g1, sh2, sc2, g2 = jnp.split(mod, 6, axis=-1)
    h = modulate(rms_norm(x, norm1_g), sh1, sc1)
    x = x + g1[:, None, :] * mixer(h, w_in, q_norm_g, k_norm_g, sink, w_fourier_out, w_attn_out, w_out)
    h = modulate(rms_norm(x, norm2_g), sh2, sc2)
    x = x + g2[:, None, :] * hier_moe(h, w_group, b_group, w_expert, b_expert, w1, w3, w2)
    return x


def run_trunk(x, c, w_ada, b_ada, norm1_g, w_in, q_norm_g, k_norm_g, sink,
              w_fourier_out, w_attn_out, w_out, norm2_g, w_group, b_group,
              w_expert, b_expert, w1, w3, w2):
    for l in range(DEPTH):
        x = encoder_layer(x, c, w_ada[l], b_ada[l], norm1_g[l], w_in[l], q_norm_g[l], k_norm_g[l],
                          sink[l], w_fourier_out[l], w_attn_out[l], w_out[l], norm2_g[l],
                          w_group[l], b_group[l], w_expert[l], b_expert[l], w1[l], w3[l], w2[l])
    return x


def setup_inputs(seed: int = 0) -> dict:
    key = jax.random.key(seed)
    ks = jax.random.split(key, 24)
    f32 = jnp.float32
    nrm = lambda k, shape, s: jax.random.normal(k, shape, f32) * s
    L, D = DEPTH, D_MODEL
    return {
        "x_prompt": nrm(ks[0], (BATCH, SEQ, D), 1.0),
        "x_sample": nrm(ks[1], (DEC_BATCH, DEC_SEQ, D), 1.0),
        "c_prompt": nrm(ks[2], (BATCH, D), 1.0),
        "c_sample": nrm(ks[3], (DEC_BATCH, D), 1.0),
        "w_ada": nrm(ks[4], (L, D, 6 * D), 0.5 * D ** -0.5),
        "b_ada": nrm(ks[5], (L, 6 * D), 0.02),
        "norm1_g": 1.0 + nrm(ks[6], (L, D), 0.02),
        "w_in": nrm(ks[7], (L, D, IN_WIDTH), D ** -0.5),
        "q_norm_g": 1.0 + nrm(ks[8], (L, HEAD_DIM), 0.02),
        "k_norm_g": 1.0 + nrm(ks[9], (L, HEAD_DIM), 0.02),
        "sink": nrm(ks[10], (L, N_HEADS), 0.5),
        "w_fourier_out": nrm(ks[11], (L, FOURIER_WIDTH, D), FOURIER_WIDTH ** -0.5),
        "w_attn_out": nrm(ks[12], (L, ATTN_WIDTH, D), ATTN_WIDTH ** -0.5),
        "w_out": nrm(ks[13], (L, D, D), D ** -0.5),
        "norm2_g": 1.0 + nrm(ks[14], (L, D), 0.02),
        "w_group": nrm(ks[15], (L, D, N_GROUPS), D ** -0.5),
        "b_group": nrm(ks[16], (L, N_GROUPS), 0.01),
        "w_expert": nrm(ks[17], (L, D, N_EXPERTS), D ** -0.5),
        "b_expert": nrm(ks[18], (L, N_EXPERTS), 0.01),
        "w1": nrm(ks[19], (L, N_EXPERTS, D, D_FF_EXPERT), D ** -0.5),
        "w3": nrm(ks[20], (L, N_EXPERTS, D, D_FF_EXPERT), D ** -0.5),
        "w2": nrm(ks[21], (L, N_EXPERTS, D_FF_EXPERT, D), D_FF_EXPERT ** -0.5),
    }


def reference(x_prompt, x_sample, c_prompt, c_sample, w_ada, b_ada, norm1_g, w_in, q_norm_g,
              k_norm_g, sink, w_fourier_out, w_attn_out, w_out, norm2_g, w_group, b_group,
              w_expert, b_expert, w1, w3, w2):
    y_prompt = run_trunk(x_prompt, c_prompt, w_ada, b_ada, norm1_g, w_in, q_norm_g, k_norm_g, sink,
                         w_fourier_out, w_attn_out, w_out, norm2_g, w_group, b_group,
                         w_expert, b_expert, w1, w3, w2)
    y_sample = run_trunk(x_sample, c_sample, w_ada, b_ada, norm1_g, w_in, q_norm_g, k_norm_g, sink,
                         w_fourier_out, w_attn_out, w_out, norm2_g, w_group, b_group,
                         w_expert, b_expert, w1, w3, w2)
    return (y_prompt, y_sample)
```

```python
import functools
import math

import jax
import jax.numpy as jnp
from jax import lax
from jax.experimental import pallas as pl
from jax.experimental.pallas import tpu as pltpu

F32 = jnp.float32
BF16 = jnp.bfloat16
I32 = jnp.int32

HEAD_DIM = 128
N_KV_HEADS = 2
Q_PER_KV = 4
N_HEADS = N_KV_HEADS * Q_PER_KV
ROT_DIM = 32
ROPE_THETA = 500000.0
WINDOW = 128
FOURIER_GROUPS = 4
FOURIER_GROUP_DIM = 256
N_GROUPS = 8
EXPERTS_PER_GROUP = 8
N_EXPERTS = N_GROUPS * EXPERTS_PER_GROUP
TOP_K = 2
EPS = 1e-6

LANES = 128
V7X_VMEM_LIMIT = 56 * 1024 * 1024

ROW_TILE = 256
ATTN_TILE = 512
FFT_N1 = 128
FFT_UNROLL = 4
MOE_ROWS = 128
ROUTE_LANES = 128
NEG = -1e30


def _cparams(sem):
    return pltpu.CompilerParams(dimension_semantics=sem, vmem_limit_bytes=V7X_VMEM_LIMIT)


def _resident(shape, index_map):
    return pl.BlockSpec(shape, index_map, pipeline_mode=pl.Buffered(1))


def _adaln_kernel(c_ref, w_ref, b_ref, o_ref):
    c = c_ref[...]
    s = c * jax.nn.sigmoid(c)
    o_ref[...] = jnp.dot(s, w_ref[...], precision=lax.Precision.HIGHEST,
                         preferred_element_type=F32) + b_ref[...]


def _adaln(c, w_ada, b_ada):
    r, d = c.shape
    n = w_ada.shape[1]
    tn = 1024
    return pl.pallas_call(
        _adaln_kernel,
        grid=(n // tn,),
        in_specs=[pl.BlockSpec((r, d), lambda j: (0, 0)),
                  pl.BlockSpec((d, tn), lambda j: (0, j)),
                  pl.BlockSpec((1, tn), lambda j: (0, j))],
        out_specs=pl.BlockSpec((r, tn), lambda j: (0, j)),
        out_shape=jax.ShapeDtypeStruct((r, n), F32),
        compiler_params=_cparams(("parallel",)),
        name="adaln",
    )(c, w_ada, b_ada.reshape(1, n))


def _rope_tables(s):
    half = ROT_DIM // 2
    inv_freq = ROPE_THETA ** (-jnp.arange(0, ROT_DIM, 2, dtype=F32) / ROT_DIM)
    ang = jnp.arange(s, dtype=F32)[:, None] * inv_freq[None, :]
    cos, sin = jnp.cos(ang), jnp.sin(ang)
    pad = jnp.zeros((s, LANES - ROT_DIM), F32)
    cos_f = jnp.concatenate([cos, cos, jnp.ones((s, LANES - ROT_DIM), F32)], axis=1)
    sin_up = jnp.concatenate([-sin, jnp.zeros((s, half), F32), pad], axis=1)
    sin_dn = jnp.concatenate([jnp.zeros((s, half), F32), sin, pad], axis=1)
    return cos_f, sin_up, sin_dn


def _inproj_kernel(x_ref, sh_ref, sc_ref, g_ref, qg_ref, kg_ref, cos_ref, sup_ref, sdn_ref, w_ref,
                   u_ref, qkv_ref, gate_ref, *, fw, aw, kvw):
    x = x_ref[...]
    ms = jnp.mean(x * x, axis=-1, keepdims=True)
    h = (x * lax.rsqrt(ms + EPS) * g_ref[...]) * (1.0 + sc_ref[...]) + sh_ref[...]
    hb = h.astype(BF16)
    chunk = 512
    for c0 in range(0, fw, chunk):
        u_ref[:, c0:c0 + chunk] = jnp.dot(hb, w_ref[:, c0:c0 + chunk], preferred_element_type=F32)
    cos_f, sin_up, sin_dn = cos_ref[...], sup_ref[...], sdn_ref[...]
    half = ROT_DIM // 2
    scale = HEAD_DIM ** -0.5
    for c0 in list(range(0, aw, chunk)) + [aw]:
        width = chunk if c0 < aw else kvw
        acc = jnp.dot(hb, w_ref[:, fw + c0:fw + c0 + width], preferred_element_type=F32)
        for hh in range(width // HEAD_DIM):
            col = c0 + hh * HEAD_DIM
            t = acc[:, hh * HEAD_DIM:(hh + 1) * HEAD_DIM]
            is_q = col < aw
            gain = qg_ref[...] if is_q else kg_ref[...]
            t = t * lax.rsqrt(jnp.mean(t * t, axis=-1, keepdims=True) + EPS) * gain
            t = (t * cos_f + pltpu.roll(t, LANES - half, axis=1) * sin_up
                 + pltpu.roll(t, half, axis=1) * sin_dn)
            if is_q:
                t = t * scale
            qkv_ref[:, col:col + HEAD_DIM] = t.astype(BF16)
    v0 = fw + aw + kvw
    qkv_ref[:, aw + kvw:aw + 2 * kvw] = jnp.dot(
        hb, w_ref[:, v0:v0 + kvw], preferred_element_type=F32).astype(BF16)
    g0 = v0 + kvw
    gw = gate_ref.shape[-1]
    for c0 in range(0, gw, chunk):
        gate_ref[:, c0:c0 + chunk] = jnp.dot(
            hb, w_ref[:, g0 + c0:g0 + c0 + chunk], preferred_element_type=F32).astype(BF16)


def _inproj(x, shift, scale, norm_g, q_g, k_g, w_in_b, tables):
    b, s, d = x.shape
    fw = FOURIER_GROUPS * FOURIER_GROUP_DIM
    aw = N_HEADS * HEAD_DIM
    kvw = N_KV_HEADS * HEAD_DIM
    gw = 2 * d
    tm = ROW_TILE
    row = lambda w: pl.BlockSpec((None, tm, w), lambda bi, i: (bi, i, 0))
    per_b = pl.BlockSpec((None, 1, d), lambda bi, i: (bi, 0, 0))
    const = lambda w: pl.BlockSpec((1, w), lambda bi, i: (0, 0))
    tab = pl.BlockSpec((tm, LANES), lambda bi, i: (i, 0))
    return pl.pallas_call(
        functools.partial(_inproj_kernel, fw=fw, aw=aw, kvw=kvw),
        grid=(b, s // tm),
        in_specs=[row(d), per_b, per_b, const(d), const(HEAD_DIM), const(HEAD_DIM), tab, tab, tab,
                  _resident(w_in_b.shape, lambda bi, i: (0, 0))],
        out_specs=[row(fw), row(aw + 2 * kvw), row(gw)],
        out_shape=[jax.ShapeDtypeStruct((b, s, fw), F32),
                   jax.ShapeDtypeStruct((b, s, aw + 2 * kvw), BF16),
                   jax.ShapeDtypeStruct((b, s, gw), BF16)],
        compiler_params=_cparams(("parallel", "parallel")),
        name="inproj",
    )(x, shift, scale, norm_g, q_g, k_g, *tables, w_in_b)


def _pack_pair(a, b):
    ab = lax.bitcast_convert_type(a.astype(BF16).astype(F32), I32)
    bb = lax.bitcast_convert_type(b.astype(BF16).astype(F32), I32)
    return ab | lax.shift_right_logical(bb, jnp.full(bb.shape, 16, I32))


def _unpack_pair(p):
    hi = lax.bitcast_convert_type(p & jnp.int32(-65536), F32)
    lo = lax.bitcast_convert_type(lax.shift_left(p, jnp.full(p.shape, 16, I32)), F32)
    return hi, lo


def _fft_tables(s):
    n1 = FFT_N1
    n2 = s // n1
    k1 = jnp.arange(n1, dtype=I32)
    nn = (n2 * jnp.arange(n1, dtype=I32))[None, None, :] + jnp.arange(n2, dtype=I32)[:, None, None]
    ph = (k1[None, :, None] * nn) % s
    ang = ph.astype(F32) * (2.0 * math.pi / s)
    sc1 = n1 ** -0.5
    g = jnp.concatenate([jnp.cos(ang) * sc1, -jnp.sin(ang) * sc1], axis=1).astype(BF16)
    k2 = jnp.arange(n2, dtype=I32)
    ang2 = ((k2[:, None] * k2[None, :]) % n2).astype(F32) * (2.0 * math.pi / n2)
    c2, s2 = jnp.cos(ang2) * n2 ** -0.5, jnp.sin(ang2) * n2 ** -0.5
    f2 = jnp.concatenate([jnp.concatenate([c2, s2], axis=1),
                          jnp.concatenate([-s2, c2], axis=1)], axis=0).astype(BF16)
    return g, f2


def _fft_kernel(u_ref, g_ref, f2_ref, o_ref, y_scr, *, n1, n2, unroll):
    def stage1(i, carry):
        for uu in range(unroll):
            m = i * unroll + uu
            xm = u_ref[pl.ds(m, n1, stride=n2), :].astype(BF16)
            y = jnp.dot(g_ref[m], xm, preferred_element_type=F32)
            y_scr[pl.ds(pl.multiple_of(m * n1, n1), n1), :] = _pack_pair(y[:n1], y[n1:])
        return carry

    lax.fori_loop(0, n2 // unroll, stage1, 0)
    f2 = f2_ref[...]

    def stage2(i, carry):
        for uu in range(unroll):
            k1 = i * unroll + uu
            yr, yi = _unpack_pair(y_scr[pl.ds(k1, n2, stride=n1), :])
            rhs = jnp.concatenate([yr.astype(BF16), yi.astype(BF16)], axis=0)
            z = jnp.dot(f2, rhs, preferred_element_type=F32)
            o_ref[pl.ds(k1, n2, stride=n1), :] = _pack_pair(z[:n2], z[n2:])
        return carry

    lax.fori_loop(0, n1 // unroll, stage2, 0)


def _seq_fft(u):
    b, s, c = u.shape
    n1 = FFT_N1
    n2 = s // n1
    g, f2 = _fft_tables(s)
    unroll = min(FFT_UNROLL, n2)
    blk = pl.BlockSpec((None, s, LANES), lambda bi, j: (bi, 0, j))
    return pl.pallas_call(
        functools.partial(_fft_kernel, n1=n1, n2=n2, unroll=unroll),
        grid=(b, c // LANES),
        in_specs=[blk, _resident(g.shape, lambda bi, j: (0, 0, 0)), _resident(f2.shape, lambda bi, j: (0, 0))],
        out_specs=blk,
        out_shape=jax.ShapeDtypeStruct((b, s, c), I32),
        scratch_shapes=[pltpu.VMEM((s, LANES), I32)],
        compiler_params=_cparams(("parallel", "parallel")),
        name="seq_fft",
    )(u, g, f2)


def _attn_kernel(sink_ref, q_ref, kp_ref, kc_ref, kn_ref, vp_ref, vc_ref, vn_ref, o_ref, *, tq):
    i = pl.program_id(1)
    first = i == 0
    last = i == pl.num_programs(1) - 1
    nsub = tq // WINDOW
    rows = Q_PER_KV * WINDOW
    r = lax.broadcasted_iota(I32, (rows, 3 * WINDOW), 0) & (WINDOW - 1)
    c = lax.broadcasted_iota(I32, (rows, 3 * WINDOW), 1)
    band = ((c >= r) & (c < WINDOW)) | ((c >= WINDOW) & (c < 2 * WINDOW)) | ((c >= 2 * WINDOW) & (c - 2 * WINDOW <= r))
    for h in range(N_KV_HEADS):
        hs = slice(h * HEAD_DIM, (h + 1) * HEAD_DIM)
        kcat = jnp.concatenate([kp_ref[:, hs], kc_ref[:, hs], kn_ref[:, hs]], axis=0)
        vcat = jnp.concatenate([vp_ref[:, hs], vc_ref[:, hs], vn_ref[:, hs]], axis=0)
        sink = jnp.concatenate(
            [jnp.full((WINDOW, 1), sink_ref[h * Q_PER_KV + g], F32) for g in range(Q_PER_KV)], axis=0)
        for sb in range(nsub):
            q4 = jnp.concatenate(
                [q_ref[sb * WINDOW:(sb + 1) * WINDOW,
                       (h * Q_PER_KV + g) * HEAD_DIM:(h * Q_PER_KV + g + 1) * HEAD_DIM]
                 for g in range(Q_PER_KV)], axis=0)
            kw = kcat[sb * WINDOW:(sb + 3) * WINDOW]
            vw = vcat[sb * WINDOW:(sb + 3) * WINDOW]
            sc = lax.dot_general(q4, kw, (((1,), (1,)), ((), ())), preferred_element_type=F32)
            valid = band
            if sb == 0:
                valid = valid & ((c >= WINDOW) | jnp.logical_not(first))
            if sb == nsub - 1:
                valid = valid & ((c < 2 * WINDOW) | jnp.logical_not(last))
            sc = jnp.where(valid, sc, NEG)
            m = jnp.maximum(jnp.max(sc, axis=-1, keepdims=True), sink)
            p = jnp.exp(sc - m)
            denom = jnp.sum(p, axis=-1, keepdims=True) + jnp.exp(sink - m)
            o = jnp.dot(p.astype(BF16), vw, preferred_element_type=F32) / denom
            for g in range(Q_PER_KV):
                col = (h * Q_PER_KV + g) * HEAD_DIM
                o_ref[sb * WINDOW:(sb + 1) * WINDOW, col:col + HEAD_DIM] = (
                    o[g * WINDOW:(g + 1) * WINDOW].astype(BF16))


def _attention(qkv, sink):
    b, s, _ = qkv.shape
    aw = N_HEADS * HEAD_DIM
    kvw = N_KV_HEADS * HEAD_DIM
    tq = ATTN_TILE
    per = tq // WINDOW
    nblk = s // WINDOW
    kcol = aw // kvw
    vcol = kcol + 1
    halo = lambda col, off: pl.BlockSpec(
        (None, WINDOW, kvw),
        lambda bi, i, sk: (bi, jnp.clip(i * per + off, 0, nblk - 1), col))
    main = lambda col: pl.BlockSpec((None, tq, kvw), lambda bi, i, sk: (bi, i, col))
    return pl.pallas_call(
        functools.partial(_attn_kernel, tq=tq),
        grid_spec=pltpu.PrefetchScalarGridSpec(
            num_scalar_prefetch=1,
            grid=(b, s // tq),
            in_specs=[pl.BlockSpec((None, tq, aw), lambda bi, i, sk: (bi, i, 0)),
                      halo(kcol, -1), main(kcol), halo(kcol, per),
                      halo(vcol, -1), main(vcol), halo(vcol, per)],
            out_specs=pl.BlockSpec((None, tq, aw), lambda bi, i, sk: (bi, i, 0))),
        out_shape=jax.ShapeDtypeStruct((b, s, aw), BF16),
        compiler_params=_cparams(("parallel", "parallel")),
        name="attention",
    )(sink, qkv, qkv, qkv, qkv, qkv, qkv, qkv)


def _post_kernel(wp_ref, o_ref, gate_ref, x_ref, g1_ref, sh2_ref, sc2_ref, n2g_ref, cc_ref, cs_ref,
                 wfo_ref, wao_ref, wout_ref, wr_ref, br_ref, x1_ref, h2_ref, route_ref, cnt_ref):
    tm = x_ref.shape[0]
    d = x_ref.shape[1]

    @pl.when(pl.program_id(0) == 0)
    def _():
        cnt_ref[...] = jnp.zeros_like(cnt_ref)

    re, im = _unpack_pair(wp_ref[...])
    re, im = re.astype(BF16), im.astype(BF16)
    cc, cs = cc_ref[...], cs_ref[...]
    gd = FOURIER_GROUP_DIM
    fm = jnp.concatenate(
        [(jnp.dot(re[:, g * gd:(g + 1) * gd], cc, preferred_element_type=F32)
          + jnp.dot(im[:, g * gd:(g + 1) * gd], cs, preferred_element_type=F32)).astype(BF16)
         for g in range(FOURIER_GROUPS)], axis=1)
    y_f = jnp.dot(fm, wfo_ref[...], preferred_element_type=F32)
    y_a = jnp.dot(o_ref[...], wao_ref[...], preferred_element_type=F32)
    gates = jax.nn.sigmoid(gate_ref[...].astype(F32))
    merged = (gates[:, :d] * y_f + gates[:, d:] * y_a).astype(BF16)
    x1 = x_ref[...] + g1_ref[...] * jnp.dot(merged, wout_ref[...], preferred_element_type=F32)
    x1_ref[...] = x1
    ms = jnp.mean(x1 * x1, axis=-1, keepdims=True)
    h2 = (x1 * lax.rsqrt(ms + EPS) * n2g_ref[...]) * (1.0 + sc2_ref[...]) + sh2_ref[...]
    h2_ref[...] = h2

    logits = jnp.dot(h2, wr_ref[...], precision=lax.Precision.HIGHEST,
                     preferred_element_type=F32) + br_ref[...]
    lane = lax.broadcasted_iota(I32, (tm, ROUTE_LANES), 1)
    big = jnp.int32(ROUTE_LANES)
    is_g = lane < N_GROUPS
    gl = jnp.where(is_g, logits, NEG)
    gmax = jnp.max(gl, axis=-1, keepdims=True)
    gidx = jnp.min(jnp.where(gl == gmax, lane, big), axis=-1, keepdims=True)
    p_g = 1.0 / jnp.sum(jnp.where(is_g, jnp.exp(gl - gmax), 0.0), axis=-1, keepdims=True)
    in_grp = (lane >= N_GROUPS) & (lane < N_GROUPS + N_EXPERTS) & (
        lax.shift_right_logical(lane - N_GROUPS, jnp.full(lane.shape, 3, I32)) == gidx)
    el = jnp.where(in_grp, logits, NEG)
    e1v = jnp.max(el, axis=-1, keepdims=True)
    e1i = jnp.min(jnp.where(el == e1v, lane, big), axis=-1, keepdims=True)
    el2 = jnp.where(lane == e1i, NEG, el)
    e2v = jnp.max(el2, axis=-1, keepdims=True)
    e2i = jnp.min(jnp.where(el2 == e2v, lane, big), axis=-1, keepdims=True)
    t = jnp.exp(e2v - e1v)
    w1 = p_g / (1.0 + t)
    w2 = w1 * t
    sel1, sel2 = lane == e1i, lane == e2i
    member = jnp.where(sel1 | sel2, 1.0, 0.0)
    rr = lax.broadcasted_iota(I32, (tm, tm), 0)
    cc_i = lax.broadcasted_iota(I32, (tm, tm), 1)
    tri = jnp.where(cc_i < rr, 1.0, 0.0).astype(BF16)
    prefix = jnp.dot(tri, member.astype(BF16), preferred_element_type=F32) + cnt_ref[...]
    rank1 = jnp.sum(jnp.where(sel1, prefix, 0.0), axis=-1, keepdims=True)
    rank2 = jnp.sum(jnp.where(sel2, prefix, 0.0), axis=-1, keepdims=True)
    cnt_ref[...] = cnt_ref[...] + jnp.sum(member, axis=0, keepdims=True)
    e1f = (e1i - N_GROUPS).astype(F32)
    e2f = (e2i - N_GROUPS).astype(F32)
    route = jnp.zeros((tm, ROUTE_LANES), F32)
    for k, val in enumerate((e1f, e2f, rank1, rank2, w1, w2)):
        route = jnp.where(lane == k, val, route)
    route_ref[...] = route


def _post(wp, attn, gates, x, g1, sh2, sc2, norm2_g, cc, cs, wfo_b, wao_b, wout_b, wr, br):
    b, s, d = x.shape
    t = b * s
    tm = ROW_TILE
    per_b_tiles = s // tm
    flat = lambda a: a.reshape(t, a.shape[-1])
    row = lambda w: pl.BlockSpec((tm, w), lambda i: (i, 0))
    per_b = pl.BlockSpec((None, 1, d), lambda i: (i // per_b_tiles, 0, 0))
    const = lambda a: _resident(a.shape, lambda i: (0,) * a.ndim)
    fw = wp.shape[-1]
    aw = attn.shape[-1]
    return pl.pallas_call(
        _post_kernel,
        grid=(t // tm,),
        in_specs=[row(fw), row(aw), row(2 * d), row(d), per_b, per_b, per_b,
                  const(norm2_g), const(cc), const(cs), const(wfo_b), const(wao_b), const(wout_b),
                  const(wr), const(br)],
        out_specs=[row(d), row(d), row(ROUTE_LANES), pl.BlockSpec((1, ROUTE_LANES), lambda i: (0, 0))],
        out_shape=[jax.ShapeDtypeStruct((t, d), F32), jax.ShapeDtypeStruct((t, d), F32),
                   jax.ShapeDtypeStruct((t, ROUTE_LANES), F32), jax.ShapeDtypeStruct((1, ROUTE_LANES), F32)],
        compiler_params=_cparams(("arbitrary",)),
        name="post_router",
    )(flat(wp), flat(attn), flat(gates), flat(x), g1, sh2, sc2, norm2_g, cc, cs, wfo_b, wao_b, wout_b, wr, br)


def _expert_kernel(rowtok_ref, blke_ref, nused_ref, h_hbm, w1_ref, w3_ref, w2_ref, y_ref,
                   xbuf, sem, w1b, w3b, w2b, *, rows):
    j = pl.program_id(0)
    nblk = pl.num_programs(0)
    slot = j % 2

    def start_gather(blk, sl):
        def body(r, carry):
            tok = rowtok_ref[blk * rows + r]
            pltpu.make_async_copy(h_hbm.at[pl.ds(tok, 1), :], xbuf.at[sl, pl.ds(r, 1), :], sem.at[sl]).start()
            return carry
        lax.fori_loop(0, rows, body, 0)

    @pl.when(j == 0)
    def _():
        start_gather(0, 0)

    @pl.when(j + 1 < nblk)
    def _():
        start_gather(j + 1, 1 - slot)

    e = blke_ref[j]
    e_prev = blke_ref[jnp.maximum(j - 1, 0)]

    @pl.when((j == 0) | (e != e_prev))
    def _():
        w1b[...] = w1_ref[...].astype(BF16)
        w3b[...] = w3_ref[...].astype(BF16)
        w2b[...] = w2_ref[...].astype(BF16)

    pltpu.make_async_copy(h_hbm.at[pl.ds(0, rows), :], xbuf.at[slot], sem.at[slot]).wait()

    @pl.when(j < nused_ref[0])
    def _():
        xb = xbuf[slot].astype(BF16)
        a = jnp.dot(xb, w1b[...], preferred_element_type=F32)
        g = jnp.dot(xb, w3b[...], preferred_element_type=F32)
        hid = (a * jax.nn.sigmoid(a) * g).astype(BF16)
        y_ref[...] = jnp.dot(hid, w2b[...], preferred_element_type=F32)

    @pl.when(j >= nused_ref[0])
    def _():
        y_ref[...] = jnp.zeros_like(y_ref)


def _experts(h2, row_tok, block_e, n_used, w1, w3, w2):
    t, d = h2.shape
    p = row_tok.shape[0]
    rows = MOE_ROWS
    f = w1.shape[-1]
    return pl.pallas_call(
        functools.partial(_expert_kernel, rows=rows),
        grid_spec=pltpu.PrefetchScalarGridSpec(
            num_scalar_prefetch=3,
            grid=(p // rows,),
            in_specs=[pl.BlockSpec(memory_space=pl.ANY),
                      pl.BlockSpec((None, d, f), lambda j, rt, be, nu: (be[j], 0, 0)),
                      pl.BlockSpec((None, d, f), lambda j, rt, be, nu: (be[j], 0, 0)),
                      pl.BlockSpec((None, f, d), lambda j, rt, be, nu: (be[j], 0, 0))],
            out_specs=pl.BlockSpec((rows, d), lambda j, rt, be, nu: (j, 0)),
            scratch_shapes=[pltpu.VMEM((2, rows, d), F32), pltpu.SemaphoreType.DMA((2,)),
                            pltpu.VMEM((d, f), BF16), pltpu.VMEM((d, f), BF16), pltpu.VMEM((f, d), BF16)]),
        out_shape=jax.ShapeDtypeStruct((p, d), F32),
        compiler_params=_cparams(("arbitrary",)),
        name="experts",
    )(row_tok, block_e, n_used, h2, w1, w3, w2)


def _combine_kernel(dest_ref, y_hbm, x1_ref, route_ref, g2_ref, o_ref, ybuf, sem, *, tm):
    i = pl.program_id(0)
    n = pl.num_programs(0)
    slot = i % 2

    def start_gather(tile, sl):
        def body(r, carry):
            base = (tile * tm + r) * TOP_K
            for k in range(TOP_K):
                pltpu.make_async_copy(y_hbm.at[pl.ds(dest_ref[base + k], 1), :],
                                      ybuf.at[sl, pl.ds(k * tm + r, 1), :], sem.at[sl]).start()
            return carry
        lax.fori_loop(0, tm, body, 0)

    @pl.when(i == 0)
    def _():
        start_gather(0, 0)

    @pl.when(i + 1 < n)
    def _():
        start_gather(i + 1, 1 - slot)

    pltpu.make_async_copy(y_hbm.at[pl.ds(0, TOP_K * tm), :], ybuf.at[slot], sem.at[slot]).wait()
    route = route_ref[...]
    w1 = route[:, 4:5]
    w2 = route[:, 5:6]
    moe = w1 * ybuf[slot, 0:tm, :] + w2 * ybuf[slot, tm:2 * tm, :]
    o_ref[...] = x1_ref[...] + g2_ref[...] * moe


def _combine(y, dest, x1, route, g2, s):
    t, d = x1.shape
    tm = ROW_TILE
    per_b_tiles = s // tm
    return pl.pallas_call(
        functools.partial(_combine_kernel, tm=tm),
        grid_spec=pltpu.PrefetchScalarGridSpec(
            num_scalar_prefetch=1,
            grid=(t // tm,),
            in_specs=[pl.BlockSpec(memory_space=pl.ANY),
                      pl.BlockSpec((tm, d), lambda i, ds: (i, 0)),
                      pl.BlockSpec((tm, ROUTE_LANES), lambda i, ds: (i, 0)),
                      pl.BlockSpec((None, 1, d), lambda i, ds: (i // per_b_tiles, 0, 0))],
            out_specs=pl.BlockSpec((tm, d), lambda i, ds: (i, 0)),
            scratch_shapes=[pltpu.VMEM((2, TOP_K * tm, d), F32), pltpu.SemaphoreType.DMA((2,))]),
        out_shape=jax.ShapeDtypeStruct((t, d), F32),
        compiler_params=_cparams(("arbitrary",)),
        name="combine",
    )(dest, y, x1, route, g2)


def _dispatch_plan(route, counts, t):
    rows = MOE_ROWS
    e = route[:, 0:TOP_K].astype(I32)
    rank = route[:, TOP_K:2 * TOP_K].astype(I32)
    cnt = counts[0, N_GROUPS:N_GROUPS + N_EXPERTS].astype(I32)
    padded = (cnt + rows - 1) // rows * rows
    pad_end = jnp.cumsum(padded)
    pad_start = pad_end - padded
    dest = (pad_start[e] + rank).reshape(t * TOP_K)
    n_blocks = (t * TOP_K + N_EXPERTS * (rows - 1) + rows - 1) // rows
    tok = jnp.repeat(jnp.arange(t, dtype=I32), TOP_K)
    row_tok = jnp.zeros((n_blocks * rows,), I32).at[dest].set(tok)
    block_e = jnp.minimum(
        jnp.searchsorted(pad_end, jnp.arange(n_blocks, dtype=I32) * rows, side="right"), N_EXPERTS - 1).astype(I32)
    n_used = (pad_end[-1:] // rows).astype(I32)
    return dest, row_tok, block_e, n_used


def _trunk(x, mod, p):
    b, s, d = x.shape
    sh1, sc1, g1, sh2, sc2, g2 = [m.reshape(b, 1, d) for m in jnp.split(mod, 6, axis=-1)]
    u, qkv, gates = _inproj(x, sh1, sc1, p["norm1_g"], p["q_norm_g"], p["k_norm_g"], p["w_in"], _rope_tables(s))
    wp = _seq_fft(u)
    attn = _attention(qkv, p["sink"])
    x1, h2, route, counts = _post(wp, attn, gates, x, g1, sh2, sc2, p["norm2_g"], p["cc"], p["cs"],
                                  p["w_fourier_out"], p["w_attn_out"], p["w_out"], p["w_router"], p["b_router"])
    t = b * s
    dest, row_tok, block_e, n_used = _dispatch_plan(route, counts, t)
    y = _experts(h2, row_tok, block_e, n_used, p["w1"], p["w3"], p["w2"])
    out = _combine(y, dest, x1, route, g2, s)
    return out.reshape(b, s, d)


def _channel_dft():
    n = FOURIER_GROUP_DIM
    k = jnp.arange(n, dtype=I32)
    ang = ((k[:, None] * k[None, :]) % n).astype(F32) * (2.0 * math.pi / n)
    return (jnp.cos(ang) * n ** -0.5).astype(BF16), (jnp.sin(ang) * n ** -0.5).astype(BF16)


def kernel(x_prompt, x_sample, c_prompt, c_sample, w_ada, b_ada, norm1_g, w_in, q_norm_g, k_norm_g, sink,
           w_fourier_out, w_attn_out, w_out, norm2_g, w_group, b_group, w_expert, b_expert, w1, w3, w2):
    depth = w_ada.shape[0]
    d = x_prompt.shape[-1]
    bp = c_prompt.shape[0]
    bs = c_sample.shape[0]
    cc, cs = _channel_dft()
    xp, xs = x_prompt, x_sample
    for l in range(depth):
        c_all = jnp.concatenate([c_prompt, c_sample, jnp.zeros((8 - (bp + bs) % 8, d), F32)], axis=0)
        mod = _adaln(c_all, w_ada[l], b_ada[l])
        pad = ROUTE_LANES - N_GROUPS - N_EXPERTS
        p = {
            "norm1_g": norm1_g[l].reshape(1, d), "norm2_g": norm2_g[l].reshape(1, d),
            "q_norm_g": q_norm_g[l].reshape(1, HEAD_DIM), "k_norm_g": k_norm_g[l].reshape(1, HEAD_DIM),
            "sink": sink[l], "w_in": w_in[l].astype(BF16),
            "w_fourier_out": w_fourier_out[l].astype(BF16), "w_attn_out": w_attn_out[l].astype(BF16),
            "w_out": w_out[l].astype(BF16), "cc": cc, "cs": cs,
            "w_router": jnp.concatenate([w_group[l], w_expert[l], jnp.zeros((d, pad), F32)], axis=1),
            "b_router": jnp.concatenate([b_group[l], b_expert[l], jnp.zeros((pad,), F32)]).reshape(1, ROUTE_LANES),
            "w1": w1[l], "w3": w3[l], "w2": w2[l],
        }
        xp = _trunk(xp, mod[:bp], p)
        xs = _trunk(xs, mod[bp:bp + bs], p)
    return xp, xs
```

```python
import functools
import math

import jax
import jax.numpy as jnp
from jax import lax
from jax.experimental import pallas as pl
from jax.experimental.pallas import tpu as pltpu

F32 = jnp.float32
BF16 = jnp.bfloat16
I32 = jnp.int32

HEAD_DIM = 128
N_KV_HEADS = 2
Q_PER_KV = 4
N_HEADS = N_KV_HEADS * Q_PER_KV
ROT_DIM = 32
ROPE_THETA = 500000.0
WINDOW = 128
FOURIER_GROUPS = 4
FOURIER_GROUP_DIM = 256
N_GROUPS = 8
EXPERTS_PER_GROUP = 8
N_EXPERTS = N_GROUPS * EXPERTS_PER_GROUP
TOP_K = 2
EPS = 1e-6

LANES = 128
V7X_VMEM_LIMIT = 56 * 1024 * 1024

ROW_TILE = 256
ATTN_TILE = 512
FFT_N1 = 128
FFT_UNROLL = 4
MOE_ROWS = 128
ROUTE_LANES = 128
RANK_BITS = 17
DMA_GROUP = 8
NEG = -1e30


def _cparams(sem, **kw):
    return pltpu.CompilerParams(dimension_semantics=sem, vmem_limit_bytes=V7X_VMEM_LIMIT, **kw)


def _resident(shape, index_map):
    return pl.BlockSpec(shape, index_map, pipeline_mode=pl.Buffered(1))


def _sigmoid(x):
    return 0.5 * jnp.tanh(0.5 * x) + 0.5


def _adaln_kernel(c_ref, w_ref, b_ref, o_ref):
    c = c_ref[...]
    s = c * _sigmoid(c)
    o_ref[...] = jnp.dot(s, w_ref[...], precision=lax.Precision.HIGHEST,
                         preferred_element_type=F32) + b_ref[...]


def _adaln(c, w_ada, b_ada):
    r, d = c.shape
    n = w_ada.shape[1]
    tn = 1024
    return pl.pallas_call(
        _adaln_kernel,
        grid=(n // tn,),
        in_specs=[pl.BlockSpec((r, d), lambda j: (0, 0)),
                  pl.BlockSpec((d, tn), lambda j: (0, j)),
                  pl.BlockSpec((1, tn), lambda j: (0, j))],
        out_specs=pl.BlockSpec((r, tn), lambda j: (0, j)),
        out_shape=jax.ShapeDtypeStruct((r, n), F32),
        compiler_params=_cparams(("parallel",)),
        name="adaln",
    )(c, w_ada, b_ada.reshape(1, n))


def _rope_tables(s):
    half = ROT_DIM // 2
    inv_freq = ROPE_THETA ** (-jnp.arange(0, ROT_DIM, 2, dtype=F32) / ROT_DIM)
    ang = jnp.arange(s, dtype=F32)[:, None] * inv_freq[None, :]
    cos, sin = jnp.cos(ang), jnp.sin(ang)
    pad = jnp.zeros((s, LANES - ROT_DIM), F32)
    cos_f = jnp.concatenate([cos, cos, jnp.ones((s, LANES - ROT_DIM), F32)], axis=1)
    sin_up = jnp.concatenate([-sin, jnp.zeros((s, half), F32), pad], axis=1)
    sin_dn = jnp.concatenate([jnp.zeros((s, half), F32), sin, pad], axis=1)
    return cos_f, sin_up, sin_dn


def _inproj_kernel(x_ref, sh_ref, sc_ref, g_ref, qg_ref, kg_ref, cos_ref, sup_ref, sdn_ref, w_ref,
                   u_ref, qkv_ref, gate_ref, *, fw, aw, kvw):
    x = x_ref[...]
    ms = jnp.mean(x * x, axis=-1, keepdims=True)
    h = (x * lax.rsqrt(ms + EPS) * g_ref[...]) * (1.0 + sc_ref[...]) + sh_ref[...]
    hb = h.astype(BF16)
    chunk = 512
    for c0 in range(0, fw, chunk):
        u_ref[:, c0:c0 + chunk] = jnp.dot(hb, w_ref[:, c0:c0 + chunk], preferred_element_type=F32)
    cos_f, sin_up, sin_dn = cos_ref[...], sup_ref[...], sdn_ref[...]
    half = ROT_DIM // 2
    scale = HEAD_DIM ** -0.5
    for c0 in list(range(0, aw, chunk)) + [aw]:
        width = chunk if c0 < aw else kvw
        acc = jnp.dot(hb, w_ref[:, fw + c0:fw + c0 + width], preferred_element_type=F32)
        for hh in range(width // HEAD_DIM):
            col = c0 + hh * HEAD_DIM
            t = acc[:, hh * HEAD_DIM:(hh + 1) * HEAD_DIM]
            is_q = col < aw
            gain = qg_ref[...] if is_q else kg_ref[...]
            t = t * lax.rsqrt(jnp.mean(t * t, axis=-1, keepdims=True) + EPS) * gain
            t = (t * cos_f + pltpu.roll(t, LANES - half, axis=1) * sin_up
                 + pltpu.roll(t, half, axis=1) * sin_dn)
            if is_q:
                t = t * scale
            qkv_ref[:, col:col + HEAD_DIM] = t.astype(BF16)
    v0 = fw + aw + kvw
    qkv_ref[:, aw + kvw:aw + 2 * kvw] = jnp.dot(
        hb, w_ref[:, v0:v0 + kvw], preferred_element_type=F32).astype(BF16)
    g0 = v0 + kvw
    gw = gate_ref.shape[-1]
    for c0 in range(0, gw, chunk):
        gate_ref[:, c0:c0 + chunk] = jnp.dot(
            hb, w_ref[:, g0 + c0:g0 + c0 + chunk], preferred_element_type=F32).astype(BF16)


def _inproj(x, shift, scale, norm_g, q_g, k_g, w_in_b, tables):
    b, s, d = x.shape
    fw = FOURIER_GROUPS * FOURIER_GROUP_DIM
    aw = N_HEADS * HEAD_DIM
    kvw = N_KV_HEADS * HEAD_DIM
    gw = 2 * d
    tm = ROW_TILE
    row = lambda w: pl.BlockSpec((None, tm, w), lambda bi, i: (bi, i, 0))
    per_b = pl.BlockSpec((None, 1, d), lambda bi, i: (bi, 0, 0))
    const = lambda w: pl.BlockSpec((1, w), lambda bi, i: (0, 0))
    tab = pl.BlockSpec((tm, LANES), lambda bi, i: (i, 0))
    return pl.pallas_call(
        functools.partial(_inproj_kernel, fw=fw, aw=aw, kvw=kvw),
        grid=(b, s // tm),
        in_specs=[row(d), per_b, per_b, const(d), const(HEAD_DIM), const(HEAD_DIM), tab, tab, tab,
                  _resident(w_in_b.shape, lambda bi, i: (0, 0))],
        out_specs=[row(fw), row(aw + 2 * kvw), row(gw)],
        out_shape=[jax.ShapeDtypeStruct((b, s, fw), F32),
                   jax.ShapeDtypeStruct((b, s, aw + 2 * kvw), BF16),
                   jax.ShapeDtypeStruct((b, s, gw), BF16)],
        compiler_params=_cparams(("parallel", "parallel")),
        name="inproj",
    )(x, shift, scale, norm_g, q_g, k_g, *tables, w_in_b)


def _pack_pair(a, b):
    ab = lax.bitcast_convert_type(a.astype(BF16).astype(F32), I32)
    bb = lax.bitcast_convert_type(b.astype(BF16).astype(F32), I32)
    return ab | lax.shift_right_logical(bb, jnp.full(bb.shape, 16, I32))


def _unpack_pair(p):
    hi = lax.bitcast_convert_type(p & jnp.int32(-65536), F32)
    lo = lax.bitcast_convert_type(lax.shift_left(p, jnp.full(p.shape, 16, I32)), F32)
    return hi, lo


def _fft_tables(s):
    n1 = FFT_N1
    n2 = s // n1
    k1 = jnp.arange(n1, dtype=I32)
    nn = (n2 * jnp.arange(n1, dtype=I32))[None, None, :] + jnp.arange(n2, dtype=I32)[:, None, None]
    ph = (k1[None, :, None] * nn) % s
    ang = ph.astype(F32) * (2.0 * math.pi / s)
    sc1 = n1 ** -0.5
    g = jnp.concatenate([jnp.cos(ang) * sc1, -jnp.sin(ang) * sc1], axis=1).astype(BF16)
    k2 = jnp.arange(n2, dtype=I32)
    ang2 = ((k2[:, None] * k2[None, :]) % n2).astype(F32) * (2.0 * math.pi / n2)
    c2, s2 = jnp.cos(ang2) * n2 ** -0.5, jnp.sin(ang2) * n2 ** -0.5
    f2 = jnp.concatenate([jnp.concatenate([c2, s2], axis=1),
                          jnp.concatenate([-s2, c2], axis=1)], axis=0).astype(BF16)
    return g, f2


def _fft_kernel(u_ref, g_ref, f2_ref, o_ref, y_scr, *, n1, n2, unroll):
    def stage1(i, carry):
        for uu in range(unroll):
            m = i * unroll + uu
            xm = u_ref[pl.ds(m, n1, stride=n2), :].astype(BF16)
            y = jnp.dot(g_ref[m], xm, preferred_element_type=F32)
            y_scr[pl.ds(pl.multiple_of(m * n1, n1), n1), :] = _pack_pair(y[:n1], y[n1:])
        return carry

    lax.fori_loop(0, n2 // unroll, stage1, 0)
    f2 = f2_ref[...]

    def stage2(i, carry):
        for uu in range(unroll):
            k1 = i * unroll + uu
            yr, yi = _unpack_pair(y_scr[pl.ds(k1, n2, stride=n1), :])
            rhs = jnp.concatenate([yr.astype(BF16), yi.astype(BF16)], axis=0)
            z = jnp.dot(f2, rhs, preferred_element_type=F32)
            o_ref[pl.ds(k1, n2, stride=n1), :] = _pack_pair(z[:n2], z[n2:])
        return carry

    lax.fori_loop(0, n1 // unroll, stage2, 0)


def _seq_fft(u):
    b, s, c = u.shape
    n1 = FFT_N1
    n2 = s // n1
    g, f2 = _fft_tables(s)
    unroll = min(FFT_UNROLL, n2)
    blk = pl.BlockSpec((None, s, LANES), lambda bi, j: (bi, 0, j))
    return pl.pallas_call(
        functools.partial(_fft_kernel, n1=n1, n2=n2, unroll=unroll),
        grid=(b, c // LANES),
        in_specs=[blk, _resident(g.shape, lambda bi, j: (0, 0, 0)), _resident(f2.shape, lambda bi, j: (0, 0))],
        out_specs=blk,
        out_shape=jax.ShapeDtypeStruct((b, s, c), I32),
        scratch_shapes=[pltpu.VMEM((s, LANES), I32)],
        compiler_params=_cparams(("parallel", "parallel")),
        name="seq_fft",
    )(u, g, f2)


def _attn_kernel(sink_ref, q_ref, kp_ref, kc_ref, kn_ref, vp_ref, vc_ref, vn_ref, o_ref, *, tq):
    i = pl.program_id(1)
    first = i == 0
    last = i == pl.num_programs(1) - 1
    nsub = tq // WINDOW
    rows = Q_PER_KV * WINDOW
    r = lax.broadcasted_iota(I32, (rows, 3 * WINDOW), 0) & (WINDOW - 1)
    c = lax.broadcasted_iota(I32, (rows, 3 * WINDOW), 1)
    band = ((c >= r) & (c < WINDOW)) | ((c >= WINDOW) & (c < 2 * WINDOW)) | ((c >= 2 * WINDOW) & (c - 2 * WINDOW <= r))
    for h in range(N_KV_HEADS):
        hs = slice(h * HEAD_DIM, (h + 1) * HEAD_DIM)
        kcat = jnp.concatenate([kp_ref[:, hs], kc_ref[:, hs], kn_ref[:, hs]], axis=0)
        vcat = jnp.concatenate([vp_ref[:, hs], vc_ref[:, hs], vn_ref[:, hs]], axis=0)
        sink = jnp.concatenate(
            [jnp.full((WINDOW, 1), sink_ref[h * Q_PER_KV + g], F32) for g in range(Q_PER_KV)], axis=0)
        for sb in range(nsub):
            q4 = jnp.concatenate(
                [q_ref[sb * WINDOW:(sb + 1) * WINDOW,
                       (h * Q_PER_KV + g) * HEAD_DIM:(h * Q_PER_KV + g + 1) * HEAD_DIM]
                 for g in range(Q_PER_KV)], axis=0)
            kw = kcat[sb * WINDOW:(sb + 3) * WINDOW]
            vw = vcat[sb * WINDOW:(sb + 3) * WINDOW]
            sc = lax.dot_general(q4, kw, (((1,), (1,)), ((), ())), preferred_element_type=F32)
            valid = band
            if sb == 0:
                valid = valid & ((c >= WINDOW) | jnp.logical_not(first))
            if sb == nsub - 1:
                valid = valid & ((c < 2 * WINDOW) | jnp.logical_not(last))
            sc = jnp.where(valid, sc, NEG)
            m = jnp.maximum(jnp.max(sc, axis=-1, keepdims=True), sink)
            p = jnp.exp(sc - m)
            denom = jnp.sum(p, axis=-1, keepdims=True) + jnp.exp(sink - m)
            o = jnp.dot(p.astype(BF16), vw, preferred_element_type=F32) / denom
            for g in range(Q_PER_KV):
                col = (h * Q_PER_KV + g) * HEAD_DIM
                o_ref[sb * WINDOW:(sb + 1) * WINDOW, col:col + HEAD_DIM] = (
                    o[g * WINDOW:(g + 1) * WINDOW].astype(BF16))


def _attention(qkv, sink):
    b, s, _ = qkv.shape
    aw = N_HEADS * HEAD_DIM
    kvw = N_KV_HEADS * HEAD_DIM
    tq = ATTN_TILE
    per = tq // WINDOW
    nblk = s // WINDOW
    kcol = aw // kvw
    vcol = kcol + 1
    halo = lambda col, off: pl.BlockSpec(
        (None, WINDOW, kvw),
        lambda bi, i, sk: (bi, jnp.clip(i * per + off, 0, nblk - 1), col))
    main = lambda col: pl.BlockSpec((None, tq, kvw), lambda bi, i, sk: (bi, i, col))
    return pl.pallas_call(
        functools.partial(_attn_kernel, tq=tq),
        grid_spec=pltpu.PrefetchScalarGridSpec(
            num_scalar_prefetch=1,
            grid=(b, s // tq),
            in_specs=[pl.BlockSpec((None, tq, aw), lambda bi, i, sk: (bi, i, 0)),
                      halo(kcol, -1), main(kcol), halo(kcol, per),
                      halo(vcol, -1), main(vcol), halo(vcol, per)],
            out_specs=pl.BlockSpec((None, tq, aw), lambda bi, i, sk: (bi, i, 0))),
        out_shape=jax.ShapeDtypeStruct((b, s, aw), BF16),
        compiler_params=_cparams(("parallel", "parallel")),
        name="attention",
    )(sink, qkv, qkv, qkv, qkv, qkv, qkv, qkv)


ROW_WORDS = 1024
ROW_SUB = ROW_WORDS // LANES


def _store_tile_rows(ref, val, base=0, lead=()):
    r = val.shape[0]
    for j in range(ROW_SUB):
        ref[lead + (pl.ds(base * ROW_SUB + j, r, stride=ROW_SUB), slice(None))] = val[:, j * LANES:(j + 1) * LANES]


def _load_tile_rows(ref, base, r, lead=()):
    return jnp.concatenate(
        [ref[lead + (pl.ds(base * ROW_SUB + j, r, stride=ROW_SUB), slice(None))] for j in range(ROW_SUB)], axis=1)


def _tile_row(ref, row):
    return ref.at[pl.ds(pl.multiple_of(row * ROW_SUB, ROW_SUB), ROW_SUB), :]


def _post_kernel(wp_ref, o_ref, gate_ref, x_ref, g1_ref, sh2_ref, sc2_ref, n2g_ref, cc_ref, cs_ref,
                 wfo_ref, wao_ref, wout_ref, wrh_ref, wrl_ref, br_ref, x1_ref, h2_ref, route_ref, cnt_ref):
    tm = x_ref.shape[0]
    d = x_ref.shape[1]

    @pl.when(pl.program_id(0) == 0)
    def _():
        cnt_ref[...] = jnp.zeros_like(cnt_ref)

    re, im = _unpack_pair(wp_ref[...])
    re, im = re.astype(BF16), im.astype(BF16)
    cc, cs = cc_ref[...], cs_ref[...]
    gd = FOURIER_GROUP_DIM
    fm = jnp.concatenate(
        [(jnp.dot(re[:, g * gd:(g + 1) * gd], cc, preferred_element_type=F32)
          + jnp.dot(im[:, g * gd:(g + 1) * gd], cs, preferred_element_type=F32)).astype(BF16)
         for g in range(FOURIER_GROUPS)], axis=1)
    y_f = jnp.dot(fm, wfo_ref[...], preferred_element_type=F32)
    y_a = jnp.dot(o_ref[...], wao_ref[...], preferred_element_type=F32)
    merged = (_sigmoid(gate_ref[:, :d].astype(F32)) * y_f
              + _sigmoid(gate_ref[:, d:].astype(F32)) * y_a).astype(BF16)
    x1 = x_ref[...] + g1_ref[...] * jnp.dot(merged, wout_ref[...], preferred_element_type=F32)
    x1_ref[...] = x1
    ms = jnp.mean(x1 * x1, axis=-1, keepdims=True)
    h2 = (x1 * lax.rsqrt(ms + EPS) * n2g_ref[...]) * (1.0 + sc2_ref[...]) + sh2_ref[...]
    half = d // 2
    _store_tile_rows(h2_ref, _pack_pair(h2[:, :half], h2[:, half:]))

    h_hi = h2.astype(BF16)
    h_lo = (h2 - h_hi.astype(F32)).astype(BF16)
    logits = (jnp.dot(h_hi, wrh_ref[...], preferred_element_type=F32)
              + jnp.dot(h_lo, wrh_ref[...], preferred_element_type=F32)
              + jnp.dot(h_hi, wrl_ref[...], preferred_element_type=F32)) + br_ref[...]
    lane = lax.broadcasted_iota(I32, (tm, ROUTE_LANES), 1)
    big = jnp.int32(ROUTE_LANES)
    is_g = lane < N_GROUPS
    gl = jnp.where(is_g, logits, NEG)
    gmax = jnp.max(gl, axis=-1, keepdims=True)
    gidx = jnp.min(jnp.where(gl == gmax, lane, big), axis=-1, keepdims=True)
    p_g = 1.0 / jnp.sum(jnp.where(is_g, jnp.exp(gl - gmax), 0.0), axis=-1, keepdims=True)
    in_grp = (lane >= N_GROUPS) & (lane < N_GROUPS + N_EXPERTS) & (
        lax.shift_right_logical(lane - N_GROUPS, jnp.full(lane.shape, 3, I32)) == gidx)
    el = jnp.where(in_grp, logits, NEG)
    e1v = jnp.max(el, axis=-1, keepdims=True)
    e1i = jnp.min(jnp.where(el == e1v, lane, big), axis=-1, keepdims=True)
    el2 = jnp.where(lane == e1i, NEG, el)
    e2v = jnp.max(el2, axis=-1, keepdims=True)
    e2i = jnp.min(jnp.where(el2 == e2v, lane, big), axis=-1, keepdims=True)
    t = jnp.exp(e2v - e1v)
    w1 = p_g / (1.0 + t)
    w2 = w1 * t
    sel1, sel2 = lane == e1i, lane == e2i
    member = jnp.where(sel1 | sel2, 1.0, 0.0)
    rr = lax.broadcasted_iota(I32, (tm, tm), 0)
    cc_i = lax.broadcasted_iota(I32, (tm, tm), 1)
    tri = jnp.where(cc_i < rr, 1.0, 0.0).astype(BF16)
    prefix = jnp.dot(tri, member.astype(BF16), preferred_element_type=F32) + cnt_ref[...]
    rank1 = jnp.sum(jnp.where(sel1, prefix, 0.0), axis=-1, keepdims=True)
    rank2 = jnp.sum(jnp.where(sel2, prefix, 0.0), axis=-1, keepdims=True)
    cnt_ref[...] = cnt_ref[...] + jnp.sum(member, axis=0, keepdims=True)
    key_scale = float(1 << RANK_BITS)
    key1 = (e1i - N_GROUPS).astype(F32) * key_scale + rank1
    key2 = (e2i - N_GROUPS).astype(F32) * key_scale + rank2
    route = jnp.zeros((tm, ROUTE_LANES), F32)
    for k, val in enumerate((key1, key2, w1, w2)):
        route = jnp.where(lane == k, val, route)
    route_ref[...] = route


def _post(wp, attn, gates, x, g1, sh2, sc2, norm2_g, cc, cs, wfo_b, wao_b, wout_b, wr_hi, wr_lo, br):
    b, s, d = x.shape
    t = b * s
    tm = ROW_TILE
    per_b_tiles = s // tm
    flat = lambda a: a.reshape(t, a.shape[-1])
    row = lambda w: pl.BlockSpec((tm, w), lambda i: (i, 0))
    per_b = pl.BlockSpec((None, 1, d), lambda i: (i // per_b_tiles, 0, 0))
    const = lambda a: _resident(a.shape, lambda i: (0,) * a.ndim)
    fw = wp.shape[-1]
    aw = attn.shape[-1]
    return pl.pallas_call(
        _post_kernel,
        grid=(t // tm,),
        in_specs=[row(fw), row(aw), row(2 * d), row(d), per_b, per_b, per_b,
                  const(norm2_g), const(cc), const(cs), const(wfo_b), const(wao_b), const(wout_b),
                  const(wr_hi), const(wr_lo), const(br)],
        out_specs=[row(d), pl.BlockSpec((tm * ROW_SUB, LANES), lambda i: (i, 0)), row(ROUTE_LANES),
                   pl.BlockSpec((1, ROUTE_LANES), lambda i: (0, 0))],
        out_shape=[jax.ShapeDtypeStruct((t, d), F32), jax.ShapeDtypeStruct((t * ROW_SUB, LANES), I32),
                   jax.ShapeDtypeStruct((t, ROUTE_LANES), F32), jax.ShapeDtypeStruct((1, ROUTE_LANES), F32)],
        compiler_params=_cparams(("arbitrary",)),
        name="post_router",
    )(flat(wp), flat(attn), flat(gates), flat(x), g1, sh2, sc2, norm2_g, cc, cs, wfo_b, wao_b, wout_b,
      wr_hi, wr_lo, br)


def _dispatch_kernel(dest_ref, pstart_ref, pend_ref, h_hbm, xs_hbm, zbuf, zsem, sem, *, tm, rows):
    i = pl.program_id(0)
    n = pl.num_programs(0)
    slot = i % 2

    @pl.when(i == 0)
    def _():
        zbuf[...] = jnp.zeros_like(zbuf)

        def zero_copy(e):
            off = pl.multiple_of((pend_ref[e] - rows) * ROW_SUB, rows * ROW_SUB)
            return pltpu.make_async_copy(zbuf, xs_hbm.at[pl.ds(off, rows * ROW_SUB), :], zsem)

        def start(e, carry):
            @pl.when(pend_ref[e] > pstart_ref[e])
            def _():
                zero_copy(e).start()
            return carry

        def wait(e, carry):
            @pl.when(pend_ref[e] > pstart_ref[e])
            def _():
                zero_copy(e).wait()
            return carry

        lax.fori_loop(0, N_EXPERTS, start, 0)
        lax.fori_loop(0, N_EXPERTS, wait, 0)

        def tail_copy(j):
            off = pl.multiple_of(j * (rows * ROW_SUB), rows * ROW_SUB)
            return pltpu.make_async_copy(zbuf, xs_hbm.at[pl.ds(off, rows * ROW_SUB), :], zsem)

        n_used = pend_ref[N_EXPERTS - 1] // rows
        n_blocks = xs_hbm.shape[0] // (rows * ROW_SUB)
        lax.fori_loop(n_used, n_blocks, lambda j, c: (tail_copy(j).start(), c)[1], 0)
        lax.fori_loop(n_used, n_blocks, lambda j, c: (tail_copy(j).wait(), c)[1], 0)

    def wait_slot(sl):
        nsub = TOP_K * tm * ROW_SUB
        pltpu.make_async_copy(h_hbm.at[pl.ds(0, nsub), :], xs_hbm.at[pl.ds(0, nsub), :], sem.at[sl]).wait()

    for r0 in range(0, tm, DMA_GROUP):
        base = (i * tm + r0) * TOP_K
        dests = [dest_ref[base + q] for q in range(DMA_GROUP * TOP_K)]
        for q, dest in enumerate(dests):
            pltpu.make_async_copy(_tile_row(h_hbm, i * tm + r0 + q // TOP_K), _tile_row(xs_hbm, dest),
                                  sem.at[slot]).start()

    @pl.when(i > 0)
    def _():
        wait_slot(1 - slot)

    @pl.when(i == n - 1)
    def _():
        wait_slot(slot)


def _dispatch(h2p, dest, pad_start, pad_end, n_rows):
    t = h2p.shape[0] // ROW_SUB
    tm = ROW_TILE
    return pl.pallas_call(
        functools.partial(_dispatch_kernel, tm=tm, rows=MOE_ROWS),
        grid_spec=pltpu.PrefetchScalarGridSpec(
            num_scalar_prefetch=3,
            grid=(t // tm,),
            in_specs=[pl.BlockSpec(memory_space=pl.ANY)],
            out_specs=pl.BlockSpec(memory_space=pl.ANY),
            scratch_shapes=[pltpu.VMEM((MOE_ROWS * ROW_SUB, LANES), I32), pltpu.SemaphoreType.DMA(()),
                            pltpu.SemaphoreType.DMA((2,))]),
        out_shape=jax.ShapeDtypeStruct((n_rows * ROW_SUB, LANES), I32),
        compiler_params=_cparams(("arbitrary",), disable_bounds_checks=True, has_side_effects=True),
        name="dispatch",
    )(dest, pad_start, pad_end, h2p)


def _expert_kernel(blke_ref, nused_ref, xs_ref, w1_ref, w3_ref, w2_ref, y_ref, w1b, w3b, w2b):
    j = pl.program_id(0)
    e = blke_ref[j]
    e_prev = blke_ref[jnp.maximum(j - 1, 0)]

    @pl.when((j == 0) | (e != e_prev))
    def _():
        w1b[...] = w1_ref[...].astype(BF16)
        w3b[...] = w3_ref[...].astype(BF16)
        w2b[...] = w2_ref[...].astype(BF16)

    @pl.when(j < nused_ref[0])
    def _():
        hi, lo = _unpack_pair(_load_tile_rows(xs_ref, 0, MOE_ROWS))
        xb = jnp.concatenate([hi.astype(BF16), lo.astype(BF16)], axis=1)
        a = jnp.dot(xb, w1b[...], preferred_element_type=F32)
        g = jnp.dot(xb, w3b[...], preferred_element_type=F32)
        hid = (a * _sigmoid(a) * g).astype(BF16)
        y = jnp.dot(hid, w2b[...], preferred_element_type=F32)
        half = y.shape[1] // 2
        _store_tile_rows(y_ref, _pack_pair(y[:, :half], y[:, half:]))

    @pl.when(j >= nused_ref[0])
    def _():
        y_ref[...] = jnp.zeros_like(y_ref)


def _experts(xs, block_e, n_used, w1, w3, w2):
    rows = MOE_ROWS
    blk = rows * ROW_SUB
    d, f = w1.shape[1], w1.shape[2]
    used = lambda j, nu: jnp.minimum(j, nu[0] - 1)
    return pl.pallas_call(
        _expert_kernel,
        grid_spec=pltpu.PrefetchScalarGridSpec(
            num_scalar_prefetch=2,
            grid=(xs.shape[0] // blk,),
            in_specs=[pl.BlockSpec((blk, LANES), lambda j, be, nu: (used(j, nu), 0)),
                      pl.BlockSpec((None, d, f), lambda j, be, nu: (be[j], 0, 0)),
                      pl.BlockSpec((None, d, f), lambda j, be, nu: (be[j], 0, 0)),
                      pl.BlockSpec((None, f, d), lambda j, be, nu: (be[j], 0, 0))],
            out_specs=pl.BlockSpec((blk, LANES), lambda j, be, nu: (j, 0)),
            scratch_shapes=[pltpu.VMEM((d, f), BF16), pltpu.VMEM((d, f), BF16), pltpu.VMEM((f, d), BF16)]),
        out_shape=jax.ShapeDtypeStruct(xs.shape, I32),
        compiler_params=_cparams(("arbitrary",)),
        name="experts",
    )(block_e, n_used, xs, w1, w3, w2)


def _combine_kernel(dest_ref, y_hbm, x1_ref, route_ref, g2_ref, o_ref, ybuf, sem, *, tm):
    i = pl.program_id(0)
    n = pl.num_programs(0)
    slot = i % 2

    def start_gather(tile, sl):
        for r0 in range(0, tm, DMA_GROUP):
            base = (tile * tm + r0) * TOP_K
            dests = [dest_ref[base + q] for q in range(DMA_GROUP * TOP_K)]
            for q, dest in enumerate(dests):
                pltpu.make_async_copy(_tile_row(y_hbm, dest),
                                      _tile_row(ybuf.at[sl], (q % TOP_K) * tm + r0 + q // TOP_K),
                                      sem.at[sl]).start()

    @pl.when(i == 0)
    def _():
        start_gather(0, 0)

    @pl.when(i + 1 < n)
    def _():
        start_gather(i + 1, 1 - slot)

    pltpu.make_async_copy(y_hbm.at[pl.ds(0, TOP_K * tm * ROW_SUB), :], ybuf.at[slot], sem.at[slot]).wait()
    route = route_ref[...]
    w1 = route[:, 2:3]
    w2 = route[:, 3:4]
    hi1, lo1 = _unpack_pair(_load_tile_rows(ybuf, 0, tm, lead=(slot,)))
    hi2, lo2 = _unpack_pair(_load_tile_rows(ybuf, tm, tm, lead=(slot,)))
    half = o_ref.shape[1] // 2
    o_ref[:, :half] = x1_ref[:, :half] + g2_ref[:, :half] * (w1 * hi1 + w2 * hi2)
    o_ref[:, half:] = x1_ref[:, half:] + g2_ref[:, half:] * (w1 * lo1 + w2 * lo2)


def _combine(y, dest, x1, route, g2, s):
    t, d = x1.shape
    tm = ROW_TILE
    per_b_tiles = s // tm
    return pl.pallas_call(
        functools.partial(_combine_kernel, tm=tm),
        grid_spec=pltpu.PrefetchScalarGridSpec(
            num_scalar_prefetch=1,
            grid=(t // tm,),
            in_specs=[pl.BlockSpec(memory_space=pl.ANY),
                      pl.BlockSpec((tm, d), lambda i, ds: (i, 0)),
                      pl.BlockSpec((tm, ROUTE_LANES), lambda i, ds: (i, 0)),
                      pl.BlockSpec((None, 1, d), lambda i, ds: (i // per_b_tiles, 0, 0))],
            out_specs=pl.BlockSpec((tm, d), lambda i, ds: (i, 0)),
            scratch_shapes=[pltpu.VMEM((2, TOP_K * tm * ROW_SUB, LANES), I32), pltpu.SemaphoreType.DMA((2,))]),
        out_shape=jax.ShapeDtypeStruct((t, d), F32),
        compiler_params=_cparams(("arbitrary",), disable_bounds_checks=True),
        name="combine",
    )(dest, y, x1, route, g2)


def _dispatch_plan(route, counts, t):
    rows = MOE_ROWS
    keys = route[:, 0:TOP_K].astype(I32)
    expert = lax.shift_right_logical(keys, jnp.full(keys.shape, RANK_BITS, I32))
    rank = keys & ((1 << RANK_BITS) - 1)
    cnt = counts[0, N_GROUPS:N_GROUPS + N_EXPERTS].astype(I32)
    padded = (cnt + rows - 1) // rows * rows
    pad_end = jnp.cumsum(padded)
    pad_start = pad_end - padded
    onehot = expert[..., None] == jnp.arange(N_EXPERTS, dtype=I32)
    dest = (jnp.sum(jnp.where(onehot, pad_start, 0), axis=-1) + rank).reshape(t * TOP_K)
    n_blocks = (t * TOP_K + N_EXPERTS * (rows - 1) + rows - 1) // rows
    n_used = pad_end[-1:] // rows
    blk = jnp.minimum(jnp.arange(n_blocks, dtype=I32), n_used - 1) * rows
    block_e = jnp.sum((pad_end[None, :] <= blk[:, None]).astype(I32), axis=1)
    return dest, pad_start, pad_end, block_e, n_used.astype(I32), n_blocks * rows


def _trunk(x, mod, p):
    b, s, d = x.shape
    sh1, sc1, g1, sh2, sc2, g2 = [m.reshape(b, 1, d) for m in jnp.split(mod, 6, axis=-1)]
    u, qkv, gates = _inproj(x, sh1, sc1, p["norm1_g"], p["q_norm_g"], p["k_norm_g"], p["w_in"], _rope_tables(s))
    wp = _seq_fft(u)
    attn = _attention(qkv, p["sink"])
    x1, h2p, route, counts = _post(wp, attn, gates, x, g1, sh2, sc2, p["norm2_g"], p["cc"], p["cs"],
                                   p["w_fourier_out"], p["w_attn_out"], p["w_out"],
                                   p["w_router_hi"], p["w_router_lo"], p["b_router"])
    t = b * s
    dest, pad_start, pad_end, block_e, n_used, n_rows = _dispatch_plan(route, counts, t)
    xs = _dispatch(h2p, dest, pad_start, pad_end, n_rows)
    y = _experts(xs, block_e, n_used, p["w1"], p["w3"], p["w2"])
    out = _combine(y, dest, x1, route, g2, s)
    return out.reshape(b, s, d)


def _channel_dft():
    n = FOURIER_GROUP_DIM
    k = jnp.arange(n, dtype=I32)
    ang = ((k[:, None] * k[None, :]) % n).astype(F32) * (2.0 * math.pi / n)
    return (jnp.cos(ang) * n ** -0.5).astype(BF16), (jnp.sin(ang) * n ** -0.5).astype(BF16)


def kernel(x_prompt, x_sample, c_prompt, c_sample, w_ada, b_ada, norm1_g, w_in, q_norm_g, k_norm_g, sink,
           w_fourier_out, w_attn_out, w_out, norm2_g, w_group, b_group, w_expert, b_expert, w1, w3, w2):
    depth = w_ada.shape[0]
    d = x_prompt.shape[-1]
    bp = c_prompt.shape[0]
    bs = c_sample.shape[0]
    cc, cs = _channel_dft()
    xp, xs = x_prompt, x_sample
    for l in range(depth):
        c_all = jnp.concatenate([c_prompt, c_sample, jnp.zeros((8 - (bp + bs) % 8, d), F32)], axis=0)
        mod = _adaln(c_all, w_ada[l], b_ada[l])
        pad = ROUTE_LANES - N_GROUPS - N_EXPERTS
        w_router = jnp.concatenate([w_group[l], w_expert[l], jnp.zeros((d, pad), F32)], axis=1)
        w_router_hi = w_router.astype(BF16)
        p = {
            "norm1_g": norm1_g[l].reshape(1, d), "norm2_g": norm2_g[l].reshape(1, d),
            "q_norm_g": q_norm_g[l].reshape(1, HEAD_DIM), "k_norm_g": k_norm_g[l].reshape(1, HEAD_DIM),
            "sink": sink[l], "w_in": w_in[l].astype(BF16),
            "w_fourier_out": w_fourier_out[l].astype(BF16), "w_attn_out": w_attn_out[l].astype(BF16),
            "w_out": w_out[l].astype(BF16), "cc": cc, "cs": cs,
            "w_router_hi": w_router_hi,
            "w_router_lo": (w_router - w_router_hi.astype(F32)).astype(BF16),
            "b_router": jnp.concatenate([b_group[l], b_expert[l], jnp.zeros((pad,), F32)]).reshape(1, ROUTE_LANES),
            "w1": w1[l], "w3": w3[l], "w2": w2[l],
        }
        xp = _trunk(xp, mod[:bp], p)
        xs = _trunk(xs, mod[bp:bp + bs], p)
    return xp, xs
```

```python
import functools
import math

import jax
import jax.numpy as jnp
from jax import lax
from jax.experimental import pallas as pl
from jax.experimental.pallas import tpu as pltpu

F32 = jnp.float32
BF16 = jnp.bfloat16
I32 = jnp.int32

HEAD_DIM = 128
N_KV_HEADS = 2
Q_PER_KV = 4
N_HEADS = N_KV_HEADS * Q_PER_KV
ROT_DIM = 32
ROPE_THETA = 500000.0
WINDOW = 128
FOURIER_GROUPS = 4
FOURIER_GROUP_DIM = 256
N_GROUPS = 8
EXPERTS_PER_GROUP = 8
N_EXPERTS = N_GROUPS * EXPERTS_PER_GROUP
TOP_K = 2
EPS = 1e-6

LANES = 128
V7X_VMEM_LIMIT = 56 * 1024 * 1024

ROW_TILE = 256
ATTN_TILE = 512
FFT_N1 = 128
FFT_UNROLL = 16
MOE_ROWS = 256
ROUTE_LANES = 128
RANK_BITS = 17
DMA_GROUP = 8
NEG = -1e30


def _cparams(sem, **kw):
    return pltpu.CompilerParams(dimension_semantics=sem, vmem_limit_bytes=V7X_VMEM_LIMIT, **kw)


def _resident(shape, index_map):
    return pl.BlockSpec(shape, index_map, pipeline_mode=pl.Buffered(1))


def _sigmoid(x):
    return 0.5 * jnp.tanh(0.5 * x) + 0.5


def _adaln_kernel(c_ref, w_ref, b_ref, o_ref):
    c = c_ref[...]
    s = c * _sigmoid(c)
    o_ref[...] = jnp.dot(s, w_ref[...], precision=lax.Precision.HIGHEST,
                         preferred_element_type=F32) + b_ref[...]


def _adaln(c, w_ada, b_ada):
    r, d = c.shape
    n = w_ada.shape[1]
    tn = 1024
    return pl.pallas_call(
        _adaln_kernel,
        grid=(n // tn,),
        in_specs=[pl.BlockSpec((r, d), lambda j: (0, 0)),
                  pl.BlockSpec((d, tn), lambda j: (0, j)),
                  pl.BlockSpec((1, tn), lambda j: (0, j))],
        out_specs=pl.BlockSpec((r, tn), lambda j: (0, j)),
        out_shape=jax.ShapeDtypeStruct((r, n), F32),
        compiler_params=_cparams(("parallel",)),
        name="adaln",
    )(c, w_ada, b_ada.reshape(1, n))


def _rope_tables(s):
    half = ROT_DIM // 2
    inv_freq = ROPE_THETA ** (-jnp.arange(0, ROT_DIM, 2, dtype=F32) / ROT_DIM)
    ang = jnp.arange(s, dtype=F32)[:, None] * inv_freq[None, :]
    cos, sin = jnp.cos(ang), jnp.sin(ang)
    pad = jnp.zeros((s, LANES - ROT_DIM), F32)
    cos_f = jnp.concatenate([cos, cos, jnp.ones((s, LANES - ROT_DIM), F32)], axis=1)
    sin_up = jnp.concatenate([-sin, jnp.zeros((s, half), F32), pad], axis=1)
    sin_dn = jnp.concatenate([jnp.zeros((s, half), F32), sin, pad], axis=1)
    return cos_f, sin_up, sin_dn


def _inproj_kernel(x_ref, sh_ref, sc_ref, g_ref, qg_ref, kg_ref, cos_ref, sup_ref, sdn_ref, w_ref,
                   u_ref, qkv_ref, gate_ref, *, fw, aw, kvw):
    x = x_ref[...]
    ms = jnp.mean(x * x, axis=-1, keepdims=True)
    h = (x * lax.rsqrt(ms + EPS) * g_ref[...]) * (1.0 + sc_ref[...]) + sh_ref[...]
    hb = h.astype(BF16)
    chunk = 512
    for c0 in range(0, fw, chunk):
        u_ref[:, c0:c0 + chunk] = jnp.dot(hb, w_ref[:, c0:c0 + chunk], preferred_element_type=F32)
    cos_f, sin_up, sin_dn = cos_ref[...], sup_ref[...], sdn_ref[...]
    half = ROT_DIM // 2
    scale = HEAD_DIM ** -0.5
    for c0 in list(range(0, aw, chunk)) + [aw]:
        width = chunk if c0 < aw else kvw
        acc = jnp.dot(hb, w_ref[:, fw + c0:fw + c0 + width], preferred_element_type=F32)
        for hh in range(width // HEAD_DIM):
            col = c0 + hh * HEAD_DIM
            t = acc[:, hh * HEAD_DIM:(hh + 1) * HEAD_DIM]
            is_q = col < aw
            gain = qg_ref[...] if is_q else kg_ref[...]
            t = t * lax.rsqrt(jnp.mean(t * t, axis=-1, keepdims=True) + EPS) * gain
            t = (t * cos_f + pltpu.roll(t, LANES - half, axis=1) * sin_up
                 + pltpu.roll(t, half, axis=1) * sin_dn)
            if is_q:
                t = t * scale
            qkv_ref[:, col:col + HEAD_DIM] = t.astype(BF16)
    v0 = fw + aw + kvw
    qkv_ref[:, aw + kvw:aw + 2 * kvw] = jnp.dot(
        hb, w_ref[:, v0:v0 + kvw], preferred_element_type=F32).astype(BF16)
    g0 = v0 + kvw
    gw = gate_ref.shape[-1]
    for c0 in range(0, gw, chunk):
        gate_ref[:, c0:c0 + chunk] = jnp.dot(
            hb, w_ref[:, g0 + c0:g0 + c0 + chunk], preferred_element_type=F32).astype(BF16)


def _inproj(x, shift, scale, norm_g, q_g, k_g, w_in_b, tables):
    b, s, d = x.shape
    fw = FOURIER_GROUPS * FOURIER_GROUP_DIM
    aw = N_HEADS * HEAD_DIM
    kvw = N_KV_HEADS * HEAD_DIM
    gw = 2 * d
    tm = ROW_TILE
    row = lambda w: pl.BlockSpec((None, tm, w), lambda bi, i: (bi, i, 0))
    per_b = pl.BlockSpec((None, 1, d), lambda bi, i: (bi, 0, 0))
    const = lambda w: pl.BlockSpec((1, w), lambda bi, i: (0, 0))
    tab = pl.BlockSpec((tm, LANES), lambda bi, i: (i, 0))
    return pl.pallas_call(
        functools.partial(_inproj_kernel, fw=fw, aw=aw, kvw=kvw),
        grid=(b, s // tm),
        in_specs=[row(d), per_b, per_b, const(d), const(HEAD_DIM), const(HEAD_DIM), tab, tab, tab,
                  _resident(w_in_b.shape, lambda bi, i: (0, 0))],
        out_specs=[row(fw), row(aw + 2 * kvw), row(gw)],
        out_shape=[jax.ShapeDtypeStruct((b, s, fw), F32),
                   jax.ShapeDtypeStruct((b, s, aw + 2 * kvw), BF16),
                   jax.ShapeDtypeStruct((b, s, gw), BF16)],
        compiler_params=_cparams(("parallel", "parallel")),
        name="inproj",
    )(x, shift, scale, norm_g, q_g, k_g, *tables, w_in_b)


def _pack_pair(a, b):
    ab = lax.bitcast_convert_type(a.astype(BF16).astype(F32), I32)
    bb = lax.bitcast_convert_type(b.astype(BF16).astype(F32), I32)
    return ab | lax.shift_right_logical(bb, jnp.full(bb.shape, 16, I32))


def _unpack_pair(p):
    hi = lax.bitcast_convert_type(p & jnp.int32(-65536), F32)
    lo = lax.bitcast_convert_type(lax.shift_left(p, jnp.full(p.shape, 16, I32)), F32)
    return hi, lo


def _fft_tables(s):
    n1 = FFT_N1
    n2 = s // n1
    k1 = jnp.arange(n1, dtype=I32)
    nn = (n2 * jnp.arange(n1, dtype=I32))[None, None, :] + jnp.arange(n2, dtype=I32)[:, None, None]
    ph = (k1[None, :, None] * nn) % s
    ang = ph.astype(F32) * (2.0 * math.pi / s)
    sc1 = n1 ** -0.5
    g = jnp.concatenate([jnp.cos(ang) * sc1, -jnp.sin(ang) * sc1], axis=1).astype(BF16)
    k2 = jnp.arange(n2, dtype=I32)
    ang2 = ((k2[:, None] * k2[None, :]) % n2).astype(F32) * (2.0 * math.pi / n2)
    c2, s2 = jnp.cos(ang2) * n2 ** -0.5, jnp.sin(ang2) * n2 ** -0.5
    f2 = jnp.concatenate([jnp.concatenate([c2, s2], axis=1),
                          jnp.concatenate([-s2, c2], axis=1)], axis=0).astype(BF16)
    return g, f2


def _fft_kernel(u_ref, g_ref, f2_ref, o_ref, y_scr, *, n1, n2, unroll):
    def stage1(i, carry):
        for uu in range(unroll):
            m = i * unroll + uu
            xm = u_ref[pl.ds(m, n1, stride=n2), :].astype(BF16)
            y = jnp.dot(g_ref[m], xm, preferred_element_type=F32)
            y_scr[pl.ds(pl.multiple_of(m * n1, n1), n1), :] = _pack_pair(y[:n1], y[n1:])
        return carry

    lax.fori_loop(0, n2 // unroll, stage1, 0)
    f2 = f2_ref[...]

    def stage2(i, carry):
        for uu in range(unroll):
            k1 = i * unroll + uu
            yr, yi = _unpack_pair(y_scr[pl.ds(k1, n2, stride=n1), :])
            rhs = jnp.concatenate([yr.astype(BF16), yi.astype(BF16)], axis=0)
            z = jnp.dot(f2, rhs, preferred_element_type=F32)
            o_ref[pl.ds(k1, n2, stride=n1), :] = _pack_pair(z[:n2], z[n2:])
        return carry

    lax.fori_loop(0, n1 // unroll, stage2, 0)


def _seq_fft(u):
    b, s, c = u.shape
    n1 = FFT_N1
    n2 = s // n1
    g, f2 = _fft_tables(s)
    unroll = min(FFT_UNROLL, n2)
    blk = pl.BlockSpec((None, s, LANES), lambda bi, j: (bi, 0, j))
    return pl.pallas_call(
        functools.partial(_fft_kernel, n1=n1, n2=n2, unroll=unroll),
        grid=(b, c // LANES),
        in_specs=[blk, _resident(g.shape, lambda bi, j: (0, 0, 0)), _resident(f2.shape, lambda bi, j: (0, 0))],
        out_specs=blk,
        out_shape=jax.ShapeDtypeStruct((b, s, c), I32),
        scratch_shapes=[pltpu.VMEM((s, LANES), I32)],
        compiler_params=_cparams(("parallel", "parallel")),
        name="seq_fft",
    )(u, g, f2)


def _attn_kernel(sink_ref, q_ref, kp_ref, kc_ref, kn_ref, vp_ref, vc_ref, vn_ref, o_ref,
                 s_scr, p_scr, inv_scr, *, tq):
    i = pl.program_id(1)
    first = i == 0
    last = i == pl.num_programs(1) - 1
    nsub = tq // WINDOW
    nq = Q_PER_KV * WINDOW
    nk = 3 * WINDOW
    c = lax.broadcasted_iota(I32, (nk, nq), 0)
    r = lax.broadcasted_iota(I32, (nk, nq), 1) & (WINDOW - 1)
    band = ((c >= r) & (c < WINDOW)) | ((c >= WINDOW) & (c < 2 * WINDOW)) | ((c >= 2 * WINDOW) & (c - 2 * WINDOW <= r))
    pairs = [(h, sb) for h in range(N_KV_HEADS) for sb in range(nsub)]
    kcat, vt = [], []
    for h in range(N_KV_HEADS):
        hs = slice(h * HEAD_DIM, (h + 1) * HEAD_DIM)
        kcat.append(jnp.concatenate([kp_ref[:, hs], kc_ref[:, hs], kn_ref[:, hs]], axis=0))
        vcat = jnp.concatenate([vp_ref[:, hs], vc_ref[:, hs], vn_ref[:, hs]], axis=0)
        vt.append(vcat.astype(F32).T.astype(BF16))

    for n, (h, sb) in enumerate(pairs):
        q4 = jnp.concatenate(
            [q_ref[sb * WINDOW:(sb + 1) * WINDOW,
                   (h * Q_PER_KV + g) * HEAD_DIM:(h * Q_PER_KV + g + 1) * HEAD_DIM]
             for g in range(Q_PER_KV)], axis=0)
        kw = kcat[h][sb * WINDOW:(sb + 3) * WINDOW]
        s_scr[n] = lax.dot_general(kw, q4, (((1,), (1,)), ((), ())), preferred_element_type=F32)

    for n, (h, sb) in enumerate(pairs):
        sc = jnp.where(band, s_scr[n], NEG)
        if sb == 0:
            sc = jnp.where((c < WINDOW) & first, NEG, sc)
        if sb == nsub - 1:
            sc = jnp.where((c >= 2 * WINDOW) & last, NEG, sc)
        sink = jnp.concatenate(
            [jnp.full((1, WINDOW), sink_ref[h * Q_PER_KV + g], F32) for g in range(Q_PER_KV)], axis=1)
        m = jnp.maximum(jnp.max(sc, axis=0, keepdims=True), sink)
        p = jnp.exp(sc - m)
        denom = jnp.sum(p, axis=0, keepdims=True) + jnp.exp(sink - m)
        p_scr[n] = p.astype(BF16)
        inv_scr[pl.ds(n, 1), :] = 1.0 / denom

    for n, (h, sb) in enumerate(pairs):
        ot = jnp.dot(vt[h][:, sb * WINDOW:(sb + 3) * WINDOW], p_scr[n], preferred_element_type=F32)
        o = (ot * inv_scr[pl.ds(n, 1), :]).T
        for g in range(Q_PER_KV):
            col = (h * Q_PER_KV + g) * HEAD_DIM
            o_ref[sb * WINDOW:(sb + 1) * WINDOW, col:col + HEAD_DIM] = (
                o[g * WINDOW:(g + 1) * WINDOW].astype(BF16))


def _attention(qkv, sink):
    b, s, _ = qkv.shape
    aw = N_HEADS * HEAD_DIM
    kvw = N_KV_HEADS * HEAD_DIM
    tq = ATTN_TILE
    per = tq // WINDOW
    nblk = s // WINDOW
    kcol = aw // kvw
    vcol = kcol + 1
    halo = lambda col, off: pl.BlockSpec(
        (None, WINDOW, kvw),
        lambda bi, i, sk: (bi, jnp.clip(i * per + off, 0, nblk - 1), col))
    main = lambda col: pl.BlockSpec((None, tq, kvw), lambda bi, i, sk: (bi, i, col))
    return pl.pallas_call(
        functools.partial(_attn_kernel, tq=tq),
        grid_spec=pltpu.PrefetchScalarGridSpec(
            num_scalar_prefetch=1,
            grid=(b, s // tq),
            in_specs=[pl.BlockSpec((None, tq, aw), lambda bi, i, sk: (bi, i, 0)),
                      halo(kcol, -1), main(kcol), halo(kcol, per),
                      halo(vcol, -1), main(vcol), halo(vcol, per)],
            out_specs=pl.BlockSpec((None, tq, aw), lambda bi, i, sk: (bi, i, 0)),
            scratch_shapes=[pltpu.VMEM((N_KV_HEADS * per, 3 * WINDOW, Q_PER_KV * WINDOW), F32),
                            pltpu.VMEM((N_KV_HEADS * per, 3 * WINDOW, Q_PER_KV * WINDOW), BF16),
                            pltpu.VMEM((N_KV_HEADS * per, Q_PER_KV * WINDOW), F32)]),
        out_shape=jax.ShapeDtypeStruct((b, s, aw), BF16),
        compiler_params=_cparams(("parallel", "parallel")),
        name="attention",
    )(sink, qkv, qkv, qkv, qkv, qkv, qkv, qkv)


ROW_WORDS = 1024
ROW_SUB = ROW_WORDS // LANES


def _store_tile_rows(ref, val, base=0, lead=()):
    r = val.shape[0]
    for j in range(ROW_SUB):
        ref[lead + (pl.ds(base * ROW_SUB + j, r, stride=ROW_SUB), slice(None))] = val[:, j * LANES:(j + 1) * LANES]


def _load_tile_rows(ref, base, r, lead=()):
    return jnp.concatenate(
        [ref[lead + (pl.ds(base * ROW_SUB + j, r, stride=ROW_SUB), slice(None))] for j in range(ROW_SUB)], axis=1)


def _tile_row(ref, row):
    return ref.at[pl.ds(pl.multiple_of(row * ROW_SUB, ROW_SUB), ROW_SUB), :]


def _post_kernel(wp_ref, o_ref, gate_ref, x_ref, g1_ref, sh2_ref, sc2_ref, n2g_ref, cc_ref, cs_ref,
                 wfo_ref, wao_ref, wout_ref, wr_ref, br_ref, x1_ref, h2_ref, route_ref, cnt_ref):
    tm = x_ref.shape[0]
    d = x_ref.shape[1]

    @pl.when(pl.program_id(0) == 0)
    def _():
        cnt_ref[...] = jnp.zeros_like(cnt_ref)

    re, im = _unpack_pair(wp_ref[...])
    re, im = re.astype(BF16), im.astype(BF16)
    cc, cs = cc_ref[...], cs_ref[...]
    gd = FOURIER_GROUP_DIM
    fm = jnp.concatenate(
        [(jnp.dot(re[:, g * gd:(g + 1) * gd], cc, preferred_element_type=F32)
          + jnp.dot(im[:, g * gd:(g + 1) * gd], cs, preferred_element_type=F32)).astype(BF16)
         for g in range(FOURIER_GROUPS)], axis=1)
    y_f = jnp.dot(fm, wfo_ref[...], preferred_element_type=F32)
    y_a = jnp.dot(o_ref[...], wao_ref[...], preferred_element_type=F32)
    merged = (_sigmoid(gate_ref[:, :d].astype(F32)) * y_f
              + _sigmoid(gate_ref[:, d:].astype(F32)) * y_a).astype(BF16)
    x1 = x_ref[...] + g1_ref[...] * jnp.dot(merged, wout_ref[...], preferred_element_type=F32)
    x1_ref[...] = x1
    ms = jnp.mean(x1 * x1, axis=-1, keepdims=True)
    h2 = (x1 * lax.rsqrt(ms + EPS) * n2g_ref[...]) * (1.0 + sc2_ref[...]) + sh2_ref[...]
    half = d // 2
    _store_tile_rows(h2_ref, _pack_pair(h2[:, :half], h2[:, half:]))

    h_hi = h2.astype(BF16)
    h_lo = (h2 - h_hi.astype(F32)).astype(BF16)
    wr = wr_ref[...]
    r_hi = jnp.dot(h_hi, wr, preferred_element_type=F32)
    r_lo = jnp.dot(h_lo, wr, preferred_element_type=F32)
    logits = ((r_hi[:, :ROUTE_LANES] + r_hi[:, ROUTE_LANES:])
              + (r_lo[:, :ROUTE_LANES] + r_lo[:, ROUTE_LANES:])) + br_ref[...]
    lane = lax.broadcasted_iota(I32, (tm, ROUTE_LANES), 1)
    big = jnp.int32(ROUTE_LANES)
    is_g = lane < N_GROUPS
    gl = jnp.where(is_g, logits, NEG)
    gmax = jnp.max(gl, axis=-1, keepdims=True)
    gidx = jnp.min(jnp.where(gl == gmax, lane, big), axis=-1, keepdims=True)
    p_g = 1.0 / jnp.sum(jnp.where(is_g, jnp.exp(gl - gmax), 0.0), axis=-1, keepdims=True)
    in_grp = (lane >= N_GROUPS) & (lane < N_GROUPS + N_EXPERTS) & (
        lax.shift_right_logical(lane - N_GROUPS, jnp.full(lane.shape, 3, I32)) == gidx)
    el = jnp.where(in_grp, logits, NEG)
    e1v = jnp.max(el, axis=-1, keepdims=True)
    e1i = jnp.min(jnp.where(el == e1v, lane, big), axis=-1, keepdims=True)
    el2 = jnp.where(lane == e1i, NEG, el)
    e2v = jnp.max(el2, axis=-1, keepdims=True)
    e2i = jnp.min(jnp.where(el2 == e2v, lane, big), axis=-1, keepdims=True)
    t = jnp.exp(e2v - e1v)
    w1 = p_g / (1.0 + t)
    w2 = w1 * t
    sel1, sel2 = lane == e1i, lane == e2i
    member = jnp.where(sel1 | sel2, 1.0, 0.0)
    rr = lax.broadcasted_iota(I32, (tm, tm), 0)
    cc_i = lax.broadcasted_iota(I32, (tm, tm), 1)
    tri = jnp.where(cc_i < rr, 1.0, 0.0).astype(BF16)
    prefix = jnp.dot(tri, member.astype(BF16), preferred_element_type=F32) + cnt_ref[...]
    rank1 = jnp.sum(jnp.where(sel1, prefix, 0.0), axis=-1, keepdims=True)
    rank2 = jnp.sum(jnp.where(sel2, prefix, 0.0), axis=-1, keepdims=True)
    cnt_ref[...] = cnt_ref[...] + jnp.sum(member, axis=0, keepdims=True)
    key_scale = float(1 << RANK_BITS)
    key1 = (e1i - N_GROUPS).astype(F32) * key_scale + rank1
    key2 = (e2i - N_GROUPS).astype(F32) * key_scale + rank2
    route = jnp.zeros((tm, ROUTE_LANES), F32)
    for k, val in enumerate((key1, key2, w1, w2)):
        route = jnp.where(lane == k, val, route)
    route_ref[...] = route


def _post(wp, attn, gates, x, g1, sh2, sc2, norm2_g, cc, cs, wfo_b, wao_b, wout_b, wr, br):
    b, s, d = x.shape
    t = b * s
    tm = ROW_TILE
    per_b_tiles = s // tm
    flat = lambda a: a.reshape(t, a.shape[-1])
    row = lambda w: pl.BlockSpec((tm, w), lambda i: (i, 0))
    per_b = pl.BlockSpec((None, 1, d), lambda i: (i // per_b_tiles, 0, 0))
    const = lambda a: _resident(a.shape, lambda i: (0,) * a.ndim)
    fw = wp.shape[-1]
    aw = attn.shape[-1]
    return pl.pallas_call(
        _post_kernel,
        grid=(t // tm,),
        in_specs=[row(fw), row(aw), row(2 * d), row(d), per_b, per_b, per_b,
                  const(norm2_g), const(cc), const(cs), const(wfo_b), const(wao_b), const(wout_b),
                  const(wr), const(br)],
        out_specs=[row(d), pl.BlockSpec((tm * ROW_SUB, LANES), lambda i: (i, 0)), row(ROUTE_LANES),
                   pl.BlockSpec((1, ROUTE_LANES), lambda i: (0, 0))],
        out_shape=[jax.ShapeDtypeStruct((t, d), F32), jax.ShapeDtypeStruct((t * ROW_SUB, LANES), I32),
                   jax.ShapeDtypeStruct((t, ROUTE_LANES), F32), jax.ShapeDtypeStruct((1, ROUTE_LANES), F32)],
        compiler_params=_cparams(("arbitrary",)),
        name="post_router",
    )(flat(wp), flat(attn), flat(gates), flat(x), g1, sh2, sc2, norm2_g, cc, cs, wfo_b, wao_b, wout_b, wr, br)


def _expert_kernel(dest_ref, blke_ref, nused_ref, h_hbm, w1_ref, w3_ref, w2_ref, y_ref,
                   rowtok, xbuf, sem, w1b, w3b, w2b, *, rows):
    j = pl.program_id(0)
    nblk = pl.num_programs(0)
    slot = j % 2

    def start_gather(blk, sl):
        group = DMA_GROUP * TOP_K

        def body(gi, carry):
            r0 = gi * group
            toks = [rowtok[blk * rows + r0 + q] for q in range(group)]
            for q, tok in enumerate(toks):
                pltpu.make_async_copy(_tile_row(h_hbm, tok), _tile_row(xbuf.at[sl], r0 + q), sem.at[sl]).start()
            return carry

        lax.fori_loop(0, rows // group, body, 0)

    @pl.when(j == 0)
    def _():
        unroll = 8

        def zero(i, carry):
            for u in range(unroll):
                rowtok[i * unroll + u] = 0
            return carry

        def scatter(i, carry):
            for u in range(unroll):
                a = i * unroll + u
                rowtok[dest_ref[a]] = a // TOP_K
            return carry

        lax.fori_loop(j, j + rowtok.shape[0] // unroll, zero, 0)
        lax.fori_loop(j, j + dest_ref.shape[0] // unroll, scatter, 0)
        start_gather(0, 0)

    @pl.when(j + 1 < nblk)
    def _():
        start_gather(j + 1, 1 - slot)

    e = blke_ref[j]
    e_prev = blke_ref[jnp.maximum(j - 1, 0)]

    @pl.when((j == 0) | (e != e_prev))
    def _():
        w1b[...] = w1_ref[...].astype(BF16)
        w3b[...] = w3_ref[...].astype(BF16)
        w2b[...] = w2_ref[...].astype(BF16)

    pltpu.make_async_copy(h_hbm.at[pl.ds(0, rows * ROW_SUB), :], xbuf.at[slot], sem.at[slot]).wait()

    @pl.when(j < nused_ref[0])
    def _():
        hi, lo = _unpack_pair(_load_tile_rows(xbuf, 0, rows, lead=(slot,)))
        xb = jnp.concatenate([hi.astype(BF16), lo.astype(BF16)], axis=1)
        a = jnp.dot(xb, w1b[...], preferred_element_type=F32)
        g = jnp.dot(xb, w3b[...], preferred_element_type=F32)
        hid = (a * _sigmoid(a) * g).astype(BF16)
        y = jnp.dot(hid, w2b[...], preferred_element_type=F32)
        half = y.shape[1] // 2
        _store_tile_rows(y_ref, _pack_pair(y[:, :half], y[:, half:]))

    @pl.when(j >= nused_ref[0])
    def _():
        y_ref[...] = jnp.zeros_like(y_ref)


def _experts(h2p, dest, block_e, n_used, n_rows, w1, w3, w2):
    rows = MOE_ROWS
    blk = rows * ROW_SUB
    d, f = w1.shape[1], w1.shape[2]
    wspec = lambda shape: pl.BlockSpec(shape, lambda j, ds, be, nu: (be[j], 0, 0))
    return pl.pallas_call(
        functools.partial(_expert_kernel, rows=rows),
        grid_spec=pltpu.PrefetchScalarGridSpec(
            num_scalar_prefetch=3,
            grid=(n_rows // rows,),
            in_specs=[pl.BlockSpec(memory_space=pl.ANY),
                      wspec((None, d, f)), wspec((None, d, f)), wspec((None, f, d))],
            out_specs=pl.BlockSpec((blk, LANES), lambda j, ds, be, nu: (j, 0)),
            scratch_shapes=[pltpu.SMEM((n_rows,), I32), pltpu.VMEM((2, blk, LANES), I32),
                            pltpu.SemaphoreType.DMA((2,)),
                            pltpu.VMEM((d, f), BF16), pltpu.VMEM((d, f), BF16), pltpu.VMEM((f, d), BF16)]),
        out_shape=jax.ShapeDtypeStruct((n_rows * ROW_SUB, LANES), I32),
        compiler_params=_cparams(("arbitrary",), disable_bounds_checks=True),
        name="experts",
    )(dest, block_e, n_used, h2p, w1, w3, w2)


def _combine_kernel(dest_ref, y_hbm, x1_ref, route_ref, g2_ref, o_ref, ybuf, sem, *, tm):
    i = pl.program_id(0)
    n = pl.num_programs(0)
    slot = i % 2

    def start_gather(tile, sl):
        def body(gi, carry):
            r0 = gi * DMA_GROUP
            base = (tile * tm + r0) * TOP_K
            dests = [dest_ref[base + q] for q in range(DMA_GROUP * TOP_K)]
            for q, dest in enumerate(dests):
                pltpu.make_async_copy(_tile_row(y_hbm, dest),
                                      _tile_row(ybuf.at[sl], (q % TOP_K) * tm + r0 + q // TOP_K),
                                      sem.at[sl]).start()
            return carry

        lax.fori_loop(0, tm // DMA_GROUP, body, 0)

    @pl.when(i == 0)
    def _():
        start_gather(0, 0)

    @pl.when(i + 1 < n)
    def _():
        start_gather(i + 1, 1 - slot)

    pltpu.make_async_copy(y_hbm.at[pl.ds(0, TOP_K * tm * ROW_SUB), :], ybuf.at[slot], sem.at[slot]).wait()
    route = route_ref[...]
    w1 = route[:, 2:3]
    w2 = route[:, 3:4]
    hi1, lo1 = _unpack_pair(_load_tile_rows(ybuf, 0, tm, lead=(slot,)))
    hi2, lo2 = _unpack_pair(_load_tile_rows(ybuf, tm, tm, lead=(slot,)))
    half = o_ref.shape[1] // 2
    o_ref[:, :half] = x1_ref[:, :half] + g2_ref[:, :half] * (w1 * hi1 + w2 * hi2)
    o_ref[:, half:] = x1_ref[:, half:] + g2_ref[:, half:] * (w1 * lo1 + w2 * lo2)


def _combine(y, dest, x1, route, g2, s):
    t, d = x1.shape
    tm = ROW_TILE
    per_b_tiles = s // tm
    return pl.pallas_call(
        functools.partial(_combine_kernel, tm=tm),
        grid_spec=pltpu.PrefetchScalarGridSpec(
            num_scalar_prefetch=1,
            grid=(t // tm,),
            in_specs=[pl.BlockSpec(memory_space=pl.ANY),
                      pl.BlockSpec((tm, d), lambda i, ds: (i, 0)),
                      pl.BlockSpec((tm, ROUTE_LANES), lambda i, ds: (i, 0)),
                      pl.BlockSpec((None, 1, d), lambda i, ds: (i // per_b_tiles, 0, 0))],
            out_specs=pl.BlockSpec((tm, d), lambda i, ds: (i, 0)),
            scratch_shapes=[pltpu.VMEM((2, TOP_K * tm * ROW_SUB, LANES), I32), pltpu.SemaphoreType.DMA((2,))]),
        out_shape=jax.ShapeDtypeStruct((t, d), F32),
        compiler_params=_cparams(("arbitrary",), disable_bounds_checks=True),
        name="combine",
    )(dest, y, x1, route, g2)


def _dispatch_plan(route, counts, t):
    rows = MOE_ROWS
    keys = route[:, 0:TOP_K].astype(I32)
    expert = lax.shift_right_logical(keys, jnp.full(keys.shape, RANK_BITS, I32))
    rank = keys & ((1 << RANK_BITS) - 1)
    cnt = counts[0, N_GROUPS:N_GROUPS + N_EXPERTS].astype(I32)
    padded = (cnt + rows - 1) // rows * rows
    pad_end = jnp.cumsum(padded)
    pad_start = pad_end - padded
    onehot = expert[..., None] == jnp.arange(N_EXPERTS, dtype=I32)
    dest = (jnp.sum(jnp.where(onehot, pad_start, 0), axis=-1) + rank).reshape(t * TOP_K)
    n_blocks = (t * TOP_K + N_EXPERTS * (rows - 1) + rows - 1) // rows
    n_used = pad_end[-1:] // rows
    blk = jnp.minimum(jnp.arange(n_blocks, dtype=I32), n_used - 1) * rows
    block_e = jnp.sum((pad_end[None, :] <= blk[:, None]).astype(I32), axis=1)
    return dest, block_e, n_used.astype(I32), n_blocks * rows


def _trunk(x, mod, p):
    b, s, d = x.shape
    sh1, sc1, g1, sh2, sc2, g2 = [m.reshape(b, 1, d) for m in jnp.split(mod, 6, axis=-1)]
    u, qkv, gates = _inproj(x, sh1, sc1, p["norm1_g"], p["q_norm_g"], p["k_norm_g"], p["w_in"], _rope_tables(s))
    wp = _seq_fft(u)
    attn = _attention(qkv, p["sink"])
    x1, h2p, route, counts = _post(wp, attn, gates, x, g1, sh2, sc2, p["norm2_g"], p["cc"], p["cs"],
                                   p["w_fourier_out"], p["w_attn_out"], p["w_out"],
                                   p["w_router"], p["b_router"])
    t = b * s
    dest, block_e, n_used, n_rows = _dispatch_plan(route, counts, t)
    y = _experts(h2p, dest, block_e, n_used, n_rows, p["w1"], p["w3"], p["w2"])
    out = _combine(y, dest, x1, route, g2, s)
    return out.reshape(b, s, d)


def _channel_dft():
    n = FOURIER_GROUP_DIM
    k = jnp.arange(n, dtype=I32)
    ang = ((k[:, None] * k[None, :]) % n).astype(F32) * (2.0 * math.pi / n)
    return (jnp.cos(ang) * n ** -0.5).astype(BF16), (jnp.sin(ang) * n ** -0.5).astype(BF16)


def kernel(x_prompt, x_sample, c_prompt, c_sample, w_ada, b_ada, norm1_g, w_in, q_norm_g, k_norm_g, sink,
           w_fourier_out, w_attn_out, w_out, norm2_g, w_group, b_group, w_expert, b_expert, w1, w3, w2):
    depth = w_ada.shape[0]
    d = x_prompt.shape[-1]
    bp = c_prompt.shape[0]
    bs = c_sample.shape[0]
    cc, cs = _channel_dft()
    xp, xs = x_prompt, x_sample
    for l in range(depth):
        c_all = jnp.concatenate([c_prompt, c_sample, jnp.zeros((8 - (bp + bs) % 8, d), F32)], axis=0)
        mod = _adaln(c_all, w_ada[l], b_ada[l])
        pad = ROUTE_LANES - N_GROUPS - N_EXPERTS
        w_router = jnp.concatenate([w_group[l], w_expert[l], jnp.zeros((d, pad), F32)], axis=1)
        w_router_hi = w_router.astype(BF16)
        p = {
            "norm1_g": norm1_g[l].reshape(1, d), "norm2_g": norm2_g[l].reshape(1, d),
            "q_norm_g": q_norm_g[l].reshape(1, HEAD_DIM), "k_norm_g": k_norm_g[l].reshape(1, HEAD_DIM),
            "sink": sink[l], "w_in": w_in[l].astype(BF16),
            "w_fourier_out": w_fourier_out[l].astype(BF16), "w_attn_out": w_attn_out[l].astype(BF16),
            "w_out": w_out[l].astype(BF16), "cc": cc, "cs": cs,
            "w_router": jnp.concatenate(
                [w_router_hi, (w_router - w_router_hi.astype(F32)).astype(BF16)], axis=1),
            "b_router": jnp.concatenate([b_group[l], b_expert[l], jnp.zeros((pad,), F32)]).reshape(1, ROUTE_LANES),
            "w1": w1[l], "w3": w3[l], "w2": w2[l],
        }
        xp = _trunk(xp, mod[:bp], p)
        xs = _trunk(xs, mod[bp:bp + bs], p)
    return xp, xs
```

```python
import functools
import math

import jax
import jax.numpy as jnp
from jax import lax
from jax.experimental import pallas as pl
from jax.experimental.pallas import tpu as pltpu

F32 = jnp.float32
BF16 = jnp.bfloat16
I32 = jnp.int32

HEAD_DIM = 128
N_KV_HEADS = 2
Q_PER_KV = 4
N_HEADS = N_KV_HEADS * Q_PER_KV
ROT_DIM = 32
ROPE_THETA = 500000.0
WINDOW = 128
FOURIER_GROUPS = 4
FOURIER_GROUP_DIM = 256
N_GROUPS = 8
EXPERTS_PER_GROUP = 8
N_EXPERTS = N_GROUPS * EXPERTS_PER_GROUP
TOP_K = 2
EPS = 1e-6

LANES = 128
V7X_VMEM_LIMIT = 56 * 1024 * 1024

ROW_TILE = 256
ATTN_TILE = 512
FFT_N1 = 128
FFT_UNROLL = 16
FFT_PITCH_PAD = 8
MOE_ROWS = 256
ROUTE_LANES = 128
RANK_BITS = 17
DMA_GROUP = 8
NEG = -1e30


def _cparams(sem, **kw):
    return pltpu.CompilerParams(dimension_semantics=sem, vmem_limit_bytes=V7X_VMEM_LIMIT, **kw)


def _resident(shape, index_map):
    return pl.BlockSpec(shape, index_map, pipeline_mode=pl.Buffered(1))


def _sigmoid(x):
    return 0.5 * jnp.tanh(0.5 * x) + 0.5


def _adaln_kernel(c_ref, w_ref, b_ref, o_ref):
    c = c_ref[...]
    s = c * _sigmoid(c)
    o_ref[...] = jnp.dot(s, w_ref[...], precision=lax.Precision.HIGHEST,
                         preferred_element_type=F32) + b_ref[...]


def _adaln(c, w_ada, b_ada):
    r, d = c.shape
    n = w_ada.shape[1]
    tn = 1024
    return pl.pallas_call(
        _adaln_kernel,
        grid=(n // tn,),
        in_specs=[pl.BlockSpec((r, d), lambda j: (0, 0)),
                  pl.BlockSpec((d, tn), lambda j: (0, j)),
                  pl.BlockSpec((1, tn), lambda j: (0, j))],
        out_specs=pl.BlockSpec((r, tn), lambda j: (0, j)),
        out_shape=jax.ShapeDtypeStruct((r, n), F32),
        compiler_params=_cparams(("parallel",)),
        name="adaln",
    )(c, w_ada, b_ada.reshape(1, n))


def _rope_tables(s):
    half = ROT_DIM // 2
    inv_freq = ROPE_THETA ** (-jnp.arange(0, ROT_DIM, 2, dtype=F32) / ROT_DIM)
    ang = jnp.arange(s, dtype=F32)[:, None] * inv_freq[None, :]
    cos, sin = jnp.cos(ang), jnp.sin(ang)
    pad = jnp.zeros((s, LANES - ROT_DIM), F32)
    cos_f = jnp.concatenate([cos, cos, jnp.ones((s, LANES - ROT_DIM), F32)], axis=1)
    sin_up = jnp.concatenate([-sin, jnp.zeros((s, half), F32), pad], axis=1)
    sin_dn = jnp.concatenate([jnp.zeros((s, half), F32), sin, pad], axis=1)
    return cos_f, sin_up, sin_dn


def _inproj_kernel(x_ref, sh_ref, sc_ref, g_ref, qg_ref, kg_ref, cos_ref, sup_ref, sdn_ref, w_ref,
                   u_ref, qkv_ref, gate_ref, *, fw, aw, kvw):
    x = x_ref[...]
    ms = jnp.mean(x * x, axis=-1, keepdims=True)
    h = (x * lax.rsqrt(ms + EPS) * g_ref[...]) * (1.0 + sc_ref[...]) + sh_ref[...]
    hb = h.astype(BF16)
    chunk = 512
    for c0 in range(0, fw, chunk):
        u_ref[:, c0:c0 + chunk] = jnp.dot(hb, w_ref[:, c0:c0 + chunk], preferred_element_type=F32)
    cos_f, sin_up, sin_dn = cos_ref[...], sup_ref[...], sdn_ref[...]
    half = ROT_DIM // 2
    scale = HEAD_DIM ** -0.5
    for c0 in list(range(0, aw, chunk)) + [aw]:
        width = chunk if c0 < aw else kvw
        acc = jnp.dot(hb, w_ref[:, fw + c0:fw + c0 + width], preferred_element_type=F32)
        for hh in range(width // HEAD_DIM):
            col = c0 + hh * HEAD_DIM
            t = acc[:, hh * HEAD_DIM:(hh + 1) * HEAD_DIM]
            is_q = col < aw
            gain = qg_ref[...] if is_q else kg_ref[...]
            t = t * lax.rsqrt(jnp.mean(t * t, axis=-1, keepdims=True) + EPS) * gain
            t = (t * cos_f + pltpu.roll(t, LANES - half, axis=1) * sin_up
                 + pltpu.roll(t, half, axis=1) * sin_dn)
            if is_q:
                t = t * scale
            qkv_ref[:, col:col + HEAD_DIM] = t.astype(BF16)
    v0 = fw + aw + kvw
    qkv_ref[:, aw + kvw:aw + 2 * kvw] = jnp.dot(
        hb, w_ref[:, v0:v0 + kvw], preferred_element_type=F32).astype(BF16)
    g0 = v0 + kvw
    gw = gate_ref.shape[-1]
    for c0 in range(0, gw, chunk):
        gate_ref[:, c0:c0 + chunk] = jnp.dot(
            hb, w_ref[:, g0 + c0:g0 + c0 + chunk], preferred_element_type=F32).astype(BF16)


def _inproj(x, shift, scale, norm_g, q_g, k_g, w_in_b, tables):
    b, s, d = x.shape
    fw = FOURIER_GROUPS * FOURIER_GROUP_DIM
    aw = N_HEADS * HEAD_DIM
    kvw = N_KV_HEADS * HEAD_DIM
    gw = 2 * d
    tm = ROW_TILE
    row = lambda w: pl.BlockSpec((None, tm, w), lambda bi, i: (bi, i, 0))
    per_b = pl.BlockSpec((None, 1, d), lambda bi, i: (bi, 0, 0))
    const = lambda w: pl.BlockSpec((1, w), lambda bi, i: (0, 0))
    tab = pl.BlockSpec((tm, LANES), lambda bi, i: (i, 0))
    return pl.pallas_call(
        functools.partial(_inproj_kernel, fw=fw, aw=aw, kvw=kvw),
        grid=(b, s // tm),
        in_specs=[row(d), per_b, per_b, const(d), const(HEAD_DIM), const(HEAD_DIM), tab, tab, tab,
                  _resident(w_in_b.shape, lambda bi, i: (0, 0))],
        out_specs=[row(fw), row(aw + 2 * kvw), row(gw)],
        out_shape=[jax.ShapeDtypeStruct((b, s, fw), F32),
                   jax.ShapeDtypeStruct((b, s, aw + 2 * kvw), BF16),
                   jax.ShapeDtypeStruct((b, s, gw), BF16)],
        compiler_params=_cparams(("parallel", "parallel")),
        name="inproj",
    )(x, shift, scale, norm_g, q_g, k_g, *tables, w_in_b)


def _pack_pair(a, b):
    ab = lax.bitcast_convert_type(a.astype(BF16).astype(F32), I32)
    bb = lax.bitcast_convert_type(b.astype(BF16).astype(F32), I32)
    return ab | lax.shift_right_logical(bb, jnp.full(bb.shape, 16, I32))


def _unpack_pair(p):
    hi = lax.bitcast_convert_type(p & jnp.int32(-65536), F32)
    lo = lax.bitcast_convert_type(lax.shift_left(p, jnp.full(p.shape, 16, I32)), F32)
    return hi, lo


def _fft_tables(s):
    n1 = FFT_N1
    n2 = s // n1
    k1 = jnp.arange(n1, dtype=I32)
    nn = (n2 * jnp.arange(n1, dtype=I32))[None, None, :] + jnp.arange(n2, dtype=I32)[:, None, None]
    ph = (k1[None, :, None] * nn) % s
    ang = ph.astype(F32) * (2.0 * math.pi / s)
    sc1 = n1 ** -0.5
    g = jnp.concatenate([jnp.cos(ang) * sc1, -jnp.sin(ang) * sc1], axis=1).astype(BF16)
    k2 = jnp.arange(n2, dtype=I32)
    ang2 = ((k2[:, None] * k2[None, :]) % n2).astype(F32) * (2.0 * math.pi / n2)
    c2, s2 = jnp.cos(ang2) * n2 ** -0.5, jnp.sin(ang2) * n2 ** -0.5
    f2 = jnp.concatenate([jnp.concatenate([c2, s2], axis=1),
                          jnp.concatenate([-s2, c2], axis=1)], axis=0).astype(BF16)
    return g, f2


def _fft_kernel(u_ref, g_ref, f2_ref, o_ref, y_scr, *, n1, n2, unroll):
    pitch = n1 + FFT_PITCH_PAD

    def stage1(i, carry):
        for uu in range(unroll):
            m = i * unroll + uu
            xm = u_ref[pl.ds(m, n1, stride=n2), :].astype(BF16)
            y = jnp.dot(g_ref[m], xm, preferred_element_type=F32)
            y_scr[pl.ds(pl.multiple_of(m * pitch, 8), n1), :] = _pack_pair(y[:n1], y[n1:])
        return carry

    lax.fori_loop(0, n2 // unroll, stage1, 0)
    f2 = f2_ref[...]

    def stage2(i, carry):
        for uu in range(unroll):
            k1 = i * unroll + uu
            yr, yi = _unpack_pair(y_scr[pl.ds(k1, n2, stride=pitch), :])
            rhs = jnp.concatenate([yr.astype(BF16), yi.astype(BF16)], axis=0)
            z = jnp.dot(f2, rhs, preferred_element_type=F32)
            y_scr[pl.ds(k1, n2, stride=pitch), :] = _pack_pair(z[:n2], z[n2:])
        return carry

    lax.fori_loop(0, n1 // unroll, stage2, 0)

    def compact(i, carry):
        for uu in range(unroll):
            k2 = i * unroll + uu
            o_ref[pl.ds(pl.multiple_of(k2 * n1, n1), n1), :] = y_scr[pl.ds(pl.multiple_of(k2 * pitch, 8), n1), :]
        return carry

    lax.fori_loop(0, n2 // unroll, compact, 0)


def _seq_fft(u):
    b, s, c = u.shape
    n1 = FFT_N1
    n2 = s // n1
    g, f2 = _fft_tables(s)
    unroll = min(FFT_UNROLL, n2)
    blk = pl.BlockSpec((None, s, LANES), lambda bi, j: (bi, 0, j))
    return pl.pallas_call(
        functools.partial(_fft_kernel, n1=n1, n2=n2, unroll=unroll),
        grid=(b, c // LANES),
        in_specs=[blk, _resident(g.shape, lambda bi, j: (0, 0, 0)), _resident(f2.shape, lambda bi, j: (0, 0))],
        out_specs=blk,
        out_shape=jax.ShapeDtypeStruct((b, s, c), I32),
        scratch_shapes=[pltpu.VMEM((n2 * (n1 + FFT_PITCH_PAD), LANES), I32)],
        compiler_params=_cparams(("parallel", "parallel")),
        name="seq_fft",
    )(u, g, f2)


def _attn_kernel(sink_ref, q_ref, kp_ref, kc_ref, kn_ref, vp_ref, vc_ref, vn_ref, o_ref,
                 s_scr, p_scr, inv_scr, *, tq):
    i = pl.program_id(1)
    first = i == 0
    last = i == pl.num_programs(1) - 1
    nsub = tq // WINDOW
    nq = Q_PER_KV * WINDOW
    nk = 3 * WINDOW
    c = lax.broadcasted_iota(I32, (nk, nq), 0)
    r = lax.broadcasted_iota(I32, (nk, nq), 1) & (WINDOW - 1)
    band = ((c >= r) & (c < WINDOW)) | ((c >= WINDOW) & (c < 2 * WINDOW)) | ((c >= 2 * WINDOW) & (c - 2 * WINDOW <= r))
    pairs = [(h, sb) for h in range(N_KV_HEADS) for sb in range(nsub)]
    kcat, vt = [], []
    for h in range(N_KV_HEADS):
        hs = slice(h * HEAD_DIM, (h + 1) * HEAD_DIM)
        kcat.append(jnp.concatenate([kp_ref[:, hs], kc_ref[:, hs], kn_ref[:, hs]], axis=0))
        vcat = jnp.concatenate([vp_ref[:, hs], vc_ref[:, hs], vn_ref[:, hs]], axis=0)
        vt.append(vcat.astype(F32).T.astype(BF16))

    for n, (h, sb) in enumerate(pairs):
        q4 = jnp.concatenate(
            [q_ref[sb * WINDOW:(sb + 1) * WINDOW,
                   (h * Q_PER_KV + g) * HEAD_DIM:(h * Q_PER_KV + g + 1) * HEAD_DIM]
             for g in range(Q_PER_KV)], axis=0)
        kw = kcat[h][sb * WINDOW:(sb + 3) * WINDOW]
        s_scr[n] = lax.dot_general(kw, q4, (((1,), (1,)), ((), ())), preferred_element_type=F32)

    for n, (h, sb) in enumerate(pairs):
        sc = jnp.where(band, s_scr[n], NEG)
        if sb == 0:
            sc = jnp.where((c < WINDOW) & first, NEG, sc)
        if sb == nsub - 1:
            sc = jnp.where((c >= 2 * WINDOW) & last, NEG, sc)
        sink = jnp.concatenate(
            [jnp.full((1, WINDOW), sink_ref[h * Q_PER_KV + g], F32) for g in range(Q_PER_KV)], axis=1)
        m = jnp.maximum(jnp.max(sc, axis=0, keepdims=True), sink)
        p = jnp.exp(sc - m)
        denom = jnp.sum(p, axis=0, keepdims=True) + jnp.exp(sink - m)
        p_scr[n] = p.astype(BF16)
        inv_scr[pl.ds(n, 1), :] = 1.0 / denom

    for n, (h, sb) in enumerate(pairs):
        ot = jnp.dot(vt[h][:, sb * WINDOW:(sb + 3) * WINDOW], p_scr[n], preferred_element_type=F32)
        o = (ot * inv_scr[pl.ds(n, 1), :]).T
        for g in range(Q_PER_KV):
            col = (h * Q_PER_KV + g) * HEAD_DIM
            o_ref[sb * WINDOW:(sb + 1) * WINDOW, col:col + HEAD_DIM] = (
                o[g * WINDOW:(g + 1) * WINDOW].astype(BF16))


def _attention(qkv, sink):
    b, s, _ = qkv.shape
    aw = N_HEADS * HEAD_DIM
    kvw = N_KV_HEADS * HEAD_DIM
    tq = ATTN_TILE
    per = tq // WINDOW
    nblk = s // WINDOW
    kcol = aw // kvw
    vcol = kcol + 1
    halo = lambda col, off: pl.BlockSpec(
        (None, WINDOW, kvw),
        lambda bi, i, sk: (bi, jnp.clip(i * per + off, 0, nblk - 1), col))
    main = lambda col: pl.BlockSpec((None, tq, kvw), lambda bi, i, sk: (bi, i, col))
    return pl.pallas_call(
        functools.partial(_attn_kernel, tq=tq),
        grid_spec=pltpu.PrefetchScalarGridSpec(
            num_scalar_prefetch=1,
            grid=(b, s // tq),
            in_specs=[pl.BlockSpec((None, tq, aw), lambda bi, i, sk: (bi, i, 0)),
                      halo(kcol, -1), main(kcol), halo(kcol, per),
                      halo(vcol, -1), main(vcol), halo(vcol, per)],
            out_specs=pl.BlockSpec((None, tq, aw), lambda bi, i, sk: (bi, i, 0)),
            scratch_shapes=[pltpu.VMEM((N_KV_HEADS * per, 3 * WINDOW, Q_PER_KV * WINDOW), F32),
                            pltpu.VMEM((N_KV_HEADS * per, 3 * WINDOW, Q_PER_KV * WINDOW), BF16),
                            pltpu.VMEM((N_KV_HEADS * per, Q_PER_KV * WINDOW), F32)]),
        out_shape=jax.ShapeDtypeStruct((b, s, aw), BF16),
        compiler_params=_cparams(("parallel", "parallel")),
        name="attention",
    )(sink, qkv, qkv, qkv, qkv, qkv, qkv, qkv)


ROW_WORDS = 1024
ROW_SUB = ROW_WORDS // LANES


def _store_tile_rows(ref, val, base=0, lead=()):
    r = val.shape[0]
    for j in range(ROW_SUB):
        ref[lead + (pl.ds(base * ROW_SUB + j, r, stride=ROW_SUB), slice(None))] = val[:, j * LANES:(j + 1) * LANES]


def _load_tile_rows(ref, base, r, lead=()):
    return jnp.concatenate(
        [ref[lead + (pl.ds(base * ROW_SUB + j, r, stride=ROW_SUB), slice(None))] for j in range(ROW_SUB)], axis=1)


def _tile_row(ref, row):
    return ref.at[pl.ds(pl.multiple_of(row * ROW_SUB, ROW_SUB), ROW_SUB), :]


def _post_kernel(wp_ref, o_ref, gate_ref, x_ref, g1_ref, sh2_ref, sc2_ref, n2g_ref, cc_ref, cs_ref,
                 wfo_ref, wao_ref, wout_ref, wr_ref, br_ref, x1_ref, h2_ref, route_ref, cnt_ref):
    tm = x_ref.shape[0]
    d = x_ref.shape[1]

    @pl.when(pl.program_id(0) == 0)
    def _():
        cnt_ref[...] = jnp.zeros_like(cnt_ref)

    re, im = _unpack_pair(wp_ref[...])
    re, im = re.astype(BF16), im.astype(BF16)
    cc, cs = cc_ref[...], cs_ref[...]
    gd = FOURIER_GROUP_DIM
    fm = jnp.concatenate(
        [(jnp.dot(re[:, g * gd:(g + 1) * gd], cc, preferred_element_type=F32)
          + jnp.dot(im[:, g * gd:(g + 1) * gd], cs, preferred_element_type=F32)).astype(BF16)
         for g in range(FOURIER_GROUPS)], axis=1)
    y_f = jnp.dot(fm, wfo_ref[...], preferred_element_type=F32)
    y_a = jnp.dot(o_ref[...], wao_ref[...], preferred_element_type=F32)
    merged = (_sigmoid(gate_ref[:, :d].astype(F32)) * y_f
              + _sigmoid(gate_ref[:, d:].astype(F32)) * y_a).astype(BF16)
    x1 = x_ref[...] + g1_ref[...] * jnp.dot(merged, wout_ref[...], preferred_element_type=F32)
    x1_ref[...] = x1
    ms = jnp.mean(x1 * x1, axis=-1, keepdims=True)
    h2 = (x1 * lax.rsqrt(ms + EPS) * n2g_ref[...]) * (1.0 + sc2_ref[...]) + sh2_ref[...]
    half = d // 2
    _store_tile_rows(h2_ref, _pack_pair(h2[:, :half], h2[:, half:]))

    h_hi = h2.astype(BF16)
    h_lo = (h2 - h_hi.astype(F32)).astype(BF16)
    wr = wr_ref[...]
    r_hi = jnp.dot(h_hi, wr, preferred_element_type=F32)
    r_lo = jnp.dot(h_lo, wr, preferred_element_type=F32)
    logits = ((r_hi[:, :ROUTE_LANES] + r_hi[:, ROUTE_LANES:])
              + (r_lo[:, :ROUTE_LANES] + r_lo[:, ROUTE_LANES:])) + br_ref[...]
    lane = lax.broadcasted_iota(I32, (tm, ROUTE_LANES), 1)
    big = jnp.int32(ROUTE_LANES)
    is_g = lane < N_GROUPS
    gl = jnp.where(is_g, logits, NEG)
    gmax = jnp.max(gl, axis=-1, keepdims=True)
    gidx = jnp.min(jnp.where(gl == gmax, lane, big), axis=-1, keepdims=True)
    p_g = 1.0 / jnp.sum(jnp.where(is_g, jnp.exp(gl - gmax), 0.0), axis=-1, keepdims=True)
    in_grp = (lane >= N_GROUPS) & (lane < N_GROUPS + N_EXPERTS) & (
        lax.shift_right_logical(lane - N_GROUPS, jnp.full(lane.shape, 3, I32)) == gidx)
    el = jnp.where(in_grp, logits, NEG)
    e1v = jnp.max(el, axis=-1, keepdims=True)
    e1i = jnp.min(jnp.where(el == e1v, lane, big), axis=-1, keepdims=True)
    el2 = jnp.where(lane == e1i, NEG, el)
    e2v = jnp.max(el2, axis=-1, keepdims=True)
    e2i = jnp.min(jnp.where(el2 == e2v, lane, big), axis=-1, keepdims=True)
    t = jnp.exp(e2v - e1v)
    w1 = p_g / (1.0 + t)
    w2 = w1 * t
    sel1, sel2 = lane == e1i, lane == e2i
    member = jnp.where(sel1 | sel2, 1.0, 0.0)
    rr = lax.broadcasted_iota(I32, (tm, tm), 0)
    cc_i = lax.broadcasted_iota(I32, (tm, tm), 1)
    tri = jnp.where(cc_i < rr, 1.0, 0.0).astype(BF16)
    prefix = jnp.dot(tri, member.astype(BF16), preferred_element_type=F32) + cnt_ref[...]
    rank1 = jnp.sum(jnp.where(sel1, prefix, 0.0), axis=-1, keepdims=True)
    rank2 = jnp.sum(jnp.where(sel2, prefix, 0.0), axis=-1, keepdims=True)
    cnt_ref[...] = cnt_ref[...] + jnp.sum(member, axis=0, keepdims=True)
    key_scale = float(1 << RANK_BITS)
    key1 = (e1i - N_GROUPS).astype(F32) * key_scale + rank1
    key2 = (e2i - N_GROUPS).astype(F32) * key_scale + rank2
    route = jnp.zeros((tm, ROUTE_LANES), F32)
    for k, val in enumerate((key1, key2, w1, w2)):
        route = jnp.where(lane == k, val, route)
    route_ref[...] = route


def _post(wp, attn, gates, x, g1, sh2, sc2, norm2_g, cc, cs, wfo_b, wao_b, wout_b, wr, br):
    b, s, d = x.shape
    t = b * s
    tm = ROW_TILE
    per_b_tiles = s // tm
    flat = lambda a: a.reshape(t, a.shape[-1])
    row = lambda w: pl.BlockSpec((tm, w), lambda i: (i, 0))
    per_b = pl.BlockSpec((None, 1, d), lambda i: (i // per_b_tiles, 0, 0))
    const = lambda a: _resident(a.shape, lambda i: (0,) * a.ndim)
    fw = wp.shape[-1]
    aw = attn.shape[-1]
    return pl.pallas_call(
        _post_kernel,
        grid=(t // tm,),
        in_specs=[row(fw), row(aw), row(2 * d), row(d), per_b, per_b, per_b,
                  const(norm2_g), const(cc), const(cs), const(wfo_b), const(wao_b), const(wout_b),
                  const(wr), const(br)],
        out_specs=[row(d), pl.BlockSpec((tm * ROW_SUB, LANES), lambda i: (i, 0)), row(ROUTE_LANES),
                   pl.BlockSpec((1, ROUTE_LANES), lambda i: (0, 0))],
        out_shape=[jax.ShapeDtypeStruct((t, d), F32), jax.ShapeDtypeStruct((t * ROW_SUB, LANES), I32),
                   jax.ShapeDtypeStruct((t, ROUTE_LANES), F32), jax.ShapeDtypeStruct((1, ROUTE_LANES), F32)],
        compiler_params=_cparams(("arbitrary",)),
        name="post_router",
    )(flat(wp), flat(attn), flat(gates), flat(x), g1, sh2, sc2, norm2_g, cc, cs, wfo_b, wao_b, wout_b, wr, br)


def _expert_kernel(dest_ref, pos_ref, seq_ref, nused_ref, h_hbm, zeros_hbm, w1_hbm, w3_hbm, w2_hbm, y_ref,
                   rowtok, xbuf, sem, zsem, w1f, w3f, w2f, wsem, w1b, w3b, w2b, *, rows):
    j = pl.program_id(0)
    nblk = pl.num_programs(0)
    slot = j % 2

    def weight_copies(k, buf):
        e = seq_ref[k]
        return [pltpu.make_async_copy(src.at[e], dst.at[buf], wsem.at[buf])
                for src, dst in ((w1_hbm, w1f), (w3_hbm, w3f), (w2_hbm, w2f))]

    def start_gather(blk, sl):
        group = DMA_GROUP * TOP_K

        def body(gi, carry):
            r0 = gi * group
            toks = [rowtok[blk * rows + r0 + q] for q in range(group)]
            for q, tok in enumerate(toks):
                pltpu.make_async_copy(_tile_row(h_hbm, tok), _tile_row(xbuf.at[sl], r0 + q), sem.at[sl]).start()
            return carry

        lax.fori_loop(0, rows // group, body, 0)

    @pl.when(j == 0)
    def _():
        for cp in weight_copies(0, 0):
            cp.start()
        zero = pltpu.make_async_copy(zeros_hbm, rowtok, zsem)
        zero.start()
        zero.wait()
        unroll = 16

        def scatter(i, carry):
            dests = [dest_ref[i * unroll + u] for u in range(unroll)]
            tok0 = i * (unroll // TOP_K)
            for u, dst in enumerate(dests):
                rowtok[dst] = tok0 + u // TOP_K
            return carry

        lax.fori_loop(j, j + dest_ref.shape[0] // unroll, scatter, 0)
        start_gather(0, 0)

    @pl.when(j + 1 < nblk)
    def _():
        start_gather(j + 1, 1 - slot)

    pos = pos_ref[j]
    pos_prev = pos_ref[jnp.maximum(j - 1, 0)]

    @pl.when((j == 0) | (pos != pos_prev))
    def _():
        buf = pos % 2
        for cp in weight_copies(pos, buf):
            cp.wait()

        @pl.when(pos + 1 < nused_ref[1])
        def _():
            for cp in weight_copies(pos + 1, 1 - buf):
                cp.start()

        w1b[...] = w1f[buf].astype(BF16)
        w3b[...] = w3f[buf].astype(BF16)
        w2b[...] = w2f[buf].astype(BF16)

    pltpu.make_async_copy(h_hbm.at[pl.ds(0, rows * ROW_SUB), :], xbuf.at[slot], sem.at[slot]).wait()

    @pl.when(j < nused_ref[0])
    def _():
        hi, lo = _unpack_pair(_load_tile_rows(xbuf, 0, rows, lead=(slot,)))
        xb = jnp.concatenate([hi.astype(BF16), lo.astype(BF16)], axis=1)
        a = jnp.dot(xb, w1b[...], preferred_element_type=F32)
        g = jnp.dot(xb, w3b[...], preferred_element_type=F32)
        hid = (a * _sigmoid(a) * g).astype(BF16)
        y = jnp.dot(hid, w2b[...], preferred_element_type=F32)
        half = y.shape[1] // 2
        _store_tile_rows(y_ref, _pack_pair(y[:, :half], y[:, half:]))

    @pl.when(j >= nused_ref[0])
    def _():
        y_ref[...] = jnp.zeros_like(y_ref)


def _experts(h2p, dest, block_pos, expert_seq, n_used, n_rows, w1, w3, w2):
    rows = MOE_ROWS
    blk = rows * ROW_SUB
    d, f = w1.shape[1], w1.shape[2]
    hbm = pl.BlockSpec(memory_space=pl.ANY)
    return pl.pallas_call(
        functools.partial(_expert_kernel, rows=rows),
        grid_spec=pltpu.PrefetchScalarGridSpec(
            num_scalar_prefetch=4,
            grid=(n_rows // rows,),
            in_specs=[hbm, hbm, hbm, hbm, hbm],
            out_specs=pl.BlockSpec((blk, LANES), lambda j, *_: (j, 0)),
            scratch_shapes=[pltpu.SMEM((n_rows,), I32), pltpu.VMEM((2, blk, LANES), I32),
                            pltpu.SemaphoreType.DMA((2,)), pltpu.SemaphoreType.DMA(()),
                            pltpu.VMEM((2, d, f), F32), pltpu.VMEM((2, d, f), F32), pltpu.VMEM((2, f, d), F32),
                            pltpu.SemaphoreType.DMA((2,)),
                            pltpu.VMEM((d, f), BF16), pltpu.VMEM((d, f), BF16), pltpu.VMEM((f, d), BF16)]),
        out_shape=jax.ShapeDtypeStruct((n_rows * ROW_SUB, LANES), I32),
        compiler_params=_cparams(("arbitrary",), disable_bounds_checks=True),
        name="experts",
    )(dest, block_pos, expert_seq, n_used, h2p, jnp.zeros((n_rows,), I32), w1, w3, w2)


def _combine_kernel(dest_ref, y_hbm, x1_ref, route_ref, g2_ref, o_ref, ybuf, sem, *, tm):
    i = pl.program_id(0)
    n = pl.num_programs(0)
    slot = i % 2

    def start_gather(tile, sl):
        def body(gi, carry):
            r0 = gi * DMA_GROUP
            base = (tile * tm + r0) * TOP_K
            dests = [dest_ref[base + q] for q in range(DMA_GROUP * TOP_K)]
            for q, dest in enumerate(dests):
                pltpu.make_async_copy(_tile_row(y_hbm, dest),
                                      _tile_row(ybuf.at[sl], (q % TOP_K) * tm + r0 + q // TOP_K),
                                      sem.at[sl]).start()
            return carry

        lax.fori_loop(0, tm // DMA_GROUP, body, 0)

    @pl.when(i == 0)
    def _():
        start_gather(0, 0)

    @pl.when(i + 1 < n)
    def _():
        start_gather(i + 1, 1 - slot)

    pltpu.make_async_copy(y_hbm.at[pl.ds(0, TOP_K * tm * ROW_SUB), :], ybuf.at[slot], sem.at[slot]).wait()
    route = route_ref[...]
    w1 = route[:, 2:3]
    w2 = route[:, 3:4]
    hi1, lo1 = _unpack_pair(_load_tile_rows(ybuf, 0, tm, lead=(slot,)))
    hi2, lo2 = _unpack_pair(_load_tile_rows(ybuf, tm, tm, lead=(slot,)))
    half = o_ref.shape[1] // 2
    o_ref[:, :half] = x1_ref[:, :half] + g2_ref[:, :half] * (w1 * hi1 + w2 * hi2)
    o_ref[:, half:] = x1_ref[:, half:] + g2_ref[:, half:] * (w1 * lo1 + w2 * lo2)


def _combine(y, dest, x1, route, g2, s):
    t, d = x1.shape
    tm = ROW_TILE
    per_b_tiles = s // tm
    return pl.pallas_call(
        functools.partial(_combine_kernel, tm=tm),
        grid_spec=pltpu.PrefetchScalarGridSpec(
            num_scalar_prefetch=1,
            grid=(t // tm,),
            in_specs=[pl.BlockSpec(memory_space=pl.ANY),
                      pl.BlockSpec((tm, d), lambda i, ds: (i, 0)),
                      pl.BlockSpec((tm, ROUTE_LANES), lambda i, ds: (i, 0)),
                      pl.BlockSpec((None, 1, d), lambda i, ds: (i // per_b_tiles, 0, 0))],
            out_specs=pl.BlockSpec((tm, d), lambda i, ds: (i, 0)),
            scratch_shapes=[pltpu.VMEM((2, TOP_K * tm * ROW_SUB, LANES), I32), pltpu.SemaphoreType.DMA((2,))]),
        out_shape=jax.ShapeDtypeStruct((t, d), F32),
        compiler_params=_cparams(("arbitrary",), disable_bounds_checks=True),
        name="combine",
    )(dest, y, x1, route, g2)


def _dispatch_plan(route, counts, t):
    rows = MOE_ROWS
    keys = route[:, 0:TOP_K].astype(I32)
    expert = lax.shift_right_logical(keys, jnp.full(keys.shape, RANK_BITS, I32))
    rank = keys & ((1 << RANK_BITS) - 1)
    cnt = counts[0, N_GROUPS:N_GROUPS + N_EXPERTS].astype(I32)
    padded = (cnt + rows - 1) // rows * rows
    pad_end = jnp.cumsum(padded)
    pad_start = pad_end - padded
    onehot = expert[..., None] == jnp.arange(N_EXPERTS, dtype=I32)
    dest = (jnp.sum(jnp.where(onehot, pad_start, 0), axis=-1) + rank).reshape(t * TOP_K)
    n_blocks = (t * TOP_K + N_EXPERTS * (rows - 1) + rows - 1) // rows
    n_used = pad_end[-1:] // rows
    blk = jnp.minimum(jnp.arange(n_blocks, dtype=I32), n_used - 1) * rows
    block_e = jnp.sum((pad_end[None, :] <= blk[:, None]).astype(I32), axis=1)
    owns = cnt > 0
    expert_seq = jnp.argsort(jnp.logical_not(owns), stable=True).astype(I32)
    seq_index = jnp.cumsum(owns.astype(I32)) - 1
    block_pos = jnp.sum(jnp.where(block_e[:, None] == jnp.arange(N_EXPERTS, dtype=I32), seq_index, 0), axis=1)
    used = jnp.concatenate([n_used, jnp.sum(owns.astype(I32), keepdims=True)]).astype(I32)
    return dest, block_pos.astype(I32), expert_seq, used, n_blocks * rows


def _trunk(x, mod, p):
    b, s, d = x.shape
    sh1, sc1, g1, sh2, sc2, g2 = [m.reshape(b, 1, d) for m in jnp.split(mod, 6, axis=-1)]
    u, qkv, gates = _inproj(x, sh1, sc1, p["norm1_g"], p["q_norm_g"], p["k_norm_g"], p["w_in"], _rope_tables(s))
    wp = _seq_fft(u)
    attn = _attention(qkv, p["sink"])
    x1, h2p, route, counts = _post(wp, attn, gates, x, g1, sh2, sc2, p["norm2_g"], p["cc"], p["cs"],
                                   p["w_fourier_out"], p["w_attn_out"], p["w_out"],
                                   p["w_router"], p["b_router"])
    t = b * s
    dest, block_pos, expert_seq, n_used, n_rows = _dispatch_plan(route, counts, t)
    y = _experts(h2p, dest, block_pos, expert_seq, n_used, n_rows, p["w1"], p["w3"], p["w2"])
    out = _combine(y, dest, x1, route, g2, s)
    return out.reshape(b, s, d)


def _channel_dft():
    n = FOURIER_GROUP_DIM
    k = jnp.arange(n, dtype=I32)
    ang = ((k[:, None] * k[None, :]) % n).astype(F32) * (2.0 * math.pi / n)
    return (jnp.cos(ang) * n ** -0.5).astype(BF16), (jnp.sin(ang) * n ** -0.5).astype(BF16)


def kernel(x_prompt, x_sample, c_prompt, c_sample, w_ada, b_ada, norm1_g, w_in, q_norm_g, k_norm_g, sink,
           w_fourier_out, w_attn_out, w_out, norm2_g, w_group, b_group, w_expert, b_expert, w1, w3, w2):
    depth = w_ada.shape[0]
    d = x_prompt.shape[-1]
    bp = c_prompt.shape[0]
    bs = c_sample.shape[0]
    cc, cs = _channel_dft()
    xp, xs = x_prompt, x_sample
    for l in range(depth):
        c_all = jnp.concatenate([c_prompt, c_sample, jnp.zeros((8 - (bp + bs) % 8, d), F32)], axis=0)
        mod = _adaln(c_all, w_ada[l], b_ada[l])
        pad = ROUTE_LANES - N_GROUPS - N_EXPERTS
        w_router = jnp.concatenate([w_group[l], w_expert[l], jnp.zeros((d, pad), F32)], axis=1)
        w_router_hi = w_router.astype(BF16)
        p = {
            "norm1_g": norm1_g[l].reshape(1, d), "norm2_g": norm2_g[l].reshape(1, d),
            "q_norm_g": q_norm_g[l].reshape(1, HEAD_DIM), "k_norm_g": k_norm_g[l].reshape(1, HEAD_DIM),
            "sink": sink[l], "w_in": w_in[l].astype(BF16),
            "w_fourier_out": w_fourier_out[l].astype(BF16), "w_attn_out": w_attn_out[l].astype(BF16),
            "w_out": w_out[l].astype(BF16), "cc": cc, "cs": cs,
            "w_router": jnp.concatenate(
                [w_router_hi, (w_router - w_router_hi.astype(F32)).astype(BF16)], axis=1),
            "b_router": jnp.concatenate([b_group[l], b_expert[l], jnp.zeros((pad,), F32)]).reshape(1, ROUTE_LANES),
            "w1": w1[l], "w3": w3[l], "w2": w2[l],
        }
        xp = _trunk(xp, mod[:bp], p)
        xs = _trunk(xs, mod[bp:bp + bs], p)
    return xp, xs
```

```python
import functools
import math

import jax
import jax.numpy as jnp
from jax import lax
from jax.experimental import pallas as pl
from jax.experimental.pallas import tpu as pltpu

F32 = jnp.float32
BF16 = jnp.bfloat16
I32 = jnp.int32

HEAD_DIM = 128
N_KV_HEADS = 2
Q_PER_KV = 4
N_HEADS = N_KV_HEADS * Q_PER_KV
ROT_DIM = 32
ROPE_THETA = 500000.0
WINDOW = 128
FOURIER_GROUPS = 4
FOURIER_GROUP_DIM = 256
N_GROUPS = 8
EXPERTS_PER_GROUP = 8
N_EXPERTS = N_GROUPS * EXPERTS_PER_GROUP
TOP_K = 2
EPS = 1e-6

LANES = 128
V7X_VMEM_LIMIT = 56 * 1024 * 1024

ROW_TILE = 256
ATTN_TILE = 512
FFT_N1 = 128
FFT_UNROLL = 16
FFT_PITCH_PAD = 8
MOE_ROWS = 256
ROUTE_LANES = 128
RANK_BITS = 17
DMA_GROUP = 8
GATHER_AHEAD = 2
WEIGHT_CHUNKS = 8
NEG = -1e30


def _cparams(sem, **kw):
    return pltpu.CompilerParams(dimension_semantics=sem, vmem_limit_bytes=V7X_VMEM_LIMIT, **kw)


def _resident(shape, index_map):
    return pl.BlockSpec(shape, index_map, pipeline_mode=pl.Buffered(1))


def _sigmoid(x):
    return 0.5 * jnp.tanh(0.5 * x) + 0.5


def _adaln_kernel(c_ref, w_ref, b_ref, o_ref):
    c = c_ref[...]
    s = c * _sigmoid(c)
    o_ref[...] = jnp.dot(s, w_ref[...], precision=lax.Precision.HIGHEST,
                         preferred_element_type=F32) + b_ref[...]


def _adaln(c, w_ada, b_ada):
    r, d = c.shape
    n = w_ada.shape[1]
    tn = 1024
    return pl.pallas_call(
        _adaln_kernel,
        grid=(n // tn,),
        in_specs=[pl.BlockSpec((r, d), lambda j: (0, 0)),
                  pl.BlockSpec((d, tn), lambda j: (0, j)),
                  pl.BlockSpec((1, tn), lambda j: (0, j))],
        out_specs=pl.BlockSpec((r, tn), lambda j: (0, j)),
        out_shape=jax.ShapeDtypeStruct((r, n), F32),
        compiler_params=_cparams(("parallel",)),
        name="adaln",
    )(c, w_ada, b_ada.reshape(1, n))


def _rope_tables(s):
    half = ROT_DIM // 2
    inv_freq = ROPE_THETA ** (-jnp.arange(0, ROT_DIM, 2, dtype=F32) / ROT_DIM)
    ang = jnp.arange(s, dtype=F32)[:, None] * inv_freq[None, :]
    cos, sin = jnp.cos(ang), jnp.sin(ang)
    pad = jnp.zeros((s, LANES - ROT_DIM), F32)
    cos_f = jnp.concatenate([cos, cos, jnp.ones((s, LANES - ROT_DIM), F32)], axis=1)
    sin_up = jnp.concatenate([-sin, jnp.zeros((s, half), F32), pad], axis=1)
    sin_dn = jnp.concatenate([jnp.zeros((s, half), F32), sin, pad], axis=1)
    return cos_f, sin_up, sin_dn


def _inproj_kernel(x_ref, sh_ref, sc_ref, g_ref, qg_ref, kg_ref, cos_ref, sup_ref, sdn_ref, w_ref,
                   u_ref, qkv_ref, gate_ref, *, fw, aw, kvw):
    x = x_ref[...]
    ms = jnp.mean(x * x, axis=-1, keepdims=True)
    h = (x * lax.rsqrt(ms + EPS) * g_ref[...]) * (1.0 + sc_ref[...]) + sh_ref[...]
    hb = h.astype(BF16)
    chunk = 512
    for c0 in range(0, fw, chunk):
        u_ref[:, c0:c0 + chunk] = jnp.dot(hb, w_ref[:, c0:c0 + chunk], preferred_element_type=F32)
    cos_f, sin_up, sin_dn = cos_ref[...], sup_ref[...], sdn_ref[...]
    half = ROT_DIM // 2
    scale = HEAD_DIM ** -0.5
    for c0 in list(range(0, aw, chunk)) + [aw]:
        width = chunk if c0 < aw else kvw
        acc = jnp.dot(hb, w_ref[:, fw + c0:fw + c0 + width], preferred_element_type=F32)
        for hh in range(width // HEAD_DIM):
            col = c0 + hh * HEAD_DIM
            t = acc[:, hh * HEAD_DIM:(hh + 1) * HEAD_DIM]
            is_q = col < aw
            gain = qg_ref[...] if is_q else kg_ref[...]
            t = t * lax.rsqrt(jnp.mean(t * t, axis=-1, keepdims=True) + EPS) * gain
            t = (t * cos_f + pltpu.roll(t, LANES - half, axis=1) * sin_up
                 + pltpu.roll(t, half, axis=1) * sin_dn)
            if is_q:
                t = t * scale
            qkv_ref[:, col:col + HEAD_DIM] = t.astype(BF16)
    v0 = fw + aw + kvw
    qkv_ref[:, aw + kvw:aw + 2 * kvw] = jnp.dot(
        hb, w_ref[:, v0:v0 + kvw], preferred_element_type=F32).astype(BF16)
    g0 = v0 + kvw
    gw = gate_ref.shape[-1]
    for c0 in range(0, gw, chunk):
        gate_ref[:, c0:c0 + chunk] = jnp.dot(
            hb, w_ref[:, g0 + c0:g0 + c0 + chunk], preferred_element_type=F32).astype(BF16)


def _inproj(x, shift, scale, norm_g, q_g, k_g, w_in_b, tables):
    b, s, d = x.shape
    fw = FOURIER_GROUPS * FOURIER_GROUP_DIM
    aw = N_HEADS * HEAD_DIM
    kvw = N_KV_HEADS * HEAD_DIM
    gw = 2 * d
    tm = ROW_TILE
    row = lambda w: pl.BlockSpec((None, tm, w), lambda bi, i: (bi, i, 0))
    per_b = pl.BlockSpec((None, 1, d), lambda bi, i: (bi, 0, 0))
    const = lambda w: pl.BlockSpec((1, w), lambda bi, i: (0, 0))
    tab = pl.BlockSpec((tm, LANES), lambda bi, i: (i, 0))
    return pl.pallas_call(
        functools.partial(_inproj_kernel, fw=fw, aw=aw, kvw=kvw),
        grid=(b, s // tm),
        in_specs=[row(d), per_b, per_b, const(d), const(HEAD_DIM), const(HEAD_DIM), tab, tab, tab,
                  _resident(w_in_b.shape, lambda bi, i: (0, 0))],
        out_specs=[row(fw), row(aw + 2 * kvw), row(gw)],
        out_shape=[jax.ShapeDtypeStruct((b, s, fw), F32),
                   jax.ShapeDtypeStruct((b, s, aw + 2 * kvw), BF16),
                   jax.ShapeDtypeStruct((b, s, gw), BF16)],
        compiler_params=_cparams(("parallel", "parallel")),
        name="inproj",
    )(x, shift, scale, norm_g, q_g, k_g, *tables, w_in_b)


def _pack_pair(a, b):
    ab = lax.bitcast_convert_type(a.astype(BF16).astype(F32), I32)
    bb = lax.bitcast_convert_type(b.astype(BF16).astype(F32), I32)
    return ab | lax.shift_right_logical(bb, jnp.full(bb.shape, 16, I32))


def _unpack_pair(p):
    hi = lax.bitcast_convert_type(p & jnp.int32(-65536), F32)
    lo = lax.bitcast_convert_type(lax.shift_left(p, jnp.full(p.shape, 16, I32)), F32)
    return hi, lo


def _fft_tables(s):
    n1 = FFT_N1
    n2 = s // n1
    k1 = jnp.arange(n1, dtype=I32)
    nn = (n2 * jnp.arange(n1, dtype=I32))[None, None, :] + jnp.arange(n2, dtype=I32)[:, None, None]
    ph = (k1[None, :, None] * nn) % s
    ang = ph.astype(F32) * (2.0 * math.pi / s)
    sc1 = n1 ** -0.5
    g = jnp.concatenate([jnp.cos(ang) * sc1, -jnp.sin(ang) * sc1], axis=1).astype(BF16)
    k2 = jnp.arange(n2, dtype=I32)
    ang2 = ((k2[:, None] * k2[None, :]) % n2).astype(F32) * (2.0 * math.pi / n2)
    c2, s2 = jnp.cos(ang2) * n2 ** -0.5, jnp.sin(ang2) * n2 ** -0.5
    f2 = jnp.concatenate([jnp.concatenate([c2, s2], axis=1),
                          jnp.concatenate([-s2, c2], axis=1)], axis=0).astype(BF16)
    return g, f2


def _fft_kernel(u_ref, g_ref, f2_ref, o_ref, y_scr, *, n1, n2, unroll):
    pitch = n1 + FFT_PITCH_PAD

    def stage1(i, carry):
        for uu in range(unroll):
            m = i * unroll + uu
            xm = u_ref[pl.ds(m, n1, stride=n2), :].astype(BF16)
            y = jnp.dot(g_ref[m], xm, preferred_element_type=F32)
            y_scr[pl.ds(pl.multiple_of(m * pitch, 8), n1), :] = _pack_pair(y[:n1], y[n1:])
        return carry

    lax.fori_loop(0, n2 // unroll, stage1, 0)
    f2 = f2_ref[...]

    def stage2(i, carry):
        for uu in range(unroll):
            k1 = i * unroll + uu
            yr, yi = _unpack_pair(y_scr[pl.ds(k1, n2, stride=pitch), :])
            rhs = jnp.concatenate([yr.astype(BF16), yi.astype(BF16)], axis=0)
            z = jnp.dot(f2, rhs, preferred_element_type=F32)
            y_scr[pl.ds(k1, n2, stride=pitch), :] = _pack_pair(z[:n2], z[n2:])
        return carry

    lax.fori_loop(0, n1 // unroll, stage2, 0)

    def compact(i, carry):
        for uu in range(unroll):
            k2 = i * unroll + uu
            o_ref[pl.ds(pl.multiple_of(k2 * n1, n1), n1), :] = y_scr[pl.ds(pl.multiple_of(k2 * pitch, 8), n1), :]
        return carry

    lax.fori_loop(0, n2 // unroll, compact, 0)


def _seq_fft(u):
    b, s, c = u.shape
    n1 = FFT_N1
    n2 = s // n1
    g, f2 = _fft_tables(s)
    unroll = min(FFT_UNROLL, n2)
    blk = pl.BlockSpec((None, s, LANES), lambda bi, j: (bi, 0, j))
    return pl.pallas_call(
        functools.partial(_fft_kernel, n1=n1, n2=n2, unroll=unroll),
        grid=(b, c // LANES),
        in_specs=[blk, _resident(g.shape, lambda bi, j: (0, 0, 0)), _resident(f2.shape, lambda bi, j: (0, 0))],
        out_specs=blk,
        out_shape=jax.ShapeDtypeStruct((b, s, c), I32),
        scratch_shapes=[pltpu.VMEM((n2 * (n1 + FFT_PITCH_PAD), LANES), I32)],
        compiler_params=_cparams(("parallel", "parallel")),
        name="seq_fft",
    )(u, g, f2)


def _attn_kernel(sink_ref, q_ref, kp_ref, kc_ref, kn_ref, vp_ref, vc_ref, vn_ref, o_ref,
                 s_scr, p_scr, inv_scr, *, tq):
    i = pl.program_id(1)
    first = i == 0
    last = i == pl.num_programs(1) - 1
    nsub = tq // WINDOW
    nq = Q_PER_KV * WINDOW
    nk = 3 * WINDOW
    c = lax.broadcasted_iota(I32, (nk, nq), 0)
    r = lax.broadcasted_iota(I32, (nk, nq), 1) & (WINDOW - 1)
    band = ((c >= r) & (c < WINDOW)) | ((c >= WINDOW) & (c < 2 * WINDOW)) | ((c >= 2 * WINDOW) & (c - 2 * WINDOW <= r))
    pairs = [(h, sb) for h in range(N_KV_HEADS) for sb in range(nsub)]
    kcat, vt = [], []
    for h in range(N_KV_HEADS):
        hs = slice(h * HEAD_DIM, (h + 1) * HEAD_DIM)
        kcat.append(jnp.concatenate([kp_ref[:, hs], kc_ref[:, hs], kn_ref[:, hs]], axis=0))
        vcat = jnp.concatenate([vp_ref[:, hs], vc_ref[:, hs], vn_ref[:, hs]], axis=0)
        vt.append(vcat.astype(F32).T.astype(BF16))

    for n, (h, sb) in enumerate(pairs):
        q4 = jnp.concatenate(
            [q_ref[sb * WINDOW:(sb + 1) * WINDOW,
                   (h * Q_PER_KV + g) * HEAD_DIM:(h * Q_PER_KV + g + 1) * HEAD_DIM]
             for g in range(Q_PER_KV)], axis=0)
        kw = kcat[h][sb * WINDOW:(sb + 3) * WINDOW]
        s_scr[n] = lax.dot_general(kw, q4, (((1,), (1,)), ((), ())), preferred_element_type=F32)

    for n, (h, sb) in enumerate(pairs):
        sc = jnp.where(band, s_scr[n], NEG)
        if sb == 0:
            sc = jnp.where((c < WINDOW) & first, NEG, sc)
        if sb == nsub - 1:
            sc = jnp.where((c >= 2 * WINDOW) & last, NEG, sc)
        sink = jnp.concatenate(
            [jnp.full((1, WINDOW), sink_ref[h * Q_PER_KV + g], F32) for g in range(Q_PER_KV)], axis=1)
        m = jnp.maximum(jnp.max(sc, axis=0, keepdims=True), sink)
        p = jnp.exp(sc - m)
        denom = jnp.sum(p, axis=0, keepdims=True) + jnp.exp(sink - m)
        p_scr[n] = p.astype(BF16)
        inv_scr[pl.ds(n, 1), :] = 1.0 / denom

    for n, (h, sb) in enumerate(pairs):
        ot = jnp.dot(vt[h][:, sb * WINDOW:(sb + 3) * WINDOW], p_scr[n], preferred_element_type=F32)
        o = (ot * inv_scr[pl.ds(n, 1), :]).T
        for g in range(Q_PER_KV):
            col = (h * Q_PER_KV + g) * HEAD_DIM
            o_ref[sb * WINDOW:(sb + 1) * WINDOW, col:col + HEAD_DIM] = (
                o[g * WINDOW:(g + 1) * WINDOW].astype(BF16))


def _attention(qkv, sink):
    b, s, _ = qkv.shape
    aw = N_HEADS * HEAD_DIM
    kvw = N_KV_HEADS * HEAD_DIM
    tq = ATTN_TILE
    per = tq // WINDOW
    nblk = s // WINDOW
    kcol = aw // kvw
    vcol = kcol + 1
    halo = lambda col, off: pl.BlockSpec(
        (None, WINDOW, kvw),
        lambda bi, i, sk: (bi, jnp.clip(i * per + off, 0, nblk - 1), col))
    main = lambda col: pl.BlockSpec((None, tq, kvw), lambda bi, i, sk: (bi, i, col))
    return pl.pallas_call(
        functools.partial(_attn_kernel, tq=tq),
        grid_spec=pltpu.PrefetchScalarGridSpec(
            num_scalar_prefetch=1,
            grid=(b, s // tq),
            in_specs=[pl.BlockSpec((None, tq, aw), lambda bi, i, sk: (bi, i, 0)),
                      halo(kcol, -1), main(kcol), halo(kcol, per),
                      halo(vcol, -1), main(vcol), halo(vcol, per)],
            out_specs=pl.BlockSpec((None, tq, aw), lambda bi, i, sk: (bi, i, 0)),
            scratch_shapes=[pltpu.VMEM((N_KV_HEADS * per, 3 * WINDOW, Q_PER_KV * WINDOW), F32),
                            pltpu.VMEM((N_KV_HEADS * per, 3 * WINDOW, Q_PER_KV * WINDOW), BF16),
                            pltpu.VMEM((N_KV_HEADS * per, Q_PER_KV * WINDOW), F32)]),
        out_shape=jax.ShapeDtypeStruct((b, s, aw), BF16),
        compiler_params=_cparams(("parallel", "parallel")),
        name="attention",
    )(sink, qkv, qkv, qkv, qkv, qkv, qkv, qkv)


ROW_WORDS = 1024
ROW_SUB = ROW_WORDS // LANES


def _store_tile_rows(ref, val, base=0, lead=()):
    r = val.shape[0]
    for j in range(ROW_SUB):
        ref[lead + (pl.ds(base * ROW_SUB + j, r, stride=ROW_SUB), slice(None))] = val[:, j * LANES:(j + 1) * LANES]


def _load_tile_rows(ref, base, r, lead=()):
    return jnp.concatenate(
        [ref[lead + (pl.ds(base * ROW_SUB + j, r, stride=ROW_SUB), slice(None))] for j in range(ROW_SUB)], axis=1)


def _tile_row(ref, row):
    return ref.at[pl.ds(pl.multiple_of(row * ROW_SUB, ROW_SUB), ROW_SUB), :]


def _post_kernel(wp_ref, o_ref, gate_ref, x_ref, g1_ref, sh2_ref, sc2_ref, n2g_ref, cc_ref, cs_ref,
                 wfo_ref, wao_ref, wout_ref, wr_ref, br_ref, x1_ref, h2_ref, route_ref, cnt_ref):
    tm = x_ref.shape[0]
    d = x_ref.shape[1]

    @pl.when(pl.program_id(0) == 0)
    def _():
        cnt_ref[...] = jnp.zeros_like(cnt_ref)

    re, im = _unpack_pair(wp_ref[...])
    re, im = re.astype(BF16), im.astype(BF16)
    cc, cs = cc_ref[...], cs_ref[...]
    gd = FOURIER_GROUP_DIM
    fm = jnp.concatenate(
        [(jnp.dot(re[:, g * gd:(g + 1) * gd], cc, preferred_element_type=F32)
          + jnp.dot(im[:, g * gd:(g + 1) * gd], cs, preferred_element_type=F32)).astype(BF16)
         for g in range(FOURIER_GROUPS)], axis=1)
    y_f = jnp.dot(fm, wfo_ref[...], preferred_element_type=F32)
    y_a = jnp.dot(o_ref[...], wao_ref[...], preferred_element_type=F32)
    merged = (_sigmoid(gate_ref[:, :d].astype(F32)) * y_f
              + _sigmoid(gate_ref[:, d:].astype(F32)) * y_a).astype(BF16)
    x1 = x_ref[...] + g1_ref[...] * jnp.dot(merged, wout_ref[...], preferred_element_type=F32)
    x1_ref[...] = x1
    ms = jnp.mean(x1 * x1, axis=-1, keepdims=True)
    h2 = (x1 * lax.rsqrt(ms + EPS) * n2g_ref[...]) * (1.0 + sc2_ref[...]) + sh2_ref[...]
    half = d // 2
    _store_tile_rows(h2_ref, _pack_pair(h2[:, :half], h2[:, half:]))

    h_hi = h2.astype(BF16)
    h_lo = (h2 - h_hi.astype(F32)).astype(BF16)
    wr = wr_ref[...]
    r_hi = jnp.dot(h_hi, wr, preferred_element_type=F32)
    r_lo = jnp.dot(h_lo, wr, preferred_element_type=F32)
    logits = ((r_hi[:, :ROUTE_LANES] + r_hi[:, ROUTE_LANES:])
              + (r_lo[:, :ROUTE_LANES] + r_lo[:, ROUTE_LANES:])) + br_ref[...]
    lane = lax.broadcasted_iota(I32, (tm, ROUTE_LANES), 1)
    big = jnp.int32(ROUTE_LANES)
    is_g = lane < N_GROUPS
    gl = jnp.where(is_g, logits, NEG)
    gmax = jnp.max(gl, axis=-1, keepdims=True)
    gidx = jnp.min(jnp.where(gl == gmax, lane, big), axis=-1, keepdims=True)
    p_g = 1.0 / jnp.sum(jnp.where(is_g, jnp.exp(gl - gmax), 0.0), axis=-1, keepdims=True)
    in_grp = (lane >= N_GROUPS) & (lane < N_GROUPS + N_EXPERTS) & (
        lax.shift_right_logical(lane - N_GROUPS, jnp.full(lane.shape, 3, I32)) == gidx)
    el = jnp.where(in_grp, logits, NEG)
    e1v = jnp.max(el, axis=-1, keepdims=True)
    e1i = jnp.min(jnp.where(el == e1v, lane, big), axis=-1, keepdims=True)
    el2 = jnp.where(lane == e1i, NEG, el)
    e2v = jnp.max(el2, axis=-1, keepdims=True)
    e2i = jnp.min(jnp.where(el2 == e2v, lane, big), axis=-1, keepdims=True)
    t = jnp.exp(e2v - e1v)
    w1 = p_g / (1.0 + t)
    w2 = w1 * t
    sel1, sel2 = lane == e1i, lane == e2i
    member = jnp.where(sel1 | sel2, 1.0, 0.0)
    rr = lax.broadcasted_iota(I32, (tm, tm), 0)
    cc_i = lax.broadcasted_iota(I32, (tm, tm), 1)
    tri = jnp.where(cc_i < rr, 1.0, 0.0).astype(BF16)
    prefix = jnp.dot(tri, member.astype(BF16), preferred_element_type=F32) + cnt_ref[...]
    rank1 = jnp.sum(jnp.where(sel1, prefix, 0.0), axis=-1, keepdims=True)
    rank2 = jnp.sum(jnp.where(sel2, prefix, 0.0), axis=-1, keepdims=True)
    cnt_ref[...] = cnt_ref[...] + jnp.sum(member, axis=0, keepdims=True)
    key_scale = float(1 << RANK_BITS)
    key1 = (e1i - N_GROUPS).astype(F32) * key_scale + rank1
    key2 = (e2i - N_GROUPS).astype(F32) * key_scale + rank2
    route = jnp.zeros((tm, ROUTE_LANES), F32)
    for k, val in enumerate((key1, key2, w1, w2)):
        route = jnp.where(lane == k, val, route)
    route_ref[...] = route


def _post(wp, attn, gates, x, g1, sh2, sc2, norm2_g, cc, cs, wfo_b, wao_b, wout_b, wr, br):
    b, s, d = x.shape
    t = b * s
    tm = ROW_TILE
    per_b_tiles = s // tm
    flat = lambda a: a.reshape(t, a.shape[-1])
    row = lambda w: pl.BlockSpec((tm, w), lambda i: (i, 0))
    per_b = pl.BlockSpec((None, 1, d), lambda i: (i // per_b_tiles, 0, 0))
    const = lambda a: _resident(a.shape, lambda i: (0,) * a.ndim)
    fw = wp.shape[-1]
    aw = attn.shape[-1]
    return pl.pallas_call(
        _post_kernel,
        grid=(t // tm,),
        in_specs=[row(fw), row(aw), row(2 * d), row(d), per_b, per_b, per_b,
                  const(norm2_g), const(cc), const(cs), const(wfo_b), const(wao_b), const(wout_b),
                  const(wr), const(br)],
        out_specs=[row(d), pl.BlockSpec((tm * ROW_SUB, LANES), lambda i: (i, 0)), row(ROUTE_LANES),
                   pl.BlockSpec((1, ROUTE_LANES), lambda i: (0, 0))],
        out_shape=[jax.ShapeDtypeStruct((t, d), F32), jax.ShapeDtypeStruct((t * ROW_SUB, LANES), I32),
                   jax.ShapeDtypeStruct((t, ROUTE_LANES), F32), jax.ShapeDtypeStruct((1, ROUTE_LANES), F32)],
        compiler_params=_cparams(("arbitrary",)),
        name="post_router",
    )(flat(wp), flat(attn), flat(gates), flat(x), g1, sh2, sc2, norm2_g, cc, cs, wfo_b, wao_b, wout_b, wr, br)


def _expert_kernel(dest_ref, pos_ref, seq_ref, nused_ref, h_hbm, zeros_hbm, w1_hbm, w3_hbm, w2_hbm, y_ref,
                   rowtok, xbuf, sem, zsem, w1f, w3f, w2f, wsem, w1b, w3b, w2b, *, rows):
    j = pl.program_id(0)
    nblk = pl.num_programs(0)
    nslot = GATHER_AHEAD + 1
    slot = lax.rem(j, nslot)
    weights = ((w1_hbm, w1f), (w3_hbm, w3f), (w2_hbm, w2f))

    def start_weights(k, buf):
        e = seq_ref[k]
        for src, dst in weights:
            n = src.shape[1] // WEIGHT_CHUNKS
            for c in range(WEIGHT_CHUNKS):
                pltpu.make_async_copy(src.at[e, pl.ds(c * n, n), :], dst.at[buf, pl.ds(c * n, n), :],
                                      wsem.at[buf]).start()

    def wait_weights(k, buf):
        e = seq_ref[k]
        for src, dst in weights:
            pltpu.make_async_copy(src.at[e], dst.at[buf], wsem.at[buf]).wait()

    def start_gather(blk, sl):
        group = DMA_GROUP * TOP_K

        def body(gi, carry):
            r0 = gi * group
            toks = [rowtok[blk * rows + r0 + q] for q in range(group)]
            for q, tok in enumerate(toks):
                pltpu.make_async_copy(_tile_row(h_hbm, tok), _tile_row(xbuf.at[sl], r0 + q), sem.at[sl]).start()
            return carry

        lax.fori_loop(0, rows // group, body, 0)

    @pl.when(j == 0)
    def _():
        start_weights(0, 0)
        zero = pltpu.make_async_copy(zeros_hbm, rowtok, zsem)
        zero.start()
        zero.wait()
        unroll = 16

        def scatter(i, carry):
            dests = [dest_ref[i * unroll + u] for u in range(unroll)]
            tok0 = i * (unroll // TOP_K)
            for u, dst in enumerate(dests):
                rowtok[dst] = tok0 + u // TOP_K
            return carry

        lax.fori_loop(j, j + dest_ref.shape[0] // unroll, scatter, 0)
        for b in range(GATHER_AHEAD):
            @pl.when(b < nblk)
            def _():
                start_gather(b, b)

    pos = pos_ref[j]
    pos_prev = pos_ref[jnp.maximum(j - 1, 0)]

    @pl.when((j == 0) | (pos != pos_prev))
    def _():
        buf = pos % 2
        wait_weights(pos, buf)

        @pl.when(pos + 1 < nused_ref[1])
        def _():
            start_weights(pos + 1, 1 - buf)

        w1b[...] = w1f[buf].astype(BF16)
        w3b[...] = w3f[buf].astype(BF16)
        w2b[...] = w2f[buf].astype(BF16)

    @pl.when(j + GATHER_AHEAD < nblk)
    def _():
        start_gather(j + GATHER_AHEAD, lax.rem(j + GATHER_AHEAD, nslot))

    pltpu.make_async_copy(h_hbm.at[pl.ds(0, rows * ROW_SUB), :], xbuf.at[slot], sem.at[slot]).wait()

    @pl.when(j < nused_ref[0])
    def _():
        hi, lo = _unpack_pair(_load_tile_rows(xbuf, 0, rows, lead=(slot,)))
        xb = jnp.concatenate([hi.astype(BF16), lo.astype(BF16)], axis=1)
        a = jnp.dot(xb, w1b[...], preferred_element_type=F32)
        g = jnp.dot(xb, w3b[...], preferred_element_type=F32)
        hid = (a * _sigmoid(a) * g).astype(BF16)
        y = jnp.dot(hid, w2b[...], preferred_element_type=F32)
        half = y.shape[1] // 2
        _store_tile_rows(y_ref, _pack_pair(y[:, :half], y[:, half:]))

    @pl.when(j >= nused_ref[0])
    def _():
        y_ref[...] = jnp.zeros_like(y_ref)


def _experts(h2p, dest, block_pos, expert_seq, n_used, n_rows, w1, w3, w2):
    rows = MOE_ROWS
    blk = rows * ROW_SUB
    d, f = w1.shape[1], w1.shape[2]
    hbm = pl.BlockSpec(memory_space=pl.ANY)
    return pl.pallas_call(
        functools.partial(_expert_kernel, rows=rows),
        grid_spec=pltpu.PrefetchScalarGridSpec(
            num_scalar_prefetch=4,
            grid=(n_rows // rows,),
            in_specs=[hbm, hbm, hbm, hbm, hbm],
            out_specs=pl.BlockSpec((blk, LANES), lambda j, *_: (j, 0)),
            scratch_shapes=[pltpu.SMEM((n_rows,), I32), pltpu.VMEM((GATHER_AHEAD + 1, blk, LANES), I32),
                            pltpu.SemaphoreType.DMA((GATHER_AHEAD + 1,)), pltpu.SemaphoreType.DMA(()),
                            pltpu.VMEM((2, d, f), F32), pltpu.VMEM((2, d, f), F32), pltpu.VMEM((2, f, d), F32),
                            pltpu.SemaphoreType.DMA((2,)),
                            pltpu.VMEM((d, f), BF16), pltpu.VMEM((d, f), BF16), pltpu.VMEM((f, d), BF16)]),
        out_shape=jax.ShapeDtypeStruct((n_rows * ROW_SUB, LANES), I32),
        compiler_params=_cparams(("arbitrary",), disable_bounds_checks=True),
        name="experts",
    )(dest, block_pos, expert_seq, n_used, h2p, jnp.zeros((n_rows,), I32), w1, w3, w2)


def _combine_kernel(dest_ref, y_hbm, x1_ref, route_ref, g2_ref, o_ref, ybuf, sem, *, tm):
    i = pl.program_id(0)
    n = pl.num_programs(0)
    slot = i % 2

    def start_gather(tile, sl):
        def body(gi, carry):
            r0 = gi * DMA_GROUP
            base = (tile * tm + r0) * TOP_K
            dests = [dest_ref[base + q] for q in range(DMA_GROUP * TOP_K)]
            for q, dest in enumerate(dests):
                pltpu.make_async_copy(_tile_row(y_hbm, dest),
                                      _tile_row(ybuf.at[sl], (q % TOP_K) * tm + r0 + q // TOP_K),
                                      sem.at[sl]).start()
            return carry

        lax.fori_loop(0, tm // DMA_GROUP, body, 0)

    @pl.when(i == 0)
    def _():
        start_gather(0, 0)

    @pl.when(i + 1 < n)
    def _():
        start_gather(i + 1, 1 - slot)

    pltpu.make_async_copy(y_hbm.at[pl.ds(0, TOP_K * tm * ROW_SUB), :], ybuf.at[slot], sem.at[slot]).wait()
    route = route_ref[...]
    w1 = route[:, 2:3]
    w2 = route[:, 3:4]
    hi1, lo1 = _unpack_pair(_load_tile_rows(ybuf, 0, tm, lead=(slot,)))
    hi2, lo2 = _unpack_pair(_load_tile_rows(ybuf, tm, tm, lead=(slot,)))
    half = o_ref.shape[1] // 2
    o_ref[:, :half] = x1_ref[:, :half] + g2_ref[:, :half] * (w1 * hi1 + w2 * hi2)
    o_ref[:, half:] = x1_ref[:, half:] + g2_ref[:, half:] * (w1 * lo1 + w2 * lo2)


def _combine(y, dest, x1, route, g2, s):
    t, d = x1.shape
    tm = ROW_TILE
    per_b_tiles = s // tm
    return pl.pallas_call(
        functools.partial(_combine_kernel, tm=tm),
        grid_spec=pltpu.PrefetchScalarGridSpec(
            num_scalar_prefetch=1,
            grid=(t // tm,),
            in_specs=[pl.BlockSpec(memory_space=pl.ANY),
                      pl.BlockSpec((tm, d), lambda i, ds: (i, 0)),
                      pl.BlockSpec((tm, ROUTE_LANES), lambda i, ds: (i, 0)),
                      pl.BlockSpec((None, 1, d), lambda i, ds: (i // per_b_tiles, 0, 0))],
            out_specs=pl.BlockSpec((tm, d), lambda i, ds: (i, 0)),
            scratch_shapes=[pltpu.VMEM((2, TOP_K * tm * ROW_SUB, LANES), I32), pltpu.SemaphoreType.DMA((2,))]),
        out_shape=jax.ShapeDtypeStruct((t, d), F32),
        compiler_params=_cparams(("arbitrary",), disable_bounds_checks=True),
        name="combine",
    )(dest, y, x1, route, g2)


def _dispatch_plan(route, counts, t):
    rows = MOE_ROWS
    keys = route[:, 0:TOP_K].astype(I32)
    expert = lax.shift_right_logical(keys, jnp.full(keys.shape, RANK_BITS, I32))
    rank = keys & ((1 << RANK_BITS) - 1)
    cnt = counts[0, N_GROUPS:N_GROUPS + N_EXPERTS].astype(I32)
    padded = (cnt + rows - 1) // rows * rows
    pad_end = jnp.cumsum(padded)
    pad_start = pad_end - padded
    onehot = expert[..., None] == jnp.arange(N_EXPERTS, dtype=I32)
    dest = (jnp.sum(jnp.where(onehot, pad_start, 0), axis=-1) + rank).reshape(t * TOP_K)
    n_blocks = (t * TOP_K + N_EXPERTS * (rows - 1) + rows - 1) // rows
    n_used = pad_end[-1:] // rows
    blk = jnp.minimum(jnp.arange(n_blocks, dtype=I32), n_used - 1) * rows
    block_e = jnp.sum((pad_end[None, :] <= blk[:, None]).astype(I32), axis=1)
    owns = cnt > 0
    expert_seq = jnp.argsort(jnp.logical_not(owns), stable=True).astype(I32)
    seq_index = jnp.cumsum(owns.astype(I32)) - 1
    block_pos = jnp.sum(jnp.where(block_e[:, None] == jnp.arange(N_EXPERTS, dtype=I32), seq_index, 0), axis=1)
    used = jnp.concatenate([n_used, jnp.sum(owns.astype(I32), keepdims=True)]).astype(I32)
    return dest, block_pos.astype(I32), expert_seq, used, n_blocks * rows


def _trunk(x, mod, p):
    b, s, d = x.shape
    sh1, sc1, g1, sh2, sc2, g2 = [m.reshape(b, 1, d) for m in jnp.split(mod, 6, axis=-1)]
    u, qkv, gates = _inproj(x, sh1, sc1, p["norm1_g"], p["q_norm_g"], p["k_norm_g"], p["w_in"], _rope_tables(s))
    wp = _seq_fft(u)
    attn = _attention(qkv, p["sink"])
    x1, h2p, route, counts = _post(wp, attn, gates, x, g1, sh2, sc2, p["norm2_g"], p["cc"], p["cs"],
                                   p["w_fourier_out"], p["w_attn_out"], p["w_out"],
                                   p["w_router"], p["b_router"])
    t = b * s
    dest, block_pos, expert_seq, n_used, n_rows = _dispatch_plan(route, counts, t)
    y = _experts(h2p, dest, block_pos, expert_seq, n_used, n_rows, p["w1"], p["w3"], p["w2"])
    out = _combine(y, dest, x1, route, g2, s)
    return out.reshape(b, s, d)


def _channel_dft():
    n = FOURIER_GROUP_DIM
    k = jnp.arange(n, dtype=I32)
    ang = ((k[:, None] * k[None, :]) % n).astype(F32) * (2.0 * math.pi / n)
    return (jnp.cos(ang) * n ** -0.5).astype(BF16), (jnp.sin(ang) * n ** -0.5).astype(BF16)


def kernel(x_prompt, x_sample, c_prompt, c_sample, w_ada, b_ada, norm1_g, w_in, q_norm_g, k_norm_g, sink,
           w_fourier_out, w_attn_out, w_out, norm2_g, w_group, b_group, w_expert, b_expert, w1, w3, w2):
    depth = w_ada.shape[0]
    d = x_prompt.shape[-1]
    bp = c_prompt.shape[0]
    bs = c_sample.shape[0]
    cc, cs = _channel_dft()
    xp, xs = x_prompt, x_sample
    for l in range(depth):
        c_all = jnp.concatenate([c_prompt, c_sample, jnp.zeros((8 - (bp + bs) % 8, d), F32)], axis=0)
        mod = _adaln(c_all, w_ada[l], b_ada[l])
        pad = ROUTE_LANES - N_GROUPS - N_EXPERTS
        w_router = jnp.concatenate([w_group[l], w_expert[l], jnp.zeros((d, pad), F32)], axis=1)
        w_router_hi = w_router.astype(BF16)
        p = {
            "norm1_g": norm1_g[l].reshape(1, d), "norm2_g": norm2_g[l].reshape(1, d),
            "q_norm_g": q_norm_g[l].reshape(1, HEAD_DIM), "k_norm_g": k_norm_g[l].reshape(1, HEAD_DIM),
            "sink": sink[l], "w_in": w_in[l].astype(BF16),
            "w_fourier_out": w_fourier_out[l].astype(BF16), "w_attn_out": w_attn_out[l].astype(BF16),
            "w_out": w_out[l].astype(BF16), "cc": cc, "cs": cs,
            "w_router": jnp.concatenate(
                [w_router_hi, (w_router - w_router_hi.astype(F32)).astype(BF16)], axis=1),
            "b_router": jnp.concatenate([b_group[l], b_expert[l], jnp.zeros((pad,), F32)]).reshape(1, ROUTE_LANES),
            "w1": w1[l], "w3": w3[l], "w2": w2[l],
        }
        xp = _trunk(xp, mod[:bp], p)
        xs = _trunk(xs, mod[bp:bp + bs], p)
    return xp, xs
```

```python
import functools
import math

import jax
import jax.numpy as jnp
from jax import lax
from jax.experimental import pallas as pl
from jax.experimental.pallas import tpu as pltpu

F32 = jnp.float32
BF16 = jnp.bfloat16
I32 = jnp.int32

HEAD_DIM = 128
N_KV_HEADS = 2
Q_PER_KV = 4
N_HEADS = N_KV_HEADS * Q_PER_KV
ROT_DIM = 32
ROPE_THETA = 500000.0
WINDOW = 128
FOURIER_GROUPS = 4
FOURIER_GROUP_DIM = 256
N_GROUPS = 8
EXPERTS_PER_GROUP = 8
N_EXPERTS = N_GROUPS * EXPERTS_PER_GROUP
TOP_K = 2
EPS = 1e-6

LANES = 128
V7X_VMEM_LIMIT = 56 * 1024 * 1024

ROW_TILE = 256
ATTN_TILE = 512
FFT_N1 = 128
FFT_UNROLL = 16
FFT_PITCH_PAD = 8
MOE_ROWS = 256
ROUTE_LANES = 128
RANK_BITS = 17
DMA_GROUP = 8
GATHER_AHEAD = 2
WEIGHT_CHUNKS = 8
DEST_CHUNK = 8192
NEG = -1e30


def _cparams(sem, **kw):
    return pltpu.CompilerParams(dimension_semantics=sem, vmem_limit_bytes=V7X_VMEM_LIMIT, **kw)


def _resident(shape, index_map):
    return pl.BlockSpec(shape, index_map, pipeline_mode=pl.Buffered(1))


def _sigmoid(x):
    return 0.5 * jnp.tanh(0.5 * x) + 0.5


def _adaln_kernel(c_ref, w_ref, b_ref, o_ref):
    c = c_ref[...]
    s = c * _sigmoid(c)
    o_ref[...] = jnp.dot(s, w_ref[...], precision=lax.Precision.HIGHEST,
                         preferred_element_type=F32) + b_ref[...]


def _adaln(c, w_ada, b_ada):
    r, d = c.shape
    n = w_ada.shape[1]
    tn = 1024
    return pl.pallas_call(
        _adaln_kernel,
        grid=(n // tn,),
        in_specs=[pl.BlockSpec((r, d), lambda j: (0, 0)),
                  pl.BlockSpec((d, tn), lambda j: (0, j)),
                  pl.BlockSpec((1, tn), lambda j: (0, j))],
        out_specs=pl.BlockSpec((r, tn), lambda j: (0, j)),
        out_shape=jax.ShapeDtypeStruct((r, n), F32),
        compiler_params=_cparams(("parallel",)),
        name="adaln",
    )(c, w_ada, b_ada.reshape(1, n))


def _rope_tables(s):
    half = ROT_DIM // 2
    inv_freq = ROPE_THETA ** (-jnp.arange(0, ROT_DIM, 2, dtype=F32) / ROT_DIM)
    ang = jnp.arange(s, dtype=F32)[:, None] * inv_freq[None, :]
    cos, sin = jnp.cos(ang), jnp.sin(ang)
    pad = jnp.zeros((s, LANES - ROT_DIM), F32)
    cos_f = jnp.concatenate([cos, cos, jnp.ones((s, LANES - ROT_DIM), F32)], axis=1)
    sin_up = jnp.concatenate([-sin, jnp.zeros((s, half), F32), pad], axis=1)
    sin_dn = jnp.concatenate([jnp.zeros((s, half), F32), sin, pad], axis=1)
    return cos_f, sin_up, sin_dn


def _inproj_kernel(x_ref, sh_ref, sc_ref, g_ref, qg_ref, kg_ref, cos_ref, sup_ref, sdn_ref, w_ref,
                   u_ref, qkv_ref, gate_ref, *, fw, aw, kvw):
    x = x_ref[...]
    ms = jnp.mean(x * x, axis=-1, keepdims=True)
    h = (x * lax.rsqrt(ms + EPS) * g_ref[...]) * (1.0 + sc_ref[...]) + sh_ref[...]
    hb = h.astype(BF16)
    chunk = 512
    for c0 in range(0, fw, chunk):
        u_ref[:, c0:c0 + chunk] = jnp.dot(hb, w_ref[:, c0:c0 + chunk], preferred_element_type=F32)
    cos_f, sin_up, sin_dn = cos_ref[...], sup_ref[...], sdn_ref[...]
    half = ROT_DIM // 2
    scale = HEAD_DIM ** -0.5
    for c0 in list(range(0, aw, chunk)) + [aw]:
        width = chunk if c0 < aw else kvw
        acc = jnp.dot(hb, w_ref[:, fw + c0:fw + c0 + width], preferred_element_type=F32)
        for hh in range(width // HEAD_DIM):
            col = c0 + hh * HEAD_DIM
            t = acc[:, hh * HEAD_DIM:(hh + 1) * HEAD_DIM]
            is_q = col < aw
            gain = qg_ref[...] if is_q else kg_ref[...]
            t = t * lax.rsqrt(jnp.mean(t * t, axis=-1, keepdims=True) + EPS) * gain
            t = (t * cos_f + pltpu.roll(t, LANES - half, axis=1) * sin_up
                 + pltpu.roll(t, half, axis=1) * sin_dn)
            if is_q:
                t = t * scale
            qkv_ref[:, col:col + HEAD_DIM] = t.astype(BF16)
    v0 = fw + aw + kvw
    qkv_ref[:, aw + kvw:aw + 2 * kvw] = jnp.dot(
        hb, w_ref[:, v0:v0 + kvw], preferred_element_type=F32).astype(BF16)
    g0 = v0 + kvw
    gw = gate_ref.shape[-1]
    for c0 in range(0, gw, chunk):
        gate_ref[:, c0:c0 + chunk] = jnp.dot(
            hb, w_ref[:, g0 + c0:g0 + c0 + chunk], preferred_element_type=F32).astype(BF16)


def _inproj(x, shift, scale, norm_g, q_g, k_g, w_in_b, tables):
    b, s, d = x.shape
    fw = FOURIER_GROUPS * FOURIER_GROUP_DIM
    aw = N_HEADS * HEAD_DIM
    kvw = N_KV_HEADS * HEAD_DIM
    gw = 2 * d
    tm = ROW_TILE
    row = lambda w: pl.BlockSpec((None, tm, w), lambda bi, i: (bi, i, 0))
    per_b = pl.BlockSpec((None, 1, d), lambda bi, i: (bi, 0, 0))
    const = lambda w: pl.BlockSpec((1, w), lambda bi, i: (0, 0))
    tab = pl.BlockSpec((tm, LANES), lambda bi, i: (i, 0))
    return pl.pallas_call(
        functools.partial(_inproj_kernel, fw=fw, aw=aw, kvw=kvw),
        grid=(b, s // tm),
        in_specs=[row(d), per_b, per_b, const(d), const(HEAD_DIM), const(HEAD_DIM), tab, tab, tab,
                  _resident(w_in_b.shape, lambda bi, i: (0, 0))],
        out_specs=[row(fw), row(aw + 2 * kvw), row(gw)],
        out_shape=[jax.ShapeDtypeStruct((b, s, fw), F32),
                   jax.ShapeDtypeStruct((b, s, aw + 2 * kvw), BF16),
                   jax.ShapeDtypeStruct((b, s, gw), BF16)],
        compiler_params=_cparams(("parallel", "parallel")),
        name="inproj",
    )(x, shift, scale, norm_g, q_g, k_g, *tables, w_in_b)


def _pack_pair(a, b):
    ab = lax.bitcast_convert_type(a.astype(BF16).astype(F32), I32)
    bb = lax.bitcast_convert_type(b.astype(BF16).astype(F32), I32)
    return ab | lax.shift_right_logical(bb, jnp.full(bb.shape, 16, I32))


def _unpack_pair(p):
    hi = lax.bitcast_convert_type(p & jnp.int32(-65536), F32)
    lo = lax.bitcast_convert_type(lax.shift_left(p, jnp.full(p.shape, 16, I32)), F32)
    return hi, lo


def _fft_tables(s):
    n1 = FFT_N1
    n2 = s // n1
    k1 = jnp.arange(n1, dtype=I32)
    nn = (n2 * jnp.arange(n1, dtype=I32))[None, None, :] + jnp.arange(n2, dtype=I32)[:, None, None]
    ph = (k1[None, :, None] * nn) % s
    ang = ph.astype(F32) * (2.0 * math.pi / s)
    sc1 = n1 ** -0.5
    g = jnp.concatenate([jnp.cos(ang) * sc1, -jnp.sin(ang) * sc1], axis=1).astype(BF16)
    k2 = jnp.arange(n2, dtype=I32)
    ang2 = ((k2[:, None] * k2[None, :]) % n2).astype(F32) * (2.0 * math.pi / n2)
    c2, s2 = jnp.cos(ang2) * n2 ** -0.5, jnp.sin(ang2) * n2 ** -0.5
    f2 = jnp.concatenate([jnp.concatenate([c2, s2], axis=1),
                          jnp.concatenate([-s2, c2], axis=1)], axis=0).astype(BF16)
    return g, f2


def _fft_kernel(u_ref, g_ref, f2_ref, o_ref, y_scr, *, n1, n2, unroll):
    pitch = n1 + FFT_PITCH_PAD

    def stage1(i, carry):
        for uu in range(unroll):
            m = i * unroll + uu
            xm = u_ref[pl.ds(m, n1, stride=n2), :].astype(BF16)
            y = jnp.dot(g_ref[m], xm, preferred_element_type=F32)
            y_scr[pl.ds(pl.multiple_of(m * pitch, 8), n1), :] = _pack_pair(y[:n1], y[n1:])
        return carry

    lax.fori_loop(0, n2 // unroll, stage1, 0)
    f2 = f2_ref[...]

    def stage2(i, carry):
        for uu in range(unroll):
            k1 = i * unroll + uu
            yr, yi = _unpack_pair(y_scr[pl.ds(k1, n2, stride=pitch), :])
            rhs = jnp.concatenate([yr.astype(BF16), yi.astype(BF16)], axis=0)
            z = jnp.dot(f2, rhs, preferred_element_type=F32)
            y_scr[pl.ds(k1, n2, stride=pitch), :] = _pack_pair(z[:n2], z[n2:])
        return carry

    lax.fori_loop(0, n1 // unroll, stage2, 0)

    def compact(i, carry):
        for uu in range(unroll):
            k2 = i * unroll + uu
            o_ref[pl.ds(pl.multiple_of(k2 * n1, n1), n1), :] = y_scr[pl.ds(pl.multiple_of(k2 * pitch, 8), n1), :]
        return carry

    lax.fori_loop(0, n2 // unroll, compact, 0)


def _seq_fft(u):
    b, s, c = u.shape
    n1 = FFT_N1
    n2 = s // n1
    g, f2 = _fft_tables(s)
    unroll = min(FFT_UNROLL, n2)
    blk = pl.BlockSpec((None, s, LANES), lambda bi, j: (bi, 0, j))
    return pl.pallas_call(
        functools.partial(_fft_kernel, n1=n1, n2=n2, unroll=unroll),
        grid=(b, c // LANES),
        in_specs=[blk, _resident(g.shape, lambda bi, j: (0, 0, 0)), _resident(f2.shape, lambda bi, j: (0, 0))],
        out_specs=blk,
        out_shape=jax.ShapeDtypeStruct((b, s, c), I32),
        scratch_shapes=[pltpu.VMEM((n2 * (n1 + FFT_PITCH_PAD), LANES), I32)],
        compiler_params=_cparams(("parallel", "parallel")),
        name="seq_fft",
    )(u, g, f2)


def _attn_kernel(sink_ref, q_ref, kp_ref, kc_ref, kn_ref, vp_ref, vc_ref, vn_ref, o_ref,
                 s_scr, p_scr, inv_scr, *, tq):
    i = pl.program_id(1)
    first = i == 0
    last = i == pl.num_programs(1) - 1
    nsub = tq // WINDOW
    nq = Q_PER_KV * WINDOW
    nk = 3 * WINDOW
    c = lax.broadcasted_iota(I32, (nk, nq), 0)
    r = lax.broadcasted_iota(I32, (nk, nq), 1) & (WINDOW - 1)
    band = ((c >= r) & (c < WINDOW)) | ((c >= WINDOW) & (c < 2 * WINDOW)) | ((c >= 2 * WINDOW) & (c - 2 * WINDOW <= r))
    pairs = [(h, sb) for h in range(N_KV_HEADS) for sb in range(nsub)]
    kcat, vt = [], []
    for h in range(N_KV_HEADS):
        hs = slice(h * HEAD_DIM, (h + 1) * HEAD_DIM)
        kcat.append(jnp.concatenate([kp_ref[:, hs], kc_ref[:, hs], kn_ref[:, hs]], axis=0))
        vcat = jnp.concatenate([vp_ref[:, hs], vc_ref[:, hs], vn_ref[:, hs]], axis=0)
        vt.append(vcat.astype(F32).T.astype(BF16))

    for n, (h, sb) in enumerate(pairs):
        q4 = jnp.concatenate(
            [q_ref[sb * WINDOW:(sb + 1) * WINDOW,
                   (h * Q_PER_KV + g) * HEAD_DIM:(h * Q_PER_KV + g + 1) * HEAD_DIM]
             for g in range(Q_PER_KV)], axis=0)
        kw = kcat[h][sb * WINDOW:(sb + 3) * WINDOW]
        s_scr[n] = lax.dot_general(kw, q4, (((1,), (1,)), ((), ())), preferred_element_type=F32)

    for n, (h, sb) in enumerate(pairs):
        sc = jnp.where(band, s_scr[n], NEG)
        if sb == 0:
            sc = jnp.where((c < WINDOW) & first, NEG, sc)
        if sb == nsub - 1:
            sc = jnp.where((c >= 2 * WINDOW) & last, NEG, sc)
        sink = jnp.concatenate(
            [jnp.full((1, WINDOW), sink_ref[h * Q_PER_KV + g], F32) for g in range(Q_PER_KV)], axis=1)
        m = jnp.maximum(jnp.max(sc, axis=0, keepdims=True), sink)
        p = jnp.exp(sc - m)
        denom = jnp.sum(p, axis=0, keepdims=True) + jnp.exp(sink - m)
        p_scr[n] = p.astype(BF16)
        inv_scr[pl.ds(n, 1), :] = 1.0 / denom

    for n, (h, sb) in enumerate(pairs):
        ot = jnp.dot(vt[h][:, sb * WINDOW:(sb + 3) * WINDOW], p_scr[n], preferred_element_type=F32)
        o = (ot * inv_scr[pl.ds(n, 1), :]).T
        for g in range(Q_PER_KV):
            col = (h * Q_PER_KV + g) * HEAD_DIM
            o_ref[sb * WINDOW:(sb + 1) * WINDOW, col:col + HEAD_DIM] = (
                o[g * WINDOW:(g + 1) * WINDOW].astype(BF16))


def _attention(qkv, sink):
    b, s, _ = qkv.shape
    aw = N_HEADS * HEAD_DIM
    kvw = N_KV_HEADS * HEAD_DIM
    tq = ATTN_TILE
    per = tq // WINDOW
    nblk = s // WINDOW
    kcol = aw // kvw
    vcol = kcol + 1
    halo = lambda col, off: pl.BlockSpec(
        (None, WINDOW, kvw),
        lambda bi, i, sk: (bi, jnp.clip(i * per + off, 0, nblk - 1), col))
    main = lambda col: pl.BlockSpec((None, tq, kvw), lambda bi, i, sk: (bi, i, col))
    return pl.pallas_call(
        functools.partial(_attn_kernel, tq=tq),
        grid_spec=pltpu.PrefetchScalarGridSpec(
            num_scalar_prefetch=1,
            grid=(b, s // tq),
            in_specs=[pl.BlockSpec((None, tq, aw), lambda bi, i, sk: (bi, i, 0)),
                      halo(kcol, -1), main(kcol), halo(kcol, per),
                      halo(vcol, -1), main(vcol), halo(vcol, per)],
            out_specs=pl.BlockSpec((None, tq, aw), lambda bi, i, sk: (bi, i, 0)),
            scratch_shapes=[pltpu.VMEM((N_KV_HEADS * per, 3 * WINDOW, Q_PER_KV * WINDOW), F32),
                            pltpu.VMEM((N_KV_HEADS * per, 3 * WINDOW, Q_PER_KV * WINDOW), BF16),
                            pltpu.VMEM((N_KV_HEADS * per, Q_PER_KV * WINDOW), F32)]),
        out_shape=jax.ShapeDtypeStruct((b, s, aw), BF16),
        compiler_params=_cparams(("parallel", "parallel")),
        name="attention",
    )(sink, qkv, qkv, qkv, qkv, qkv, qkv, qkv)


ROW_WORDS = 1024
ROW_SUB = ROW_WORDS // LANES


def _store_tile_rows(ref, val, base=0, lead=()):
    r = val.shape[0]
    for j in range(ROW_SUB):
        ref[lead + (pl.ds(base * ROW_SUB + j, r, stride=ROW_SUB), slice(None))] = val[:, j * LANES:(j + 1) * LANES]


def _load_tile_rows(ref, base, r, lead=()):
    return jnp.concatenate(
        [ref[lead + (pl.ds(base * ROW_SUB + j, r, stride=ROW_SUB), slice(None))] for j in range(ROW_SUB)], axis=1)


def _tile_row(ref, row):
    return ref.at[pl.ds(pl.multiple_of(row * ROW_SUB, ROW_SUB), ROW_SUB), :]


def _post_kernel(wp_ref, o_ref, gate_ref, x_ref, g1_ref, sh2_ref, sc2_ref, n2g_ref, cc_ref, cs_ref,
                 wfo_ref, wao_ref, wout_ref, wr_ref, br_ref, cnt_in_ref, h2_all_hbm,
                 x1_ref, h2_ref, route_ref, cnt_ref):
    del h2_all_hbm
    tm = x_ref.shape[0]
    d = x_ref.shape[1]

    @pl.when(pl.program_id(0) == 0)
    def _():
        cnt_ref[...] = cnt_in_ref[...]

    re, im = _unpack_pair(wp_ref[...])
    re, im = re.astype(BF16), im.astype(BF16)
    cc, cs = cc_ref[...], cs_ref[...]
    gd = FOURIER_GROUP_DIM
    fm = jnp.concatenate(
        [(jnp.dot(re[:, g * gd:(g + 1) * gd], cc, preferred_element_type=F32)
          + jnp.dot(im[:, g * gd:(g + 1) * gd], cs, preferred_element_type=F32)).astype(BF16)
         for g in range(FOURIER_GROUPS)], axis=1)
    y_f = jnp.dot(fm, wfo_ref[...], preferred_element_type=F32)
    y_a = jnp.dot(o_ref[...], wao_ref[...], preferred_element_type=F32)
    merged = (_sigmoid(gate_ref[:, :d].astype(F32)) * y_f
              + _sigmoid(gate_ref[:, d:].astype(F32)) * y_a).astype(BF16)
    x1 = x_ref[...] + g1_ref[...] * jnp.dot(merged, wout_ref[...], preferred_element_type=F32)
    x1_ref[...] = x1
    ms = jnp.mean(x1 * x1, axis=-1, keepdims=True)
    h2 = (x1 * lax.rsqrt(ms + EPS) * n2g_ref[...]) * (1.0 + sc2_ref[...]) + sh2_ref[...]
    half = d // 2
    _store_tile_rows(h2_ref, _pack_pair(h2[:, :half], h2[:, half:]))

    h_hi = h2.astype(BF16)
    h_lo = (h2 - h_hi.astype(F32)).astype(BF16)
    wr = wr_ref[...]
    r_hi = jnp.dot(h_hi, wr, preferred_element_type=F32)
    r_lo = jnp.dot(h_lo, wr, preferred_element_type=F32)
    logits = ((r_hi[:, :ROUTE_LANES] + r_hi[:, ROUTE_LANES:])
              + (r_lo[:, :ROUTE_LANES] + r_lo[:, ROUTE_LANES:])) + br_ref[...]
    lane = lax.broadcasted_iota(I32, (tm, ROUTE_LANES), 1)
    big = jnp.int32(ROUTE_LANES)
    is_g = lane < N_GROUPS
    gl = jnp.where(is_g, logits, NEG)
    gmax = jnp.max(gl, axis=-1, keepdims=True)
    gidx = jnp.min(jnp.where(gl == gmax, lane, big), axis=-1, keepdims=True)
    p_g = 1.0 / jnp.sum(jnp.where(is_g, jnp.exp(gl - gmax), 0.0), axis=-1, keepdims=True)
    in_grp = (lane >= N_GROUPS) & (lane < N_GROUPS + N_EXPERTS) & (
        lax.shift_right_logical(lane - N_GROUPS, jnp.full(lane.shape, 3, I32)) == gidx)
    el = jnp.where(in_grp, logits, NEG)
    e1v = jnp.max(el, axis=-1, keepdims=True)
    e1i = jnp.min(jnp.where(el == e1v, lane, big), axis=-1, keepdims=True)
    el2 = jnp.where(lane == e1i, NEG, el)
    e2v = jnp.max(el2, axis=-1, keepdims=True)
    e2i = jnp.min(jnp.where(el2 == e2v, lane, big), axis=-1, keepdims=True)
    t = jnp.exp(e2v - e1v)
    w1 = p_g / (1.0 + t)
    w2 = w1 * t
    sel1, sel2 = lane == e1i, lane == e2i
    member = jnp.where(sel1 | sel2, 1.0, 0.0)
    rr = lax.broadcasted_iota(I32, (tm, tm), 0)
    cc_i = lax.broadcasted_iota(I32, (tm, tm), 1)
    tri = jnp.where(cc_i < rr, 1.0, 0.0).astype(BF16)
    prefix = jnp.dot(tri, member.astype(BF16), preferred_element_type=F32) + cnt_ref[...]
    rank1 = jnp.sum(jnp.where(sel1, prefix, 0.0), axis=-1, keepdims=True)
    rank2 = jnp.sum(jnp.where(sel2, prefix, 0.0), axis=-1, keepdims=True)
    cnt_ref[...] = cnt_ref[...] + jnp.sum(member, axis=0, keepdims=True)
    key_scale = float(1 << RANK_BITS)
    key1 = (e1i - N_GROUPS).astype(F32) * key_scale + rank1
    key2 = (e2i - N_GROUPS).astype(F32) * key_scale + rank2
    route = jnp.zeros((tm, ROUTE_LANES), F32)
    for k, val in enumerate((key1, key2, w1, w2)):
        route = jnp.where(lane == k, val, route)
    route_ref[...] = route


def _post(wp, attn, gates, x, g1, sh2, sc2, norm2_g, cc, cs, wfo_b, wao_b, wout_b, wr, br,
          counts_in, h2_all, tok_off):
    b, s, d = x.shape
    t = b * s
    tm = ROW_TILE
    per_b_tiles = s // tm
    tile_off = tok_off // tm
    flat = lambda a: a.reshape(t, a.shape[-1])
    row = lambda w: pl.BlockSpec((tm, w), lambda i: (i, 0))
    per_b = pl.BlockSpec((None, 1, d), lambda i: (i // per_b_tiles, 0, 0))
    const = lambda a: _resident(a.shape, lambda i: (0,) * a.ndim)
    fw = wp.shape[-1]
    aw = attn.shape[-1]
    return pl.pallas_call(
        _post_kernel,
        grid=(t // tm,),
        in_specs=[row(fw), row(aw), row(2 * d), row(d), per_b, per_b, per_b,
                  const(norm2_g), const(cc), const(cs), const(wfo_b), const(wao_b), const(wout_b),
                  const(wr), const(br), const(counts_in), pl.BlockSpec(memory_space=pl.ANY)],
        out_specs=[row(d), pl.BlockSpec((tm * ROW_SUB, LANES), lambda i: (i + tile_off, 0)), row(ROUTE_LANES),
                   pl.BlockSpec((1, ROUTE_LANES), lambda i: (0, 0))],
        out_shape=[jax.ShapeDtypeStruct((t, d), F32), jax.ShapeDtypeStruct(h2_all.shape, I32),
                   jax.ShapeDtypeStruct((t, ROUTE_LANES), F32), jax.ShapeDtypeStruct((1, ROUTE_LANES), F32)],
        input_output_aliases={16: 1},
        compiler_params=_cparams(("arbitrary",)),
        name="post_router",
    )(flat(wp), flat(attn), flat(gates), flat(x), g1, sh2, sc2, norm2_g, cc, cs, wfo_b, wao_b, wout_b, wr, br,
      counts_in, h2_all)


def _expert_kernel(pos_ref, seq_ref, nused_ref, h_hbm, dest_hbm, zeros_hbm, w1_hbm, w3_hbm, w2_hbm, y_ref,
                   rowtok, dbuf, xbuf, sem, zsem, w1f, w3f, w2f, wsem, w1b, w3b, w2b, *, rows):
    j = pl.program_id(0)
    nblk = pl.num_programs(0)
    nslot = GATHER_AHEAD + 1
    slot = lax.rem(j, nslot)
    weights = ((w1_hbm, w1f), (w3_hbm, w3f), (w2_hbm, w2f))

    def start_weights(k, buf):
        e = seq_ref[k]
        for src, dst in weights:
            n = src.shape[1] // WEIGHT_CHUNKS
            for c in range(WEIGHT_CHUNKS):
                pltpu.make_async_copy(src.at[e, pl.ds(c * n, n), :], dst.at[buf, pl.ds(c * n, n), :],
                                      wsem.at[buf]).start()

    def wait_weights(k, buf):
        e = seq_ref[k]
        for src, dst in weights:
            pltpu.make_async_copy(src.at[e], dst.at[buf], wsem.at[buf]).wait()

    def start_gather(blk, sl):
        group = DMA_GROUP * TOP_K

        def body(gi, carry):
            r0 = gi * group
            toks = [rowtok[blk * rows + r0 + q] for q in range(group)]
            for q, tok in enumerate(toks):
                pltpu.make_async_copy(_tile_row(h_hbm, tok), _tile_row(xbuf.at[sl], r0 + q), sem.at[sl]).start()
            return carry

        lax.fori_loop(0, rows // group, body, 0)

    @pl.when(j == 0)
    def _():
        start_weights(0, 0)
        zero = pltpu.make_async_copy(zeros_hbm, rowtok, zsem)
        zero.start()
        zero.wait()
        unroll = 16
        chunk = dbuf.shape[0]

        def scatter_chunk(c, carry):
            cp = pltpu.make_async_copy(dest_hbm.at[pl.ds(pl.multiple_of(c * chunk, chunk), chunk)], dbuf, zsem)
            cp.start()
            cp.wait()

            def scatter(i, carry2):
                dests = [dbuf[i * unroll + u] for u in range(unroll)]
                tok0 = c * (chunk // TOP_K) + i * (unroll // TOP_K)
                for u, dst in enumerate(dests):
                    rowtok[dst] = tok0 + u // TOP_K
                return carry2

            return lax.fori_loop(j, j + chunk // unroll, scatter, carry)

        lax.fori_loop(j, j + dest_hbm.shape[0] // chunk, scatter_chunk, 0)
        for b in range(GATHER_AHEAD):
            @pl.when(b < nblk)
            def _():
                start_gather(b, b)

    pos = pos_ref[j]
    pos_prev = pos_ref[jnp.maximum(j - 1, 0)]

    @pl.when((j == 0) | (pos != pos_prev))
    def _():
        buf = pos % 2
        wait_weights(pos, buf)

        @pl.when(pos + 1 < nused_ref[1])
        def _():
            start_weights(pos + 1, 1 - buf)

        w1b[...] = w1f[buf].astype(BF16)
        w3b[...] = w3f[buf].astype(BF16)
        w2b[...] = w2f[buf].astype(BF16)

    @pl.when(j + GATHER_AHEAD < nblk)
    def _():
        start_gather(j + GATHER_AHEAD, lax.rem(j + GATHER_AHEAD, nslot))

    pltpu.make_async_copy(h_hbm.at[pl.ds(0, rows * ROW_SUB), :], xbuf.at[slot], sem.at[slot]).wait()

    @pl.when(j < nused_ref[0])
    def _():
        hi, lo = _unpack_pair(_load_tile_rows(xbuf, 0, rows, lead=(slot,)))
        xb = jnp.concatenate([hi.astype(BF16), lo.astype(BF16)], axis=1)
        a = jnp.dot(xb, w1b[...], preferred_element_type=F32)
        g = jnp.dot(xb, w3b[...], preferred_element_type=F32)
        hid = (a * _sigmoid(a) * g).astype(BF16)
        y = jnp.dot(hid, w2b[...], preferred_element_type=F32)
        half = y.shape[1] // 2
        _store_tile_rows(y_ref, _pack_pair(y[:, :half], y[:, half:]))

    @pl.when(j >= nused_ref[0])
    def _():
        y_ref[...] = jnp.zeros_like(y_ref)


def _experts(h2p, dest, block_pos, expert_seq, n_used, n_rows, w1, w3, w2):
    rows = MOE_ROWS
    blk = rows * ROW_SUB
    d, f = w1.shape[1], w1.shape[2]
    hbm = pl.BlockSpec(memory_space=pl.ANY)
    return pl.pallas_call(
        functools.partial(_expert_kernel, rows=rows),
        grid_spec=pltpu.PrefetchScalarGridSpec(
            num_scalar_prefetch=3,
            grid=(n_rows // rows,),
            in_specs=[hbm, hbm, hbm, hbm, hbm, hbm],
            out_specs=pl.BlockSpec((blk, LANES), lambda j, *_: (j, 0)),
            scratch_shapes=[pltpu.SMEM((n_rows,), I32), pltpu.SMEM((math.gcd(DEST_CHUNK, dest.shape[0]),), I32),
                            pltpu.VMEM((GATHER_AHEAD + 1, blk, LANES), I32),
                            pltpu.SemaphoreType.DMA((GATHER_AHEAD + 1,)), pltpu.SemaphoreType.DMA(()),
                            pltpu.VMEM((2, d, f), F32), pltpu.VMEM((2, d, f), F32), pltpu.VMEM((2, f, d), F32),
                            pltpu.SemaphoreType.DMA((2,)),
                            pltpu.VMEM((d, f), BF16), pltpu.VMEM((d, f), BF16), pltpu.VMEM((f, d), BF16)]),
        out_shape=jax.ShapeDtypeStruct((n_rows * ROW_SUB, LANES), I32),
        compiler_params=_cparams(("arbitrary",), disable_bounds_checks=True),
        name="experts",
    )(block_pos, expert_seq, n_used, h2p, dest, jnp.zeros((n_rows,), I32), w1, w3, w2)


def _combine_kernel(dest_ref, y_hbm, x1_ref, route_ref, g2_ref, o_ref, ybuf, sem, *, tm):
    i = pl.program_id(0)
    n = pl.num_programs(0)
    slot = i % 2

    def start_gather(tile, sl):
        def body(gi, carry):
            r0 = gi * DMA_GROUP
            base = (tile * tm + r0) * TOP_K
            dests = [dest_ref[base + q] for q in range(DMA_GROUP * TOP_K)]
            for q, dest in enumerate(dests):
                pltpu.make_async_copy(_tile_row(y_hbm, dest),
                                      _tile_row(ybuf.at[sl], (q % TOP_K) * tm + r0 + q // TOP_K),
                                      sem.at[sl]).start()
            return carry

        lax.fori_loop(0, tm // DMA_GROUP, body, 0)

    @pl.when(i == 0)
    def _():
        start_gather(0, 0)

    @pl.when(i + 1 < n)
    def _():
        start_gather(i + 1, 1 - slot)

    pltpu.make_async_copy(y_hbm.at[pl.ds(0, TOP_K * tm * ROW_SUB), :], ybuf.at[slot], sem.at[slot]).wait()
    route = route_ref[...]
    w1 = route[:, 2:3]
    w2 = route[:, 3:4]
    hi1, lo1 = _unpack_pair(_load_tile_rows(ybuf, 0, tm, lead=(slot,)))
    hi2, lo2 = _unpack_pair(_load_tile_rows(ybuf, tm, tm, lead=(slot,)))
    half = o_ref.shape[1] // 2
    o_ref[:, :half] = x1_ref[:, :half] + g2_ref[:, :half] * (w1 * hi1 + w2 * hi2)
    o_ref[:, half:] = x1_ref[:, half:] + g2_ref[:, half:] * (w1 * lo1 + w2 * lo2)


def _combine(y, dest, x1, route, g2, s):
    t, d = x1.shape
    tm = ROW_TILE
    per_b_tiles = s // tm
    return pl.pallas_call(
        functools.partial(_combine_kernel, tm=tm),
        grid_spec=pltpu.PrefetchScalarGridSpec(
            num_scalar_prefetch=1,
            grid=(t // tm,),
            in_specs=[pl.BlockSpec(memory_space=pl.ANY),
                      pl.BlockSpec((tm, d), lambda i, ds: (i, 0)),
                      pl.BlockSpec((tm, ROUTE_LANES), lambda i, ds: (i, 0)),
                      pl.BlockSpec((None, 1, d), lambda i, ds: (i // per_b_tiles, 0, 0))],
            out_specs=pl.BlockSpec((tm, d), lambda i, ds: (i, 0)),
            scratch_shapes=[pltpu.VMEM((2, TOP_K * tm * ROW_SUB, LANES), I32), pltpu.SemaphoreType.DMA((2,))]),
        out_shape=jax.ShapeDtypeStruct((t, d), F32),
        compiler_params=_cparams(("arbitrary",), disable_bounds_checks=True),
        name="combine",
    )(dest, y, x1, route, g2)


def _dispatch_plan(route_keys, counts, t):
    rows = MOE_ROWS
    keys = route_keys.astype(I32)
    expert = lax.shift_right_logical(keys, jnp.full(keys.shape, RANK_BITS, I32))
    rank = keys & ((1 << RANK_BITS) - 1)
    cnt = counts[0, N_GROUPS:N_GROUPS + N_EXPERTS].astype(I32)
    padded = (cnt + rows - 1) // rows * rows
    pad_end = jnp.cumsum(padded)
    pad_start = pad_end - padded
    onehot = expert[..., None] == jnp.arange(N_EXPERTS, dtype=I32)
    dest = (jnp.sum(jnp.where(onehot, pad_start, 0), axis=-1) + rank).reshape(t * TOP_K)
    n_blocks = (t * TOP_K + N_EXPERTS * (rows - 1) + rows - 1) // rows
    n_used = pad_end[-1:] // rows
    blk = jnp.minimum(jnp.arange(n_blocks, dtype=I32), n_used - 1) * rows
    block_e = jnp.sum((pad_end[None, :] <= blk[:, None]).astype(I32), axis=1)
    owns = cnt > 0
    expert_seq = jnp.argsort(jnp.logical_not(owns), stable=True).astype(I32)
    seq_index = jnp.cumsum(owns.astype(I32)) - 1
    block_pos = jnp.sum(jnp.where(block_e[:, None] == jnp.arange(N_EXPERTS, dtype=I32), seq_index, 0), axis=1)
    used = jnp.concatenate([n_used, jnp.sum(owns.astype(I32), keepdims=True)]).astype(I32)
    return dest, block_pos.astype(I32), expert_seq, used, n_blocks * rows


def _layer(xs, mods, p):
    d = xs[0].shape[-1]
    t_all = sum(x.shape[0] * x.shape[1] for x in xs)
    h2_all = jnp.zeros((t_all * ROW_SUB, LANES), I32)
    counts = jnp.zeros((1, ROUTE_LANES), F32)
    per_group, tok_off = [], 0
    for x, mod in zip(xs, mods):
        b, s, _ = x.shape
        sh1, sc1, g1, sh2, sc2, g2 = [m.reshape(b, 1, d) for m in jnp.split(mod, 6, axis=-1)]
        u, qkv, gates = _inproj(x, sh1, sc1, p["norm1_g"], p["q_norm_g"], p["k_norm_g"], p["w_in"],
                                _rope_tables(s))
        wp = _seq_fft(u)
        attn = _attention(qkv, p["sink"])
        x1, h2_all, route, counts = _post(wp, attn, gates, x, g1, sh2, sc2, p["norm2_g"], p["cc"], p["cs"],
                                          p["w_fourier_out"], p["w_attn_out"], p["w_out"],
                                          p["w_router"], p["b_router"], counts, h2_all, tok_off)
        per_group.append((x1, route, g2, tok_off))
        tok_off += b * s
    route_keys = jnp.concatenate([r[:, 0:TOP_K] for _, r, _, _ in per_group], axis=0)
    dest, block_pos, expert_seq, n_used, n_rows = _dispatch_plan(route_keys, counts, t_all)
    y = _experts(h2_all, dest, block_pos, expert_seq, n_used, n_rows, p["w1"], p["w3"], p["w2"])
    outs = []
    for x, (x1, route, g2, off) in zip(xs, per_group):
        b, s, _ = x.shape
        dest_g = lax.slice(dest, (off * TOP_K,), ((off + b * s) * TOP_K,))
        outs.append(_combine(y, dest_g, x1, route, g2, s).reshape(b, s, d))
    return outs


def _channel_dft():
    n = FOURIER_GROUP_DIM
    k = jnp.arange(n, dtype=I32)
    ang = ((k[:, None] * k[None, :]) % n).astype(F32) * (2.0 * math.pi / n)
    return (jnp.cos(ang) * n ** -0.5).astype(BF16), (jnp.sin(ang) * n ** -0.5).astype(BF16)


def kernel(x_prompt, x_sample, c_prompt, c_sample, w_ada, b_ada, norm1_g, w_in, q_norm_g, k_norm_g, sink,
           w_fourier_out, w_attn_out, w_out, norm2_g, w_group, b_group, w_expert, b_expert, w1, w3, w2):
    depth = w_ada.shape[0]
    d = x_prompt.shape[-1]
    bp = c_prompt.shape[0]
    bs = c_sample.shape[0]
    cc, cs = _channel_dft()
    xp, xs = x_prompt, x_sample
    for l in range(depth):
        c_all = jnp.concatenate([c_prompt, c_sample, jnp.zeros((8 - (bp + bs) % 8, d), F32)], axis=0)
        mod = _adaln(c_all, w_ada[l], b_ada[l])
        pad = ROUTE_LANES - N_GROUPS - N_EXPERTS
        w_router = jnp.concatenate([w_group[l], w_expert[l], jnp.zeros((d, pad), F32)], axis=1)
        w_router_hi = w_router.astype(BF16)
        p = {
            "norm1_g": norm1_g[l].reshape(1, d), "norm2_g": norm2_g[l].reshape(1, d),
            "q_norm_g": q_norm_g[l].reshape(1, HEAD_DIM), "k_norm_g": k_norm_g[l].reshape(1, HEAD_DIM),
            "sink": sink[l], "w_in": w_in[l].astype(BF16),
            "w_fourier_out": w_fourier_out[l].astype(BF16), "w_attn_out": w_attn_out[l].astype(BF16),
            "w_out": w_out[l].astype(BF16), "cc": cc, "cs": cs,
            "w_router": jnp.concatenate(
                [w_router_hi, (w_router - w_router_hi.astype(F32)).astype(BF16)], axis=1),
            "b_router": jnp.concatenate([b_group[l], b_expert[l], jnp.zeros((pad,), F32)]).reshape(1, ROUTE_LANES),
            "w1": w1[l], "w3": w3[l], "w2": w2[l],
        }
        xp, xs = _layer([xp, xs], [mod[:bp], mod[bp:bp + bs]], p)
    return xp, xs
```

```python
import functools
import math

import jax
import jax.numpy as jnp
from jax import lax
from jax.experimental import pallas as pl
from jax.experimental.pallas import tpu as pltpu

F32 = jnp.float32
BF16 = jnp.bfloat16
I32 = jnp.int32

HEAD_DIM = 128
N_KV_HEADS = 2
Q_PER_KV = 4
N_HEADS = N_KV_HEADS * Q_PER_KV
ROT_DIM = 32
ROPE_THETA = 500000.0
WINDOW = 128
FOURIER_GROUPS = 4
FOURIER_GROUP_DIM = 256
N_GROUPS = 8
EXPERTS_PER_GROUP = 8
N_EXPERTS = N_GROUPS * EXPERTS_PER_GROUP
TOP_K = 2
EPS = 1e-6

LANES = 128
V7X_VMEM_LIMIT = 56 * 1024 * 1024

ROW_TILE = 256
ATTN_TILE = 512
FFT_N1 = 128
FFT_UNROLL = 16
FFT_PITCH_PAD = 8
MOE_ROWS = 256
ROUTE_LANES = 128
RANK_BITS = 17
DMA_GROUP = 8
SORT_ROWS = 512
WEIGHT_CHUNKS = 8
DEST_CHUNK = 8192
NEG = -1e30


def _cparams(sem, **kw):
    return pltpu.CompilerParams(dimension_semantics=sem, vmem_limit_bytes=V7X_VMEM_LIMIT, **kw)


def _resident(shape, index_map):
    return pl.BlockSpec(shape, index_map, pipeline_mode=pl.Buffered(1))


def _sigmoid(x):
    return 0.5 * jnp.tanh(0.5 * x) + 0.5


def _adaln_kernel(c_ref, w_ref, b_ref, o_ref):
    c = c_ref[...]
    s = c * _sigmoid(c)
    o_ref[...] = jnp.dot(s, w_ref[...], precision=lax.Precision.HIGHEST,
                         preferred_element_type=F32) + b_ref[...]


def _adaln(c, w_ada, b_ada):
    r, d = c.shape
    n = w_ada.shape[1]
    tn = 1024
    return pl.pallas_call(
        _adaln_kernel,
        grid=(n // tn,),
        in_specs=[pl.BlockSpec((r, d), lambda j: (0, 0)),
                  pl.BlockSpec((d, tn), lambda j: (0, j)),
                  pl.BlockSpec((1, tn), lambda j: (0, j))],
        out_specs=pl.BlockSpec((r, tn), lambda j: (0, j)),
        out_shape=jax.ShapeDtypeStruct((r, n), F32),
        compiler_params=_cparams(("parallel",)),
        name="adaln",
    )(c, w_ada, b_ada.reshape(1, n))


def _rope_tables(s):
    half = ROT_DIM // 2
    inv_freq = ROPE_THETA ** (-jnp.arange(0, ROT_DIM, 2, dtype=F32) / ROT_DIM)
    ang = jnp.arange(s, dtype=F32)[:, None] * inv_freq[None, :]
    cos, sin = jnp.cos(ang), jnp.sin(ang)
    pad = jnp.zeros((s, LANES - ROT_DIM), F32)
    cos_f = jnp.concatenate([cos, cos, jnp.ones((s, LANES - ROT_DIM), F32)], axis=1)
    sin_up = jnp.concatenate([-sin, jnp.zeros((s, half), F32), pad], axis=1)
    sin_dn = jnp.concatenate([jnp.zeros((s, half), F32), sin, pad], axis=1)
    return cos_f, sin_up, sin_dn


def _inproj_kernel(x_ref, sh_ref, sc_ref, g_ref, qg_ref, kg_ref, cos_ref, sup_ref, sdn_ref, w_ref,
                   u_ref, qkv_ref, gate_ref, *, fw, aw, kvw):
    x = x_ref[...]
    ms = jnp.mean(x * x, axis=-1, keepdims=True)
    h = (x * lax.rsqrt(ms + EPS) * g_ref[...]) * (1.0 + sc_ref[...]) + sh_ref[...]
    hb = h.astype(BF16)
    chunk = 512
    for c0 in range(0, fw, chunk):
        u_ref[:, c0:c0 + chunk] = jnp.dot(hb, w_ref[:, c0:c0 + chunk], preferred_element_type=F32)
    cos_f, sin_up, sin_dn = cos_ref[...], sup_ref[...], sdn_ref[...]
    half = ROT_DIM // 2
    scale = HEAD_DIM ** -0.5
    for c0 in list(range(0, aw, chunk)) + [aw]:
        width = chunk if c0 < aw else kvw
        acc = jnp.dot(hb, w_ref[:, fw + c0:fw + c0 + width], preferred_element_type=F32)
        for hh in range(width // HEAD_DIM):
            col = c0 + hh * HEAD_DIM
            t = acc[:, hh * HEAD_DIM:(hh + 1) * HEAD_DIM]
            is_q = col < aw
            gain = qg_ref[...] if is_q else kg_ref[...]
            t = t * lax.rsqrt(jnp.mean(t * t, axis=-1, keepdims=True) + EPS) * gain
            t = (t * cos_f + pltpu.roll(t, LANES - half, axis=1) * sin_up
                 + pltpu.roll(t, half, axis=1) * sin_dn)
            if is_q:
                t = t * scale
            qkv_ref[:, col:col + HEAD_DIM] = t.astype(BF16)
    v0 = fw + aw + kvw
    qkv_ref[:, aw + kvw:aw + 2 * kvw] = jnp.dot(
        hb, w_ref[:, v0:v0 + kvw], preferred_element_type=F32).astype(BF16)
    g0 = v0 + kvw
    gw = gate_ref.shape[-1]
    for c0 in range(0, gw, chunk):
        gate_ref[:, c0:c0 + chunk] = jnp.dot(
            hb, w_ref[:, g0 + c0:g0 + c0 + chunk], preferred_element_type=F32).astype(BF16)


def _inproj(x, shift, scale, norm_g, q_g, k_g, w_in_b, tables):
    b, s, d = x.shape
    fw = FOURIER_GROUPS * FOURIER_GROUP_DIM
    aw = N_HEADS * HEAD_DIM
    kvw = N_KV_HEADS * HEAD_DIM
    gw = 2 * d
    tm = ROW_TILE
    row = lambda w: pl.BlockSpec((None, tm, w), lambda bi, i: (bi, i, 0))
    per_b = pl.BlockSpec((None, 1, d), lambda bi, i: (bi, 0, 0))
    const = lambda w: pl.BlockSpec((1, w), lambda bi, i: (0, 0))
    tab = pl.BlockSpec((tm, LANES), lambda bi, i: (i, 0))
    return pl.pallas_call(
        functools.partial(_inproj_kernel, fw=fw, aw=aw, kvw=kvw),
        grid=(b, s // tm),
        in_specs=[row(d), per_b, per_b, const(d), const(HEAD_DIM), const(HEAD_DIM), tab, tab, tab,
                  _resident(w_in_b.shape, lambda bi, i: (0, 0))],
        out_specs=[row(fw), row(aw + 2 * kvw), row(gw)],
        out_shape=[jax.ShapeDtypeStruct((b, s, fw), F32),
                   jax.ShapeDtypeStruct((b, s, aw + 2 * kvw), BF16),
                   jax.ShapeDtypeStruct((b, s, gw), BF16)],
        compiler_params=_cparams(("parallel", "parallel")),
        name="inproj",
    )(x, shift, scale, norm_g, q_g, k_g, *tables, w_in_b)


def _pack_pair(a, b):
    ab = lax.bitcast_convert_type(a.astype(BF16).astype(F32), I32)
    bb = lax.bitcast_convert_type(b.astype(BF16).astype(F32), I32)
    return ab | lax.shift_right_logical(bb, jnp.full(bb.shape, 16, I32))


def _unpack_pair(p):
    hi = lax.bitcast_convert_type(p & jnp.int32(-65536), F32)
    lo = lax.bitcast_convert_type(lax.shift_left(p, jnp.full(p.shape, 16, I32)), F32)
    return hi, lo


def _fft_tables(s):
    n1 = FFT_N1
    n2 = s // n1
    k1 = jnp.arange(n1, dtype=I32)
    nn = (n2 * jnp.arange(n1, dtype=I32))[None, None, :] + jnp.arange(n2, dtype=I32)[:, None, None]
    ph = (k1[None, :, None] * nn) % s
    ang = ph.astype(F32) * (2.0 * math.pi / s)
    sc1 = n1 ** -0.5
    g = jnp.concatenate([jnp.cos(ang) * sc1, -jnp.sin(ang) * sc1], axis=1).astype(BF16)
    k2 = jnp.arange(n2, dtype=I32)
    ang2 = ((k2[:, None] * k2[None, :]) % n2).astype(F32) * (2.0 * math.pi / n2)
    c2, s2 = jnp.cos(ang2) * n2 ** -0.5, jnp.sin(ang2) * n2 ** -0.5
    f2 = jnp.concatenate([jnp.concatenate([c2, s2], axis=1),
                          jnp.concatenate([-s2, c2], axis=1)], axis=0).astype(BF16)
    return g, f2


def _fft_kernel(u_ref, g_ref, f2_ref, o_ref, y_scr, *, n1, n2, unroll):
    pitch = n1 + FFT_PITCH_PAD

    def stage1(i, carry):
        for uu in range(unroll):
            m = i * unroll + uu
            xm = u_ref[pl.ds(m, n1, stride=n2), :].astype(BF16)
            y = jnp.dot(g_ref[m], xm, preferred_element_type=F32)
            y_scr[pl.ds(pl.multiple_of(m * pitch, 8), n1), :] = _pack_pair(y[:n1], y[n1:])
        return carry

    lax.fori_loop(0, n2 // unroll, stage1, 0)
    f2 = f2_ref[...]

    def stage2(i, carry):
        for uu in range(unroll):
            k1 = i * unroll + uu
            yr, yi = _unpack_pair(y_scr[pl.ds(k1, n2, stride=pitch), :])
            rhs = jnp.concatenate([yr.astype(BF16), yi.astype(BF16)], axis=0)
            z = jnp.dot(f2, rhs, preferred_element_type=F32)
            y_scr[pl.ds(k1, n2, stride=pitch), :] = _pack_pair(z[:n2], z[n2:])
        return carry

    lax.fori_loop(0, n1 // unroll, stage2, 0)

    def compact(i, carry):
        for uu in range(unroll):
            k2 = i * unroll + uu
            o_ref[pl.ds(pl.multiple_of(k2 * n1, n1), n1), :] = y_scr[pl.ds(pl.multiple_of(k2 * pitch, 8), n1), :]
        return carry

    lax.fori_loop(0, n2 // unroll, compact, 0)


def _seq_fft(u):
    b, s, c = u.shape
    n1 = FFT_N1
    n2 = s // n1
    g, f2 = _fft_tables(s)
    unroll = min(FFT_UNROLL, n2)
    blk = pl.BlockSpec((None, s, LANES), lambda bi, j: (bi, 0, j))
    return pl.pallas_call(
        functools.partial(_fft_kernel, n1=n1, n2=n2, unroll=unroll),
        grid=(b, c // LANES),
        in_specs=[blk, _resident(g.shape, lambda bi, j: (0, 0, 0)), _resident(f2.shape, lambda bi, j: (0, 0))],
        out_specs=blk,
        out_shape=jax.ShapeDtypeStruct((b, s, c), I32),
        scratch_shapes=[pltpu.VMEM((n2 * (n1 + FFT_PITCH_PAD), LANES), I32)],
        compiler_params=_cparams(("parallel", "parallel")),
        name="seq_fft",
    )(u, g, f2)


def _attn_kernel(sink_ref, q_ref, kp_ref, kc_ref, kn_ref, vp_ref, vc_ref, vn_ref, o_ref,
                 s_scr, p_scr, inv_scr, *, tq):
    i = pl.program_id(1)
    first = i == 0
    last = i == pl.num_programs(1) - 1
    nsub = tq // WINDOW
    nq = Q_PER_KV * WINDOW
    nk = 3 * WINDOW
    c = lax.broadcasted_iota(I32, (nk, nq), 0)
    r = lax.broadcasted_iota(I32, (nk, nq), 1) & (WINDOW - 1)
    band = ((c >= r) & (c < WINDOW)) | ((c >= WINDOW) & (c < 2 * WINDOW)) | ((c >= 2 * WINDOW) & (c - 2 * WINDOW <= r))
    pairs = [(h, sb) for h in range(N_KV_HEADS) for sb in range(nsub)]
    kcat, vt = [], []
    for h in range(N_KV_HEADS):
        hs = slice(h * HEAD_DIM, (h + 1) * HEAD_DIM)
        kcat.append(jnp.concatenate([kp_ref[:, hs], kc_ref[:, hs], kn_ref[:, hs]], axis=0))
        vcat = jnp.concatenate([vp_ref[:, hs], vc_ref[:, hs], vn_ref[:, hs]], axis=0)
        vt.append(vcat.astype(F32).T.astype(BF16))

    for n, (h, sb) in enumerate(pairs):
        q4 = jnp.concatenate(
            [q_ref[sb * WINDOW:(sb + 1) * WINDOW,
                   (h * Q_PER_KV + g) * HEAD_DIM:(h * Q_PER_KV + g + 1) * HEAD_DIM]
             for g in range(Q_PER_KV)], axis=0)
        kw = kcat[h][sb * WINDOW:(sb + 3) * WINDOW]
        s_scr[n] = lax.dot_general(kw, q4, (((1,), (1,)), ((), ())), preferred_element_type=F32)

    for n, (h, sb) in enumerate(pairs):
        sc = jnp.where(band, s_scr[n], NEG)
        if sb == 0:
            sc = jnp.where((c < WINDOW) & first, NEG, sc)
        if sb == nsub - 1:
            sc = jnp.where((c >= 2 * WINDOW) & last, NEG, sc)
        sink = jnp.concatenate(
            [jnp.full((1, WINDOW), sink_ref[h * Q_PER_KV + g], F32) for g in range(Q_PER_KV)], axis=1)
        m = jnp.maximum(jnp.max(sc, axis=0, keepdims=True), sink)
        p = jnp.exp(sc - m)
        denom = jnp.sum(p, axis=0, keepdims=True) + jnp.exp(sink - m)
        p_scr[n] = p.astype(BF16)
        inv_scr[pl.ds(n, 1), :] = 1.0 / denom

    for n, (h, sb) in enumerate(pairs):
        ot = jnp.dot(vt[h][:, sb * WINDOW:(sb + 3) * WINDOW], p_scr[n], preferred_element_type=F32)
        o = (ot * inv_scr[pl.ds(n, 1), :]).T
        for g in range(Q_PER_KV):
            col = (h * Q_PER_KV + g) * HEAD_DIM
            o_ref[sb * WINDOW:(sb + 1) * WINDOW, col:col + HEAD_DIM] = (
                o[g * WINDOW:(g + 1) * WINDOW].astype(BF16))


def _attention(qkv, sink):
    b, s, _ = qkv.shape
    aw = N_HEADS * HEAD_DIM
    kvw = N_KV_HEADS * HEAD_DIM
    tq = ATTN_TILE
    per = tq // WINDOW
    nblk = s // WINDOW
    kcol = aw // kvw
    vcol = kcol + 1
    halo = lambda col, off: pl.BlockSpec(
        (None, WINDOW, kvw),
        lambda bi, i, sk: (bi, jnp.clip(i * per + off, 0, nblk - 1), col))
    main = lambda col: pl.BlockSpec((None, tq, kvw), lambda bi, i, sk: (bi, i, col))
    return pl.pallas_call(
        functools.partial(_attn_kernel, tq=tq),
        grid_spec=pltpu.PrefetchScalarGridSpec(
            num_scalar_prefetch=1,
            grid=(b, s // tq),
            in_specs=[pl.BlockSpec((None, tq, aw), lambda bi, i, sk: (bi, i, 0)),
                      halo(kcol, -1), main(kcol), halo(kcol, per),
                      halo(vcol, -1), main(vcol), halo(vcol, per)],
            out_specs=pl.BlockSpec((None, tq, aw), lambda bi, i, sk: (bi, i, 0)),
            scratch_shapes=[pltpu.VMEM((N_KV_HEADS * per, 3 * WINDOW, Q_PER_KV * WINDOW), F32),
                            pltpu.VMEM((N_KV_HEADS * per, 3 * WINDOW, Q_PER_KV * WINDOW), BF16),
                            pltpu.VMEM((N_KV_HEADS * per, Q_PER_KV * WINDOW), F32)]),
        out_shape=jax.ShapeDtypeStruct((b, s, aw), BF16),
        compiler_params=_cparams(("parallel", "parallel")),
        name="attention",
    )(sink, qkv, qkv, qkv, qkv, qkv, qkv, qkv)


ROW_WORDS = 1024
ROW_SUB = ROW_WORDS // LANES


def _store_tile_rows(ref, val, base=0, lead=()):
    r = val.shape[0]
    for j in range(ROW_SUB):
        ref[lead + (pl.ds(base * ROW_SUB + j, r, stride=ROW_SUB), slice(None))] = val[:, j * LANES:(j + 1) * LANES]


def _load_tile_rows(ref, base, r, lead=()):
    return jnp.concatenate(
        [ref[lead + (pl.ds(base * ROW_SUB + j, r, stride=ROW_SUB), slice(None))] for j in range(ROW_SUB)], axis=1)


def _tile_row(ref, row):
    return ref.at[pl.ds(pl.multiple_of(row * ROW_SUB, ROW_SUB), ROW_SUB), :]


def _post_kernel(wp_ref, o_ref, gate_ref, x_ref, g1_ref, sh2_ref, sc2_ref, n2g_ref, cc_ref, cs_ref,
                 wfo_ref, wao_ref, wout_ref, wr_ref, br_ref, cnt_in_ref, h2_all_hbm,
                 x1_ref, h2_ref, route_ref, cnt_ref):
    del h2_all_hbm
    tm = x_ref.shape[0]
    d = x_ref.shape[1]

    @pl.when(pl.program_id(0) == 0)
    def _():
        cnt_ref[...] = cnt_in_ref[...]

    re, im = _unpack_pair(wp_ref[...])
    re, im = re.astype(BF16), im.astype(BF16)
    cc, cs = cc_ref[...], cs_ref[...]
    gd = FOURIER_GROUP_DIM
    fm = jnp.concatenate(
        [(jnp.dot(re[:, g * gd:(g + 1) * gd], cc, preferred_element_type=F32)
          + jnp.dot(im[:, g * gd:(g + 1) * gd], cs, preferred_element_type=F32)).astype(BF16)
         for g in range(FOURIER_GROUPS)], axis=1)
    y_f = jnp.dot(fm, wfo_ref[...], preferred_element_type=F32)
    y_a = jnp.dot(o_ref[...], wao_ref[...], preferred_element_type=F32)
    merged = (_sigmoid(gate_ref[:, :d].astype(F32)) * y_f
              + _sigmoid(gate_ref[:, d:].astype(F32)) * y_a).astype(BF16)
    x1 = x_ref[...] + g1_ref[...] * jnp.dot(merged, wout_ref[...], preferred_element_type=F32)
    x1_ref[...] = x1
    ms = jnp.mean(x1 * x1, axis=-1, keepdims=True)
    h2 = (x1 * lax.rsqrt(ms + EPS) * n2g_ref[...]) * (1.0 + sc2_ref[...]) + sh2_ref[...]
    half = d // 2
    _store_tile_rows(h2_ref, _pack_pair(h2[:, :half], h2[:, half:]))

    h_hi = h2.astype(BF16)
    h_lo = (h2 - h_hi.astype(F32)).astype(BF16)
    wr = wr_ref[...]
    r_hi = jnp.dot(h_hi, wr, preferred_element_type=F32)
    r_lo = jnp.dot(h_lo, wr, preferred_element_type=F32)
    logits = ((r_hi[:, :ROUTE_LANES] + r_hi[:, ROUTE_LANES:])
              + (r_lo[:, :ROUTE_LANES] + r_lo[:, ROUTE_LANES:])) + br_ref[...]
    lane = lax.broadcasted_iota(I32, (tm, ROUTE_LANES), 1)
    big = jnp.int32(ROUTE_LANES)
    is_g = lane < N_GROUPS
    gl = jnp.where(is_g, logits, NEG)
    gmax = jnp.max(gl, axis=-1, keepdims=True)
    gidx = jnp.min(jnp.where(gl == gmax, lane, big), axis=-1, keepdims=True)
    p_g = 1.0 / jnp.sum(jnp.where(is_g, jnp.exp(gl - gmax), 0.0), axis=-1, keepdims=True)
    in_grp = (lane >= N_GROUPS) & (lane < N_GROUPS + N_EXPERTS) & (
        lax.shift_right_logical(lane - N_GROUPS, jnp.full(lane.shape, 3, I32)) == gidx)
    el = jnp.where(in_grp, logits, NEG)
    e1v = jnp.max(el, axis=-1, keepdims=True)
    e1i = jnp.min(jnp.where(el == e1v, lane, big), axis=-1, keepdims=True)
    el2 = jnp.where(lane == e1i, NEG, el)
    e2v = jnp.max(el2, axis=-1, keepdims=True)
    e2i = jnp.min(jnp.where(el2 == e2v, lane, big), axis=-1, keepdims=True)
    t = jnp.exp(e2v - e1v)
    w1 = p_g / (1.0 + t)
    w2 = w1 * t
    sel1, sel2 = lane == e1i, lane == e2i
    member = jnp.where(sel1 | sel2, 1.0, 0.0)
    rr = lax.broadcasted_iota(I32, (tm, tm), 0)
    cc_i = lax.broadcasted_iota(I32, (tm, tm), 1)
    tri = jnp.where(cc_i < rr, 1.0, 0.0).astype(BF16)
    prefix = jnp.dot(tri, member.astype(BF16), preferred_element_type=F32) + cnt_ref[...]
    rank1 = jnp.sum(jnp.where(sel1, prefix, 0.0), axis=-1, keepdims=True)
    rank2 = jnp.sum(jnp.where(sel2, prefix, 0.0), axis=-1, keepdims=True)
    cnt_ref[...] = cnt_ref[...] + jnp.sum(member, axis=0, keepdims=True)
    key_scale = float(1 << RANK_BITS)
    key1 = (e1i - N_GROUPS).astype(F32) * key_scale + rank1
    key2 = (e2i - N_GROUPS).astype(F32) * key_scale + rank2
    route = jnp.zeros((tm, ROUTE_LANES), F32)
    for k, val in enumerate((key1, key2, w1, w2)):
        route = jnp.where(lane == k, val, route)
    route_ref[...] = route


def _post(wp, attn, gates, x, g1, sh2, sc2, norm2_g, cc, cs, wfo_b, wao_b, wout_b, wr, br,
          counts_in, h2_all, tok_off):
    b, s, d = x.shape
    t = b * s
    tm = ROW_TILE
    per_b_tiles = s // tm
    tile_off = tok_off // tm
    flat = lambda a: a.reshape(t, a.shape[-1])
    row = lambda w: pl.BlockSpec((tm, w), lambda i: (i, 0))
    per_b = pl.BlockSpec((None, 1, d), lambda i: (i // per_b_tiles, 0, 0))
    const = lambda a: _resident(a.shape, lambda i: (0,) * a.ndim)
    fw = wp.shape[-1]
    aw = attn.shape[-1]
    return pl.pallas_call(
        _post_kernel,
        grid=(t // tm,),
        in_specs=[row(fw), row(aw), row(2 * d), row(d), per_b, per_b, per_b,
                  const(norm2_g), const(cc), const(cs), const(wfo_b), const(wao_b), const(wout_b),
                  const(wr), const(br), const(counts_in), pl.BlockSpec(memory_space=pl.ANY)],
        out_specs=[row(d), pl.BlockSpec((tm * ROW_SUB, LANES), lambda i: (i + tile_off, 0)), row(ROUTE_LANES),
                   pl.BlockSpec((1, ROUTE_LANES), lambda i: (0, 0))],
        out_shape=[jax.ShapeDtypeStruct((t, d), F32), jax.ShapeDtypeStruct(h2_all.shape, I32),
                   jax.ShapeDtypeStruct((t, ROUTE_LANES), F32), jax.ShapeDtypeStruct((1, ROUTE_LANES), F32)],
        input_output_aliases={16: 1},
        compiler_params=_cparams(("arbitrary",)),
        name="post_router",
    )(flat(wp), flat(attn), flat(gates), flat(x), g1, sh2, sc2, norm2_g, cc, cs, wfo_b, wao_b, wout_b, wr, br,
      counts_in, h2_all)


def _sortrows_kernel(h_hbm, dest_hbm, zeros_hbm, xs_hbm, rowtok, dbuf, buf, sem, osem, zsem, *, rows):
    j = pl.program_id(0)
    n = pl.num_programs(0)
    slot = j % 2

    def start_gather(blk, sl):
        group = DMA_GROUP * TOP_K

        def body(gi, carry):
            r0 = gi * group
            toks = [rowtok[blk * rows + r0 + q] for q in range(group)]
            for q, tok in enumerate(toks):
                pltpu.make_async_copy(_tile_row(h_hbm, tok), _tile_row(buf.at[sl], r0 + q), sem.at[sl]).start()
            return carry

        lax.fori_loop(0, rows // group, body, 0)

    def out_copy(blk, sl):
        off = pl.multiple_of(blk * (rows * ROW_SUB), rows * ROW_SUB)
        return pltpu.make_async_copy(buf.at[sl], xs_hbm.at[pl.ds(off, rows * ROW_SUB), :], osem.at[sl])

    @pl.when(j == 0)
    def _():
        zero = pltpu.make_async_copy(zeros_hbm, rowtok, zsem)
        zero.start()
        zero.wait()
        unroll = 16
        chunk = dbuf.shape[0]

        def scatter_chunk(c, carry):
            cp = pltpu.make_async_copy(dest_hbm.at[pl.ds(pl.multiple_of(c * chunk, chunk), chunk)], dbuf, zsem)
            cp.start()
            cp.wait()

            def scatter(i, carry2):
                dests = [dbuf[i * unroll + u] for u in range(unroll)]
                tok0 = c * (chunk // TOP_K) + i * (unroll // TOP_K)
                for u, dst in enumerate(dests):
                    rowtok[dst] = tok0 + u // TOP_K
                return carry2

            return lax.fori_loop(j, j + chunk // unroll, scatter, carry)

        lax.fori_loop(j, j + dest_hbm.shape[0] // chunk, scatter_chunk, 0)
        start_gather(0, 0)

    @pl.when(j + 1 < n)
    def _():
        @pl.when(j >= 1)
        def _():
            out_copy(j - 1, 1 - slot).wait()

        start_gather(j + 1, 1 - slot)

    pltpu.make_async_copy(h_hbm.at[pl.ds(0, rows * ROW_SUB), :], buf.at[slot], sem.at[slot]).wait()
    out_copy(j, slot).start()

    @pl.when(j == n - 1)
    def _():
        @pl.when(j >= 1)
        def _():
            out_copy(j - 1, 1 - slot).wait()

        out_copy(j, slot).wait()


def _sortrows(h2_all, dest, n_rows):
    rows = SORT_ROWS
    hbm = pl.BlockSpec(memory_space=pl.ANY)
    return pl.pallas_call(
        functools.partial(_sortrows_kernel, rows=rows),
        grid=(n_rows // rows,),
        in_specs=[hbm, hbm, hbm],
        out_specs=hbm,
        scratch_shapes=[pltpu.SMEM((n_rows,), I32), pltpu.SMEM((math.gcd(DEST_CHUNK, dest.shape[0]),), I32),
                        pltpu.VMEM((2, rows * ROW_SUB, LANES), I32), pltpu.SemaphoreType.DMA((2,)),
                        pltpu.SemaphoreType.DMA((2,)), pltpu.SemaphoreType.DMA(())],
        out_shape=jax.ShapeDtypeStruct((n_rows * ROW_SUB, LANES), I32),
        compiler_params=_cparams(("arbitrary",), disable_bounds_checks=True, has_side_effects=True),
        name="sort_rows",
    )(h2_all, dest, jnp.zeros((n_rows,), I32))


def _expert_kernel(pos_ref, seq_ref, nused_ref, xs_ref, w1_hbm, w3_hbm, w2_hbm, y_ref,
                   w1f, w3f, w2f, wsem, w1b, w3b, w2b, *, rows):
    j = pl.program_id(0)
    weights = ((w1_hbm, w1f), (w3_hbm, w3f), (w2_hbm, w2f))

    def start_weights(k, buf):
        e = seq_ref[k]
        for src, dst in weights:
            n = src.shape[1] // WEIGHT_CHUNKS
            for c in range(WEIGHT_CHUNKS):
                pltpu.make_async_copy(src.at[e, pl.ds(c * n, n), :], dst.at[buf, pl.ds(c * n, n), :],
                                      wsem.at[buf]).start()

    def wait_weights(k, buf):
        e = seq_ref[k]
        for src, dst in weights:
            pltpu.make_async_copy(src.at[e], dst.at[buf], wsem.at[buf]).wait()

    @pl.when(j == 0)
    def _():
        start_weights(0, 0)

    pos = pos_ref[j]
    pos_prev = pos_ref[jnp.maximum(j - 1, 0)]

    @pl.when((j == 0) | (pos != pos_prev))
    def _():
        buf = pos % 2
        wait_weights(pos, buf)

        @pl.when(pos + 1 < nused_ref[1])
        def _():
            start_weights(pos + 1, 1 - buf)

        w1b[...] = w1f[buf].astype(BF16)
        w3b[...] = w3f[buf].astype(BF16)
        w2b[...] = w2f[buf].astype(BF16)

    @pl.when(j < nused_ref[0])
    def _():
        hi, lo = _unpack_pair(_load_tile_rows(xs_ref, 0, rows))
        xb = jnp.concatenate([hi.astype(BF16), lo.astype(BF16)], axis=1)
        a = jnp.dot(xb, w1b[...], preferred_element_type=F32)
        g = jnp.dot(xb, w3b[...], preferred_element_type=F32)
        hid = (a * _sigmoid(a) * g).astype(BF16)
        y = jnp.dot(hid, w2b[...], preferred_element_type=F32)
        half = y.shape[1] // 2
        _store_tile_rows(y_ref, _pack_pair(y[:, :half], y[:, half:]))

    @pl.when(j >= nused_ref[0])
    def _():
        y_ref[...] = jnp.zeros_like(y_ref)


def _experts(xs, block_pos, expert_seq, n_used, w1, w3, w2):
    rows = MOE_ROWS
    blk = rows * ROW_SUB
    d, f = w1.shape[1], w1.shape[2]
    hbm = pl.BlockSpec(memory_space=pl.ANY)
    return pl.pallas_call(
        functools.partial(_expert_kernel, rows=rows),
        grid_spec=pltpu.PrefetchScalarGridSpec(
            num_scalar_prefetch=3,
            grid=(xs.shape[0] // blk,),
            in_specs=[pl.BlockSpec((blk, LANES), lambda j, bp, es, nu: (jnp.minimum(j, nu[0] - 1), 0)),
                      hbm, hbm, hbm],
            out_specs=pl.BlockSpec((blk, LANES), lambda j, *_: (j, 0)),
            scratch_shapes=[pltpu.VMEM((2, d, f), F32), pltpu.VMEM((2, d, f), F32), pltpu.VMEM((2, f, d), F32),
                            pltpu.SemaphoreType.DMA((2,)),
                            pltpu.VMEM((d, f), BF16), pltpu.VMEM((d, f), BF16), pltpu.VMEM((f, d), BF16)]),
        out_shape=jax.ShapeDtypeStruct(xs.shape, I32),
        compiler_params=_cparams(("arbitrary",)),
        name="experts",
    )(block_pos, expert_seq, n_used, xs, w1, w3, w2)


def _combine_kernel(dest_ref, y_hbm, x1_ref, route_ref, g2_ref, o_ref, ybuf, sem, *, tm):
    i = pl.program_id(0)
    n = pl.num_programs(0)
    slot = i % 2

    def start_gather(tile, sl):
        def body(gi, carry):
            r0 = gi * DMA_GROUP
            base = (tile * tm + r0) * TOP_K
            dests = [dest_ref[base + q] for q in range(DMA_GROUP * TOP_K)]
            for q, dest in enumerate(dests):
                pltpu.make_async_copy(_tile_row(y_hbm, dest),
                                      _tile_row(ybuf.at[sl], (q % TOP_K) * tm + r0 + q // TOP_K),
                                      sem.at[sl]).start()
            return carry

        lax.fori_loop(0, tm // DMA_GROUP, body, 0)

    @pl.when(i == 0)
    def _():
        start_gather(0, 0)

    @pl.when(i + 1 < n)
    def _():
        start_gather(i + 1, 1 - slot)

    pltpu.make_async_copy(y_hbm.at[pl.ds(0, TOP_K * tm * ROW_SUB), :], ybuf.at[slot], sem.at[slot]).wait()
    route = route_ref[...]
    w1 = route[:, 2:3]
    w2 = route[:, 3:4]
    hi1, lo1 = _unpack_pair(_load_tile_rows(ybuf, 0, tm, lead=(slot,)))
    hi2, lo2 = _unpack_pair(_load_tile_rows(ybuf, tm, tm, lead=(slot,)))
    half = o_ref.shape[1] // 2
    o_ref[:, :half] = x1_ref[:, :half] + g2_ref[:, :half] * (w1 * hi1 + w2 * hi2)
    o_ref[:, half:] = x1_ref[:, half:] + g2_ref[:, half:] * (w1 * lo1 + w2 * lo2)


def _combine(y, dest, x1, route, g2, s):
    t, d = x1.shape
    tm = ROW_TILE
    per_b_tiles = s // tm
    return pl.pallas_call(
        functools.partial(_combine_kernel, tm=tm),
        grid_spec=pltpu.PrefetchScalarGridSpec(
            num_scalar_prefetch=1,
            grid=(t // tm,),
            in_specs=[pl.BlockSpec(memory_space=pl.ANY),
                      pl.BlockSpec((tm, d), lambda i, ds: (i, 0)),
                      pl.BlockSpec((tm, ROUTE_LANES), lambda i, ds: (i, 0)),
                      pl.BlockSpec((None, 1, d), lambda i, ds: (i // per_b_tiles, 0, 0))],
            out_specs=pl.BlockSpec((tm, d), lambda i, ds: (i, 0)),
            scratch_shapes=[pltpu.VMEM((2, TOP_K * tm * ROW_SUB, LANES), I32), pltpu.SemaphoreType.DMA((2,))]),
        out_shape=jax.ShapeDtypeStruct((t, d), F32),
        compiler_params=_cparams(("arbitrary",), disable_bounds_checks=True),
        name="combine",
    )(dest, y, x1, route, g2)


def _dispatch_plan(route_keys, counts, t):
    rows = MOE_ROWS
    keys = route_keys.astype(I32)
    expert = lax.shift_right_logical(keys, jnp.full(keys.shape, RANK_BITS, I32))
    rank = keys & ((1 << RANK_BITS) - 1)
    cnt = counts[0, N_GROUPS:N_GROUPS + N_EXPERTS].astype(I32)
    padded = (cnt + rows - 1) // rows * rows
    pad_end = jnp.cumsum(padded)
    pad_start = pad_end - padded
    onehot = expert[..., None] == jnp.arange(N_EXPERTS, dtype=I32)
    dest = (jnp.sum(jnp.where(onehot, pad_start, 0), axis=-1) + rank).reshape(t * TOP_K)
    per_sort = SORT_ROWS // rows
    n_blocks = (t * TOP_K + N_EXPERTS * (rows - 1) + SORT_ROWS - 1) // SORT_ROWS * per_sort
    n_used = pad_end[-1:] // rows
    blk = jnp.minimum(jnp.arange(n_blocks, dtype=I32), n_used - 1) * rows
    block_e = jnp.sum((pad_end[None, :] <= blk[:, None]).astype(I32), axis=1)
    owns = cnt > 0
    expert_seq = jnp.argsort(jnp.logical_not(owns), stable=True).astype(I32)
    seq_index = jnp.cumsum(owns.astype(I32)) - 1
    block_pos = jnp.sum(jnp.where(block_e[:, None] == jnp.arange(N_EXPERTS, dtype=I32), seq_index, 0), axis=1)
    used = jnp.concatenate([n_used, jnp.sum(owns.astype(I32), keepdims=True)]).astype(I32)
    return dest, block_pos.astype(I32), expert_seq, used, n_blocks * rows


def _layer(xs, mods, p):
    d = xs[0].shape[-1]
    t_all = sum(x.shape[0] * x.shape[1] for x in xs)
    h2_all = jnp.zeros((t_all * ROW_SUB, LANES), I32)
    counts = jnp.zeros((1, ROUTE_LANES), F32)
    per_group, tok_off = [], 0
    for x, mod in zip(xs, mods):
        b, s, _ = x.shape
        sh1, sc1, g1, sh2, sc2, g2 = [m.reshape(b, 1, d) for m in jnp.split(mod, 6, axis=-1)]
        u, qkv, gates = _inproj(x, sh1, sc1, p["norm1_g"], p["q_norm_g"], p["k_norm_g"], p["w_in"],
                                _rope_tables(s))
        wp = _seq_fft(u)
        attn = _attention(qkv, p["sink"])
        x1, h2_all, route, counts = _post(wp, attn, gates, x, g1, sh2, sc2, p["norm2_g"], p["cc"], p["cs"],
                                          p["w_fourier_out"], p["w_attn_out"], p["w_out"],
                                          p["w_router"], p["b_router"], counts, h2_all, tok_off)
        per_group.append((x1, route, g2, tok_off))
        tok_off += b * s
    route_keys = jnp.concatenate([r[:, 0:TOP_K] for _, r, _, _ in per_group], axis=0)
    dest, block_pos, expert_seq, n_used, n_rows = _dispatch_plan(route_keys, counts, t_all)
    xs_sorted = _sortrows(h2_all, dest, n_rows)
    y = _experts(xs_sorted, block_pos, expert_seq, n_used, p["w1"], p["w3"], p["w2"])
    outs = []
    for x, (x1, route, g2, off) in zip(xs, per_group):
        b, s, _ = x.shape
        dest_g = lax.slice(dest, (off * TOP_K,), ((off + b * s) * TOP_K,))
        outs.append(_combine(y, dest_g, x1, route, g2, s).reshape(b, s, d))
    return outs


def _channel_dft():
    n = FOURIER_GROUP_DIM
    k = jnp.arange(n, dtype=I32)
    ang = ((k[:, None] * k[None, :]) % n).astype(F32) * (2.0 * math.pi / n)
    return (jnp.cos(ang) * n ** -0.5).astype(BF16), (jnp.sin(ang) * n ** -0.5).astype(BF16)


def kernel(x_prompt, x_sample, c_prompt, c_sample, w_ada, b_ada, norm1_g, w_in, q_norm_g, k_norm_g, sink,
           w_fourier_out, w_attn_out, w_out, norm2_g, w_group, b_group, w_expert, b_expert, w1, w3, w2):
    depth = w_ada.shape[0]
    d = x_prompt.shape[-1]
    bp = c_prompt.shape[0]
    bs = c_sample.shape[0]
    cc, cs = _channel_dft()
    xp, xs = x_prompt, x_sample
    for l in range(depth):
        c_all = jnp.concatenate([c_prompt, c_sample, jnp.zeros((8 - (bp + bs) % 8, d), F32)], axis=0)
        mod = _adaln(c_all, w_ada[l], b_ada[l])
        pad = ROUTE_LANES - N_GROUPS - N_EXPERTS
        w_router = jnp.concatenate([w_group[l], w_expert[l], jnp.zeros((d, pad), F32)], axis=1)
        w_router_hi = w_router.astype(BF16)
        p = {
            "norm1_g": norm1_g[l].reshape(1, d), "norm2_g": norm2_g[l].reshape(1, d),
            "q_norm_g": q_norm_g[l].reshape(1, HEAD_DIM), "k_norm_g": k_norm_g[l].reshape(1, HEAD_DIM),
            "sink": sink[l], "w_in": w_in[l].astype(BF16),
            "w_fourier_out": w_fourier_out[l].astype(BF16), "w_attn_out": w_attn_out[l].astype(BF16),
            "w_out": w_out[l].astype(BF16), "cc": cc, "cs": cs,
            "w_router": jnp.concatenate(
                [w_router_hi, (w_router - w_router_hi.astype(F32)).astype(BF16)], axis=1),
            "b_router": jnp.concatenate([b_group[l], b_expert[l], jnp.zeros((pad,), F32)]).reshape(1, ROUTE_LANES),
            "w1": w1[l], "w3": w3[l], "w2": w2[l],
        }
        xp, xs = _layer([xp, xs], [mod[:bp], mod[bp:bp + bs]], p)
    return xp, xs
```

```python
import functools
import math

import jax
import jax.numpy as jnp
from jax import lax
from jax.experimental import pallas as pl
from jax.experimental.pallas import tpu as pltpu

F32 = jnp.float32
BF16 = jnp.bfloat16
I32 = jnp.int32

HEAD_DIM = 128
N_KV_HEADS = 2
Q_PER_KV = 4
N_HEADS = N_KV_HEADS * Q_PER_KV
ROT_DIM = 32
ROPE_THETA = 500000.0
WINDOW = 128
FOURIER_GROUPS = 4
FOURIER_GROUP_DIM = 256
N_GROUPS = 8
EXPERTS_PER_GROUP = 8
N_EXPERTS = N_GROUPS * EXPERTS_PER_GROUP
TOP_K = 2
EPS = 1e-6

LANES = 128
V7X_VMEM_LIMIT = 56 * 1024 * 1024

ROW_TILE = 256
ATTN_TILE = 512
FFT_N1 = 128
FFT_UNROLL = 16
FFT_PITCH_PAD = 8
MOE_ROWS = 256
ROUTE_LANES = 128
RANK_BITS = 17
DMA_GROUP = 8
WEIGHT_CHUNKS = 8
NEG = -1e30


def _cparams(sem, **kw):
    return pltpu.CompilerParams(dimension_semantics=sem, vmem_limit_bytes=V7X_VMEM_LIMIT, **kw)


def _resident(shape, index_map):
    return pl.BlockSpec(shape, index_map, pipeline_mode=pl.Buffered(1))


def _sigmoid(x):
    return 0.5 * jnp.tanh(0.5 * x) + 0.5


def _adaln_kernel(c_ref, w_ref, b_ref, o_ref):
    c = c_ref[...]
    s = c * _sigmoid(c)
    o_ref[...] = jnp.dot(s, w_ref[...], precision=lax.Precision.HIGHEST,
                         preferred_element_type=F32) + b_ref[...]


def _adaln(c, w_ada, b_ada):
    r, d = c.shape
    n = w_ada.shape[1]
    tn = 1024
    return pl.pallas_call(
        _adaln_kernel,
        grid=(n // tn,),
        in_specs=[pl.BlockSpec((r, d), lambda j: (0, 0)),
                  pl.BlockSpec((d, tn), lambda j: (0, j)),
                  pl.BlockSpec((1, tn), lambda j: (0, j))],
        out_specs=pl.BlockSpec((r, tn), lambda j: (0, j)),
        out_shape=jax.ShapeDtypeStruct((r, n), F32),
        compiler_params=_cparams(("parallel",)),
        name="adaln",
    )(c, w_ada, b_ada.reshape(1, n))


def _rope_tables(s):
    half = ROT_DIM // 2
    inv_freq = ROPE_THETA ** (-jnp.arange(0, ROT_DIM, 2, dtype=F32) / ROT_DIM)
    ang = jnp.arange(s, dtype=F32)[:, None] * inv_freq[None, :]
    cos, sin = jnp.cos(ang), jnp.sin(ang)
    pad = jnp.zeros((s, LANES - ROT_DIM), F32)
    cos_f = jnp.concatenate([cos, cos, jnp.ones((s, LANES - ROT_DIM), F32)], axis=1)
    sin_up = jnp.concatenate([-sin, jnp.zeros((s, half), F32), pad], axis=1)
    sin_dn = jnp.concatenate([jnp.zeros((s, half), F32), sin, pad], axis=1)
    return cos_f, sin_up, sin_dn


def _inproj_kernel(x_ref, sh_ref, sc_ref, g_ref, qg_ref, kg_ref, cos_ref, sup_ref, sdn_ref, w_ref,
                   u_ref, qkv_ref, gate_ref, *, fw, aw, kvw):
    x = x_ref[...]
    ms = jnp.mean(x * x, axis=-1, keepdims=True)
    h = (x * lax.rsqrt(ms + EPS) * g_ref[...]) * (1.0 + sc_ref[...]) + sh_ref[...]
    hb = h.astype(BF16)
    chunk = 512
    for c0 in range(0, fw, chunk):
        u_ref[:, c0:c0 + chunk] = jnp.dot(hb, w_ref[:, c0:c0 + chunk], preferred_element_type=F32)
    cos_f, sin_up, sin_dn = cos_ref[...], sup_ref[...], sdn_ref[...]
    half = ROT_DIM // 2
    scale = HEAD_DIM ** -0.5
    for c0 in list(range(0, aw, chunk)) + [aw]:
        width = chunk if c0 < aw else kvw
        acc = jnp.dot(hb, w_ref[:, fw + c0:fw + c0 + width], preferred_element_type=F32)
        for hh in range(width // HEAD_DIM):
            col = c0 + hh * HEAD_DIM
            t = acc[:, hh * HEAD_DIM:(hh + 1) * HEAD_DIM]
            is_q = col < aw
            gain = qg_ref[...] if is_q else kg_ref[...]
            t = t * lax.rsqrt(jnp.mean(t * t, axis=-1, keepdims=True) + EPS) * gain
            t = (t * cos_f + pltpu.roll(t, LANES - half, axis=1) * sin_up
                 + pltpu.roll(t, half, axis=1) * sin_dn)
            if is_q:
                t = t * scale
            qkv_ref[:, col:col + HEAD_DIM] = t.astype(BF16)
    v0 = fw + aw + kvw
    qkv_ref[:, aw + kvw:aw + 2 * kvw] = jnp.dot(
        hb, w_ref[:, v0:v0 + kvw], preferred_element_type=F32).astype(BF16)
    g0 = v0 + kvw
    gw = gate_ref.shape[-1]
    for c0 in range(0, gw, chunk):
        gate_ref[:, c0:c0 + chunk] = jnp.dot(
            hb, w_ref[:, g0 + c0:g0 + c0 + chunk], preferred_element_type=F32).astype(BF16)


def _inproj(x, shift, scale, norm_g, q_g, k_g, w_in_b, tables):
    b, s, d = x.shape
    fw = FOURIER_GROUPS * FOURIER_GROUP_DIM
    aw = N_HEADS * HEAD_DIM
    kvw = N_KV_HEADS * HEAD_DIM
    gw = 2 * d
    tm = ROW_TILE
    row = lambda w: pl.BlockSpec((None, tm, w), lambda bi, i: (bi, i, 0))
    per_b = pl.BlockSpec((None, 1, d), lambda bi, i: (bi, 0, 0))
    const = lambda w: pl.BlockSpec((1, w), lambda bi, i: (0, 0))
    tab = pl.BlockSpec((tm, LANES), lambda bi, i: (i, 0))
    return pl.pallas_call(
        functools.partial(_inproj_kernel, fw=fw, aw=aw, kvw=kvw),
        grid=(b, s // tm),
        in_specs=[row(d), per_b, per_b, const(d), const(HEAD_DIM), const(HEAD_DIM), tab, tab, tab,
                  _resident(w_in_b.shape, lambda bi, i: (0, 0))],
        out_specs=[row(fw), row(aw + 2 * kvw), row(gw)],
        out_shape=[jax.ShapeDtypeStruct((b, s, fw), F32),
                   jax.ShapeDtypeStruct((b, s, aw + 2 * kvw), BF16),
                   jax.ShapeDtypeStruct((b, s, gw), BF16)],
        compiler_params=_cparams(("parallel", "parallel")),
        name="inproj",
    )(x, shift, scale, norm_g, q_g, k_g, *tables, w_in_b)


def _pack_pair(a, b):
    ab = lax.bitcast_convert_type(a.astype(BF16).astype(F32), I32)
    bb = lax.bitcast_convert_type(b.astype(BF16).astype(F32), I32)
    return ab | lax.shift_right_logical(bb, jnp.full(bb.shape, 16, I32))


def _unpack_pair(p):
    hi = lax.bitcast_convert_type(p & jnp.int32(-65536), F32)
    lo = lax.bitcast_convert_type(lax.shift_left(p, jnp.full(p.shape, 16, I32)), F32)
    return hi, lo


def _fft_tables(s):
    n1 = FFT_N1
    n2 = s // n1
    k1 = jnp.arange(n1, dtype=I32)
    nn = (n2 * jnp.arange(n1, dtype=I32))[None, None, :] + jnp.arange(n2, dtype=I32)[:, None, None]
    ph = (k1[None, :, None] * nn) % s
    ang = ph.astype(F32) * (2.0 * math.pi / s)
    sc1 = n1 ** -0.5
    g = jnp.concatenate([jnp.cos(ang) * sc1, -jnp.sin(ang) * sc1], axis=1).astype(BF16)
    k2 = jnp.arange(n2, dtype=I32)
    ang2 = ((k2[:, None] * k2[None, :]) % n2).astype(F32) * (2.0 * math.pi / n2)
    c2, s2 = jnp.cos(ang2) * n2 ** -0.5, jnp.sin(ang2) * n2 ** -0.5
    f2 = jnp.concatenate([jnp.concatenate([c2, s2], axis=1),
                          jnp.concatenate([-s2, c2], axis=1)], axis=0).astype(BF16)
    return g, f2


def _fft_kernel(u_ref, g_ref, f2_ref, o_ref, y_scr, *, n1, n2, unroll):
    pitch = n1 + FFT_PITCH_PAD

    def stage1(i, carry):
        for uu in range(unroll):
            m = i * unroll + uu
            xm = u_ref[pl.ds(m, n1, stride=n2), :].astype(BF16)
            y = jnp.dot(g_ref[m], xm, preferred_element_type=F32)
            y_scr[pl.ds(pl.multiple_of(m * pitch, 8), n1), :] = _pack_pair(y[:n1], y[n1:])
        return carry

    lax.fori_loop(0, n2 // unroll, stage1, 0)
    f2 = f2_ref[...]

    def stage2(i, carry):
        for uu in range(unroll):
            k1 = i * unroll + uu
            yr, yi = _unpack_pair(y_scr[pl.ds(k1, n2, stride=pitch), :])
            rhs = jnp.concatenate([yr.astype(BF16), yi.astype(BF16)], axis=0)
            z = jnp.dot(f2, rhs, preferred_element_type=F32)
            y_scr[pl.ds(k1, n2, stride=pitch), :] = _pack_pair(z[:n2], z[n2:])
        return carry

    lax.fori_loop(0, n1 // unroll, stage2, 0)

    def compact(i, carry):
        for uu in range(unroll):
            k2 = i * unroll + uu
            o_ref[pl.ds(pl.multiple_of(k2 * n1, n1), n1), :] = y_scr[pl.ds(pl.multiple_of(k2 * pitch, 8), n1), :]
        return carry

    lax.fori_loop(0, n2 // unroll, compact, 0)


def _seq_fft(u):
    b, s, c = u.shape
    n1 = FFT_N1
    n2 = s // n1
    g, f2 = _fft_tables(s)
    unroll = min(FFT_UNROLL, n2)
    blk = pl.BlockSpec((None, s, LANES), lambda bi, j: (bi, 0, j))
    return pl.pallas_call(
        functools.partial(_fft_kernel, n1=n1, n2=n2, unroll=unroll),
        grid=(b, c // LANES),
        in_specs=[blk, _resident(g.shape, lambda bi, j: (0, 0, 0)), _resident(f2.shape, lambda bi, j: (0, 0))],
        out_specs=blk,
        out_shape=jax.ShapeDtypeStruct((b, s, c), I32),
        scratch_shapes=[pltpu.VMEM((n2 * (n1 + FFT_PITCH_PAD), LANES), I32)],
        compiler_params=_cparams(("parallel", "parallel")),
        name="seq_fft",
    )(u, g, f2)


def _attn_kernel(sink_ref, q_ref, kp_ref, kc_ref, kn_ref, vp_ref, vc_ref, vn_ref, o_ref,
                 s_scr, p_scr, inv_scr, *, tq):
    i = pl.program_id(1)
    first = i == 0
    last = i == pl.num_programs(1) - 1
    nsub = tq // WINDOW
    nq = Q_PER_KV * WINDOW
    nk = 3 * WINDOW
    c = lax.broadcasted_iota(I32, (nk, nq), 0)
    r = lax.broadcasted_iota(I32, (nk, nq), 1) & (WINDOW - 1)
    band = ((c >= r) & (c < WINDOW)) | ((c >= WINDOW) & (c < 2 * WINDOW)) | ((c >= 2 * WINDOW) & (c - 2 * WINDOW <= r))
    pairs = [(h, sb) for h in range(N_KV_HEADS) for sb in range(nsub)]
    kcat, vt = [], []
    for h in range(N_KV_HEADS):
        hs = slice(h * HEAD_DIM, (h + 1) * HEAD_DIM)
        kcat.append(jnp.concatenate([kp_ref[:, hs], kc_ref[:, hs], kn_ref[:, hs]], axis=0))
        vcat = jnp.concatenate([vp_ref[:, hs], vc_ref[:, hs], vn_ref[:, hs]], axis=0)
        vt.append(vcat.astype(F32).T.astype(BF16))

    for n, (h, sb) in enumerate(pairs):
        q4 = jnp.concatenate(
            [q_ref[sb * WINDOW:(sb + 1) * WINDOW,
                   (h * Q_PER_KV + g) * HEAD_DIM:(h * Q_PER_KV + g + 1) * HEAD_DIM]
             for g in range(Q_PER_KV)], axis=0)
        kw = kcat[h][sb * WINDOW:(sb + 3) * WINDOW]
        s_scr[n] = lax.dot_general(kw, q4, (((1,), (1,)), ((), ())), preferred_element_type=F32)

    for n, (h, sb) in enumerate(pairs):
        sc = jnp.where(band, s_scr[n], NEG)
        if sb == 0:
            sc = jnp.where((c < WINDOW) & first, NEG, sc)
        if sb == nsub - 1:
            sc = jnp.where((c >= 2 * WINDOW) & last, NEG, sc)
        sink = jnp.concatenate(
            [jnp.full((1, WINDOW), sink_ref[h * Q_PER_KV + g], F32) for g in range(Q_PER_KV)], axis=1)
        m = jnp.maximum(jnp.max(sc, axis=0, keepdims=True), sink)
        p = jnp.exp(sc - m)
        denom = jnp.sum(p, axis=0, keepdims=True) + jnp.exp(sink - m)
        p_scr[n] = p.astype(BF16)
        inv_scr[pl.ds(n, 1), :] = 1.0 / denom

    for n, (h, sb) in enumerate(pairs):
        ot = jnp.dot(vt[h][:, sb * WINDOW:(sb + 3) * WINDOW], p_scr[n], preferred_element_type=F32)
        o = (ot * inv_scr[pl.ds(n, 1), :]).T
        for g in range(Q_PER_KV):
            col = (h * Q_PER_KV + g) * HEAD_DIM
            o_ref[sb * WINDOW:(sb + 1) * WINDOW, col:col + HEAD_DIM] = (
                o[g * WINDOW:(g + 1) * WINDOW].astype(BF16))


def _attention(qkv, sink):
    b, s, _ = qkv.shape
    aw = N_HEADS * HEAD_DIM
    kvw = N_KV_HEADS * HEAD_DIM
    tq = ATTN_TILE
    per = tq // WINDOW
    nblk = s // WINDOW
    kcol = aw // kvw
    vcol = kcol + 1
    halo = lambda col, off: pl.BlockSpec(
        (None, WINDOW, kvw),
        lambda bi, i, sk: (bi, jnp.clip(i * per + off, 0, nblk - 1), col))
    main = lambda col: pl.BlockSpec((None, tq, kvw), lambda bi, i, sk: (bi, i, col))
    return pl.pallas_call(
        functools.partial(_attn_kernel, tq=tq),
        grid_spec=pltpu.PrefetchScalarGridSpec(
            num_scalar_prefetch=1,
            grid=(b, s // tq),
            in_specs=[pl.BlockSpec((None, tq, aw), lambda bi, i, sk: (bi, i, 0)),
                      halo(kcol, -1), main(kcol), halo(kcol, per),
                      halo(vcol, -1), main(vcol), halo(vcol, per)],
            out_specs=pl.BlockSpec((None, tq, aw), lambda bi, i, sk: (bi, i, 0)),
            scratch_shapes=[pltpu.VMEM((N_KV_HEADS * per, 3 * WINDOW, Q_PER_KV * WINDOW), F32),
                            pltpu.VMEM((N_KV_HEADS * per, 3 * WINDOW, Q_PER_KV * WINDOW), BF16),
                            pltpu.VMEM((N_KV_HEADS * per, Q_PER_KV * WINDOW), F32)]),
        out_shape=jax.ShapeDtypeStruct((b, s, aw), BF16),
        compiler_params=_cparams(("parallel", "parallel")),
        name="attention",
    )(sink, qkv, qkv, qkv, qkv, qkv, qkv, qkv)


ROW_WORDS = 1024
ROW_SUB = ROW_WORDS // LANES


def _store_tile_rows(ref, val, base=0, lead=()):
    r = val.shape[0]
    for j in range(ROW_SUB):
        ref[lead + (pl.ds(base * ROW_SUB + j, r, stride=ROW_SUB), slice(None))] = val[:, j * LANES:(j + 1) * LANES]


def _load_tile_rows(ref, base, r, lead=()):
    return jnp.concatenate(
        [ref[lead + (pl.ds(base * ROW_SUB + j, r, stride=ROW_SUB), slice(None))] for j in range(ROW_SUB)], axis=1)


def _tile_row(ref, row):
    return ref.at[pl.ds(pl.multiple_of(row * ROW_SUB, ROW_SUB), ROW_SUB), :]


def _post_kernel(wp_ref, o_ref, gate_ref, x_ref, g1_ref, sh2_ref, sc2_ref, n2g_ref, cc_ref, cs_ref,
                 wfo_ref, wao_ref, wout_ref, wr_ref, br_ref, cnt_in_ref, h2_all_hbm,
                 x1_ref, h2_ref, route_ref, cnt_ref):
    del h2_all_hbm
    tm = x_ref.shape[0]
    d = x_ref.shape[1]

    @pl.when(pl.program_id(0) == 0)
    def _():
        cnt_ref[...] = cnt_in_ref[...]

    re, im = _unpack_pair(wp_ref[...])
    re, im = re.astype(BF16), im.astype(BF16)
    cc, cs = cc_ref[...], cs_ref[...]
    gd = FOURIER_GROUP_DIM
    fm = jnp.concatenate(
        [(jnp.dot(re[:, g * gd:(g + 1) * gd], cc, preferred_element_type=F32)
          + jnp.dot(im[:, g * gd:(g + 1) * gd], cs, preferred_element_type=F32)).astype(BF16)
         for g in range(FOURIER_GROUPS)], axis=1)
    y_f = jnp.dot(fm, wfo_ref[...], preferred_element_type=F32)
    y_a = jnp.dot(o_ref[...], wao_ref[...], preferred_element_type=F32)
    merged = (_sigmoid(gate_ref[:, :d].astype(F32)) * y_f
              + _sigmoid(gate_ref[:, d:].astype(F32)) * y_a).astype(BF16)
    x1 = x_ref[...] + g1_ref[...] * jnp.dot(merged, wout_ref[...], preferred_element_type=F32)
    x1_ref[...] = x1
    ms = jnp.mean(x1 * x1, axis=-1, keepdims=True)
    h2 = (x1 * lax.rsqrt(ms + EPS) * n2g_ref[...]) * (1.0 + sc2_ref[...]) + sh2_ref[...]
    half = d // 2
    _store_tile_rows(h2_ref, _pack_pair(h2[:, :half], h2[:, half:]))

    h_hi = h2.astype(BF16)
    h_lo = (h2 - h_hi.astype(F32)).astype(BF16)
    wr = wr_ref[...]
    r_hi = jnp.dot(h_hi, wr, preferred_element_type=F32)
    r_lo = jnp.dot(h_lo, wr, preferred_element_type=F32)
    logits = ((r_hi[:, :ROUTE_LANES] + r_hi[:, ROUTE_LANES:])
              + (r_lo[:, :ROUTE_LANES] + r_lo[:, ROUTE_LANES:])) + br_ref[...]
    lane = lax.broadcasted_iota(I32, (tm, ROUTE_LANES), 1)
    big = jnp.int32(ROUTE_LANES)
    is_g = lane < N_GROUPS
    gl = jnp.where(is_g, logits, NEG)
    gmax = jnp.max(gl, axis=-1, keepdims=True)
    gidx = jnp.min(jnp.where(gl == gmax, lane, big), axis=-1, keepdims=True)
    p_g = 1.0 / jnp.sum(jnp.where(is_g, jnp.exp(gl - gmax), 0.0), axis=-1, keepdims=True)
    in_grp = (lane >= N_GROUPS) & (lane < N_GROUPS + N_EXPERTS) & (
        lax.shift_right_logical(lane - N_GROUPS, jnp.full(lane.shape, 3, I32)) == gidx)
    el = jnp.where(in_grp, logits, NEG)
    e1v = jnp.max(el, axis=-1, keepdims=True)
    e1i = jnp.min(jnp.where(el == e1v, lane, big), axis=-1, keepdims=True)
    el2 = jnp.where(lane == e1i, NEG, el)
    e2v = jnp.max(el2, axis=-1, keepdims=True)
    e2i = jnp.min(jnp.where(el2 == e2v, lane, big), axis=-1, keepdims=True)
    t = jnp.exp(e2v - e1v)
    w1 = p_g / (1.0 + t)
    w2 = w1 * t
    sel1, sel2 = lane == e1i, lane == e2i
    member = jnp.where(sel1 | sel2, 1.0, 0.0)
    rr = lax.broadcasted_iota(I32, (tm, tm), 0)
    cc_i = lax.broadcasted_iota(I32, (tm, tm), 1)
    tri = jnp.where(cc_i < rr, 1.0, 0.0).astype(BF16)
    prefix = jnp.dot(tri, member.astype(BF16), preferred_element_type=F32) + cnt_ref[...]
    rank1 = jnp.sum(jnp.where(sel1, prefix, 0.0), axis=-1, keepdims=True)
    rank2 = jnp.sum(jnp.where(sel2, prefix, 0.0), axis=-1, keepdims=True)
    cnt_ref[...] = cnt_ref[...] + jnp.sum(member, axis=0, keepdims=True)
    key_scale = float(1 << RANK_BITS)
    key1 = (e1i - N_GROUPS).astype(F32) * key_scale + rank1
    key2 = (e2i - N_GROUPS).astype(F32) * key_scale + rank2
    route = jnp.zeros((tm, ROUTE_LANES), F32)
    for k, val in enumerate((key1, key2, w1, w2)):
        route = jnp.where(lane == k, val, route)
    route_ref[...] = route


def _post(wp, attn, gates, x, g1, sh2, sc2, norm2_g, cc, cs, wfo_b, wao_b, wout_b, wr, br,
          counts_in, h2_all, tok_off):
    b, s, d = x.shape
    t = b * s
    tm = ROW_TILE
    per_b_tiles = s // tm
    tile_off = tok_off // tm
    flat = lambda a: a.reshape(t, a.shape[-1])
    row = lambda w: pl.BlockSpec((tm, w), lambda i: (i, 0))
    per_b = pl.BlockSpec((None, 1, d), lambda i: (i // per_b_tiles, 0, 0))
    const = lambda a: _resident(a.shape, lambda i: (0,) * a.ndim)
    fw = wp.shape[-1]
    aw = attn.shape[-1]
    return pl.pallas_call(
        _post_kernel,
        grid=(t // tm,),
        in_specs=[row(fw), row(aw), row(2 * d), row(d), per_b, per_b, per_b,
                  const(norm2_g), const(cc), const(cs), const(wfo_b), const(wao_b), const(wout_b),
                  const(wr), const(br), const(counts_in), pl.BlockSpec(memory_space=pl.ANY)],
        out_specs=[row(d), pl.BlockSpec((tm * ROW_SUB, LANES), lambda i: (i + tile_off, 0)), row(ROUTE_LANES),
                   pl.BlockSpec((1, ROUTE_LANES), lambda i: (0, 0))],
        out_shape=[jax.ShapeDtypeStruct((t, d), F32), jax.ShapeDtypeStruct(h2_all.shape, I32),
                   jax.ShapeDtypeStruct((t, ROUTE_LANES), F32), jax.ShapeDtypeStruct((1, ROUTE_LANES), F32)],
        input_output_aliases={16: 1},
        compiler_params=_cparams(("arbitrary",)),
        name="post_router",
    )(flat(wp), flat(attn), flat(gates), flat(x), g1, sh2, sc2, norm2_g, cc, cs, wfo_b, wao_b, wout_b, wr, br,
      counts_in, h2_all)


def _sortrows_kernel(dest_ref, pstart_ref, pend_ref, h_ref, xs_hbm, stage, zbuf, sem, zsem, *, tm, rows):
    j = pl.program_id(0)
    n = pl.num_programs(0)
    slot = j % 2
    blk_sub = rows * ROW_SUB

    def wait_slot(sl):
        nsub = TOP_K * tm * ROW_SUB
        pltpu.make_async_copy(stage.at[0, pl.ds(0, nsub), :], xs_hbm.at[pl.ds(0, nsub), :], sem.at[sl]).wait()

    @pl.when(j == 0)
    def _():
        zbuf[...] = jnp.zeros_like(zbuf)

        def zero_block(off):
            return pltpu.make_async_copy(zbuf, xs_hbm.at[pl.ds(pl.multiple_of(off * ROW_SUB, blk_sub), blk_sub), :],
                                         zsem)

        def per_expert(fn):
            def body(e, carry):
                @pl.when(pend_ref[e] > pstart_ref[e])
                def _():
                    fn(zero_block(pend_ref[e] - rows))
                return carry
            lax.fori_loop(0, N_EXPERTS, body, 0)

        n_used = pend_ref[N_EXPERTS - 1] // rows
        n_blocks = xs_hbm.shape[0] // blk_sub

        def tail(fn):
            lax.fori_loop(n_used, n_blocks, lambda b, c: (fn(zero_block(b * rows)), c)[1], 0)

        per_expert(lambda cp: cp.start())
        tail(lambda cp: cp.start())
        per_expert(lambda cp: cp.wait())
        tail(lambda cp: cp.wait())

    @pl.when(j >= 2)
    def _():
        wait_slot(slot)

    stage[slot, pl.ds(0, tm * ROW_SUB), :] = h_ref[...]

    def body(gi, carry):
        r0 = gi * DMA_GROUP
        base = (j * tm + r0) * TOP_K
        dests = [dest_ref[base + q] for q in range(DMA_GROUP * TOP_K)]
        for q, dest in enumerate(dests):
            pltpu.make_async_copy(_tile_row(stage.at[slot], r0 + q // TOP_K), _tile_row(xs_hbm, dest),
                                  sem.at[slot]).start()
        return carry

    lax.fori_loop(0, tm // DMA_GROUP, body, 0)

    @pl.when(j == n - 1)
    def _():
        @pl.when(j >= 1)
        def _():
            wait_slot(1 - slot)

        wait_slot(slot)


def _sortrows(h2_all, dest, pad_start, pad_end, n_rows):
    tm = ROW_TILE
    t = h2_all.shape[0] // ROW_SUB
    return pl.pallas_call(
        functools.partial(_sortrows_kernel, tm=tm, rows=MOE_ROWS),
        grid_spec=pltpu.PrefetchScalarGridSpec(
            num_scalar_prefetch=3,
            grid=(t // tm,),
            in_specs=[pl.BlockSpec((tm * ROW_SUB, LANES), lambda j, *_: (j, 0))],
            out_specs=pl.BlockSpec(memory_space=pl.ANY),
            scratch_shapes=[pltpu.VMEM((2, TOP_K * tm * ROW_SUB, LANES), I32),
                            pltpu.VMEM((MOE_ROWS * ROW_SUB, LANES), I32),
                            pltpu.SemaphoreType.DMA((2,)), pltpu.SemaphoreType.DMA(())]),
        out_shape=jax.ShapeDtypeStruct((n_rows * ROW_SUB, LANES), I32),
        compiler_params=_cparams(("arbitrary",), disable_bounds_checks=True, has_side_effects=True),
        name="sort_rows",
    )(dest, pad_start, pad_end, h2_all)


def _expert_kernel(pos_ref, seq_ref, nused_ref, xs_ref, w1_hbm, w3_hbm, w2_hbm, y_ref,
                   w1f, w3f, w2f, wsem, w1b, w3b, w2b, *, rows):
    j = pl.program_id(0)
    weights = ((w1_hbm, w1f), (w3_hbm, w3f), (w2_hbm, w2f))

    def start_weights(k, buf):
        e = seq_ref[k]
        for src, dst in weights:
            n = src.shape[1] // WEIGHT_CHUNKS
            for c in range(WEIGHT_CHUNKS):
                pltpu.make_async_copy(src.at[e, pl.ds(c * n, n), :], dst.at[buf, pl.ds(c * n, n), :],
                                      wsem.at[buf]).start()

    def wait_weights(k, buf):
        e = seq_ref[k]
        for src, dst in weights:
            pltpu.make_async_copy(src.at[e], dst.at[buf], wsem.at[buf]).wait()

    @pl.when(j == 0)
    def _():
        start_weights(0, 0)

    pos = pos_ref[j]
    pos_prev = pos_ref[jnp.maximum(j - 1, 0)]

    @pl.when((j == 0) | (pos != pos_prev))
    def _():
        buf = pos % 2
        wait_weights(pos, buf)

        @pl.when(pos + 1 < nused_ref[1])
        def _():
            start_weights(pos + 1, 1 - buf)

        w1b[...] = w1f[buf].astype(BF16)
        w3b[...] = w3f[buf].astype(BF16)
        w2b[...] = w2f[buf].astype(BF16)

    @pl.when(j < nused_ref[0])
    def _():
        hi, lo = _unpack_pair(_load_tile_rows(xs_ref, 0, rows))
        xb = jnp.concatenate([hi.astype(BF16), lo.astype(BF16)], axis=1)
        a = jnp.dot(xb, w1b[...], preferred_element_type=F32)
        g = jnp.dot(xb, w3b[...], preferred_element_type=F32)
        hid = (a * _sigmoid(a) * g).astype(BF16)
        y = jnp.dot(hid, w2b[...], preferred_element_type=F32)
        half = y.shape[1] // 2
        _store_tile_rows(y_ref, _pack_pair(y[:, :half], y[:, half:]))

    @pl.when(j >= nused_ref[0])
    def _():
        y_ref[...] = jnp.zeros_like(y_ref)


def _experts(xs, block_pos, expert_seq, n_used, w1, w3, w2):
    rows = MOE_ROWS
    blk = rows * ROW_SUB
    d, f = w1.shape[1], w1.shape[2]
    hbm = pl.BlockSpec(memory_space=pl.ANY)
    return pl.pallas_call(
        functools.partial(_expert_kernel, rows=rows),
        grid_spec=pltpu.PrefetchScalarGridSpec(
            num_scalar_prefetch=3,
            grid=(xs.shape[0] // blk,),
            in_specs=[pl.BlockSpec((blk, LANES), lambda j, bp, es, nu: (jnp.minimum(j, nu[0] - 1), 0)),
                      hbm, hbm, hbm],
            out_specs=pl.BlockSpec((blk, LANES), lambda j, *_: (j, 0)),
            scratch_shapes=[pltpu.VMEM((2, d, f), F32), pltpu.VMEM((2, d, f), F32), pltpu.VMEM((2, f, d), F32),
                            pltpu.SemaphoreType.DMA((2,)),
                            pltpu.VMEM((d, f), BF16), pltpu.VMEM((d, f), BF16), pltpu.VMEM((f, d), BF16)]),
        out_shape=jax.ShapeDtypeStruct(xs.shape, I32),
        compiler_params=_cparams(("arbitrary",)),
        name="experts",
    )(block_pos, expert_seq, n_used, xs, w1, w3, w2)


def _combine_kernel(dest_ref, y_hbm, x1_ref, route_ref, g2_ref, o_ref, ybuf, sem, *, tm):
    i = pl.program_id(0)
    n = pl.num_programs(0)
    slot = i % 2

    def start_gather(tile, sl):
        def body(gi, carry):
            r0 = gi * DMA_GROUP
            base = (tile * tm + r0) * TOP_K
            dests = [dest_ref[base + q] for q in range(DMA_GROUP * TOP_K)]
            for q, dest in enumerate(dests):
                pltpu.make_async_copy(_tile_row(y_hbm, dest),
                                      _tile_row(ybuf.at[sl], (q % TOP_K) * tm + r0 + q // TOP_K),
                                      sem.at[sl]).start()
            return carry

        lax.fori_loop(0, tm // DMA_GROUP, body, 0)

    @pl.when(i == 0)
    def _():
        start_gather(0, 0)

    @pl.when(i + 1 < n)
    def _():
        start_gather(i + 1, 1 - slot)

    pltpu.make_async_copy(y_hbm.at[pl.ds(0, TOP_K * tm * ROW_SUB), :], ybuf.at[slot], sem.at[slot]).wait()
    route = route_ref[...]
    w1 = route[:, 2:3]
    w2 = route[:, 3:4]
    hi1, lo1 = _unpack_pair(_load_tile_rows(ybuf, 0, tm, lead=(slot,)))
    hi2, lo2 = _unpack_pair(_load_tile_rows(ybuf, tm, tm, lead=(slot,)))
    half = o_ref.shape[1] // 2
    o_ref[:, :half] = x1_ref[:, :half] + g2_ref[:, :half] * (w1 * hi1 + w2 * hi2)
    o_ref[:, half:] = x1_ref[:, half:] + g2_ref[:, half:] * (w1 * lo1 + w2 * lo2)


def _combine(y, dest, x1, route, g2, s):
    t, d = x1.shape
    tm = ROW_TILE
    per_b_tiles = s // tm
    return pl.pallas_call(
        functools.partial(_combine_kernel, tm=tm),
        grid_spec=pltpu.PrefetchScalarGridSpec(
            num_scalar_prefetch=1,
            grid=(t // tm,),
            in_specs=[pl.BlockSpec(memory_space=pl.ANY),
                      pl.BlockSpec((tm, d), lambda i, ds: (i, 0)),
                      pl.BlockSpec((tm, ROUTE_LANES), lambda i, ds: (i, 0)),
                      pl.BlockSpec((None, 1, d), lambda i, ds: (i // per_b_tiles, 0, 0))],
            out_specs=pl.BlockSpec((tm, d), lambda i, ds: (i, 0)),
            scratch_shapes=[pltpu.VMEM((2, TOP_K * tm * ROW_SUB, LANES), I32), pltpu.SemaphoreType.DMA((2,))]),
        out_shape=jax.ShapeDtypeStruct((t, d), F32),
        compiler_params=_cparams(("arbitrary",), disable_bounds_checks=True),
        name="combine",
    )(dest, y, x1, route, g2)


def _dispatch_plan(route_keys, counts, t):
    rows = MOE_ROWS
    keys = route_keys.astype(I32)
    expert = lax.shift_right_logical(keys, jnp.full(keys.shape, RANK_BITS, I32))
    rank = keys & ((1 << RANK_BITS) - 1)
    cnt = counts[0, N_GROUPS:N_GROUPS + N_EXPERTS].astype(I32)
    padded = (cnt + rows - 1) // rows * rows
    pad_end = jnp.cumsum(padded)
    pad_start = pad_end - padded
    onehot = expert[..., None] == jnp.arange(N_EXPERTS, dtype=I32)
    dest = (jnp.sum(jnp.where(onehot, pad_start, 0), axis=-1) + rank).reshape(t * TOP_K)
    n_blocks = (t * TOP_K + N_EXPERTS * (rows - 1) + rows - 1) // rows
    n_used = pad_end[-1:] // rows
    blk = jnp.minimum(jnp.arange(n_blocks, dtype=I32), n_used - 1) * rows
    block_e = jnp.sum((pad_end[None, :] <= blk[:, None]).astype(I32), axis=1)
    owns = cnt > 0
    expert_seq = jnp.argsort(jnp.logical_not(owns), stable=True).astype(I32)
    seq_index = jnp.cumsum(owns.astype(I32)) - 1
    block_pos = jnp.sum(jnp.where(block_e[:, None] == jnp.arange(N_EXPERTS, dtype=I32), seq_index, 0), axis=1)
    used = jnp.concatenate([n_used, jnp.sum(owns.astype(I32), keepdims=True)]).astype(I32)
    return dest, pad_start, pad_end, block_pos.astype(I32), expert_seq, used, n_blocks * rows


def _layer(xs, mods, p):
    d = xs[0].shape[-1]
    t_all = sum(x.shape[0] * x.shape[1] for x in xs)
    h2_all = jnp.zeros((t_all * ROW_SUB, LANES), I32)
    counts = jnp.zeros((1, ROUTE_LANES), F32)
    per_group, tok_off = [], 0
    for x, mod in zip(xs, mods):
        b, s, _ = x.shape
        sh1, sc1, g1, sh2, sc2, g2 = [m.reshape(b, 1, d) for m in jnp.split(mod, 6, axis=-1)]
        u, qkv, gates = _inproj(x, sh1, sc1, p["norm1_g"], p["q_norm_g"], p["k_norm_g"], p["w_in"],
                                _rope_tables(s))
        wp = _seq_fft(u)
        attn = _attention(qkv, p["sink"])
        x1, h2_all, route, counts = _post(wp, attn, gates, x, g1, sh2, sc2, p["norm2_g"], p["cc"], p["cs"],
                                          p["w_fourier_out"], p["w_attn_out"], p["w_out"],
                                          p["w_router"], p["b_router"], counts, h2_all, tok_off)
        per_group.append((x1, route, g2, tok_off))
        tok_off += b * s
    route_keys = jnp.concatenate([r[:, 0:TOP_K] for _, r, _, _ in per_group], axis=0)
    dest, pad_start, pad_end, block_pos, expert_seq, n_used, n_rows = _dispatch_plan(route_keys, counts, t_all)
    xs_sorted = _sortrows(h2_all, dest, pad_start, pad_end, n_rows)
    y = _experts(xs_sorted, block_pos, expert_seq, n_used, p["w1"], p["w3"], p["w2"])
    outs = []
    for x, (x1, route, g2, off) in zip(xs, per_group):
        b, s, _ = x.shape
        dest_g = lax.slice(dest, (off * TOP_K,), ((off + b * s) * TOP_K,))
        outs.append(_combine(y, dest_g, x1, route, g2, s).reshape(b, s, d))
    return outs


def _channel_dft():
    n = FOURIER_GROUP_DIM
    k = jnp.arange(n, dtype=I32)
    ang = ((k[:, None] * k[None, :]) % n).astype(F32) * (2.0 * math.pi / n)
    return (jnp.cos(ang) * n ** -0.5).astype(BF16), (jnp.sin(ang) * n ** -0.5).astype(BF16)


def kernel(x_prompt, x_sample, c_prompt, c_sample, w_ada, b_ada, norm1_g, w_in, q_norm_g, k_norm_g, sink,
           w_fourier_out, w_attn_out, w_out, norm2_g, w_group, b_group, w_expert, b_expert, w1, w3, w2):
    depth = w_ada.shape[0]
    d = x_prompt.shape[-1]
    bp = c_prompt.shape[0]
    bs = c_sample.shape[0]
    cc, cs = _channel_dft()
    xp, xs = x_prompt, x_sample
    for l in range(depth):
        c_all = jnp.concatenate([c_prompt, c_sample, jnp.zeros((8 - (bp + bs) % 8, d), F32)], axis=0)
        mod = _adaln(c_all, w_ada[l], b_ada[l])
        pad = ROUTE_LANES - N_GROUPS - N_EXPERTS
        w_router = jnp.concatenate([w_group[l], w_expert[l], jnp.zeros((d, pad), F32)], axis=1)
        w_router_hi = w_router.astype(BF16)
        p = {
            "norm1_g": norm1_g[l].reshape(1, d), "norm2_g": norm2_g[l].reshape(1, d),
            "q_norm_g": q_norm_g[l].reshape(1, HEAD_DIM), "k_norm_g": k_norm_g[l].reshape(1, HEAD_DIM),
            "sink": sink[l], "w_in": w_in[l].astype(BF16),
            "w_fourier_out": w_fourier_out[l].astype(BF16), "w_attn_out": w_attn_out[l].astype(BF16),
            "w_out": w_out[l].astype(BF16), "cc": cc, "cs": cs,
            "w_router": jnp.concatenate(
                [w_router_hi, (w_router - w_router_hi.astype(F32)).astype(BF16)], axis=1),
            "b_router": jnp.concatenate([b_group[l], b_expert[l], jnp.zeros((pad,), F32)]).reshape(1, ROUTE_LANES),
            "w1": w1[l], "w3": w3[l], "w2": w2[l],
        }
        xp, xs = _layer([xp, xs], [mod[:bp], mod[bp:bp + bs]], p)
    return xp, xs
```

```python
import functools
import math

import jax
import jax.numpy as jnp
import numpy as np
from jax import lax
from jax.experimental import pallas as pl
from jax.experimental.pallas import tpu as pltpu

F32 = jnp.float32
BF16 = jnp.bfloat16
I32 = jnp.int32

HEAD_DIM = 128
N_KV_HEADS = 2
Q_PER_KV = 4
N_HEADS = N_KV_HEADS * Q_PER_KV
ROT_DIM = 32
ROPE_THETA = 500000.0
WINDOW = 128
FOURIER_GROUPS = 4
FOURIER_GROUP_DIM = 256
N_GROUPS = 8
EXPERTS_PER_GROUP = 8
N_EXPERTS = N_GROUPS * EXPERTS_PER_GROUP
TOP_K = 2
EPS = 1e-6

LANES = 128
V7X_VMEM_LIMIT = 56 * 1024 * 1024

ROW_TILE = 256
ATTN_TILE = 512
FFT_N1 = 128
FFT_UNROLL = 16
FFT_PITCH_PAD = 8
MOE_ROWS = 256
ROUTE_LANES = 128
RANK_BITS = 17
DMA_GROUP = 8
WEIGHT_CHUNKS = 8
NEG = -1e30


def _cparams(sem, **kw):
    return pltpu.CompilerParams(dimension_semantics=sem, vmem_limit_bytes=V7X_VMEM_LIMIT, **kw)


def _resident(shape, index_map):
    return pl.BlockSpec(shape, index_map, pipeline_mode=pl.Buffered(1))


def _sigmoid(x):
    return 0.5 * jnp.tanh(0.5 * x) + 0.5


def _adaln_kernel(c_ref, w_ref, b_ref, o_ref):
    c = c_ref[...]
    s = c * _sigmoid(c)
    o_ref[...] = jnp.dot(s, w_ref[...], precision=lax.Precision.HIGHEST,
                         preferred_element_type=F32) + b_ref[...]


def _adaln(c, w_ada, b_ada):
    r, d = c.shape
    n = w_ada.shape[1]
    tn = 1024
    return pl.pallas_call(
        _adaln_kernel,
        grid=(n // tn,),
        in_specs=[pl.BlockSpec((r, d), lambda j: (0, 0)),
                  pl.BlockSpec((d, tn), lambda j: (0, j)),
                  pl.BlockSpec((1, tn), lambda j: (0, j))],
        out_specs=pl.BlockSpec((r, tn), lambda j: (0, j)),
        out_shape=jax.ShapeDtypeStruct((r, n), F32),
        compiler_params=_cparams(("parallel",)),
        name="adaln",
    )(c, w_ada, b_ada.reshape(1, n))


def _rope_tables(s):
    half = ROT_DIM // 2
    inv_freq = ROPE_THETA ** (-jnp.arange(0, ROT_DIM, 2, dtype=F32) / ROT_DIM)
    ang = jnp.arange(s, dtype=F32)[:, None] * inv_freq[None, :]
    cos, sin = jnp.cos(ang), jnp.sin(ang)
    pad = jnp.zeros((s, LANES - ROT_DIM), F32)
    cos_f = jnp.concatenate([cos, cos, jnp.ones((s, LANES - ROT_DIM), F32)], axis=1)
    sin_up = jnp.concatenate([-sin, jnp.zeros((s, half), F32), pad], axis=1)
    sin_dn = jnp.concatenate([jnp.zeros((s, half), F32), sin, pad], axis=1)
    return cos_f, sin_up, sin_dn


def _inproj_kernel(x_ref, sh_ref, sc_ref, g_ref, qg_ref, kg_ref, cos_ref, sup_ref, sdn_ref, w_ref,
                   u_ref, qkv_ref, gate_ref, *, fw, aw, kvw):
    x = x_ref[...]
    ms = jnp.mean(x * x, axis=-1, keepdims=True)
    h = (x * lax.rsqrt(ms + EPS) * g_ref[...]) * (1.0 + sc_ref[...]) + sh_ref[...]
    hb = h.astype(BF16)
    chunk = 512
    for c0 in range(0, fw, chunk):
        u_ref[:, c0:c0 + chunk] = jnp.dot(hb, w_ref[:, c0:c0 + chunk], preferred_element_type=F32)
    cos_f, sin_up, sin_dn = cos_ref[...], sup_ref[...], sdn_ref[...]
    half = ROT_DIM // 2
    scale = HEAD_DIM ** -0.5
    for c0 in list(range(0, aw, chunk)) + [aw]:
        width = chunk if c0 < aw else kvw
        acc = jnp.dot(hb, w_ref[:, fw + c0:fw + c0 + width], preferred_element_type=F32)
        for hh in range(width // HEAD_DIM):
            col = c0 + hh * HEAD_DIM
            t = acc[:, hh * HEAD_DIM:(hh + 1) * HEAD_DIM]
            is_q = col < aw
            gain = qg_ref[...] if is_q else kg_ref[...]
            t = t * lax.rsqrt(jnp.mean(t * t, axis=-1, keepdims=True) + EPS) * gain
            t = (t * cos_f + pltpu.roll(t, LANES - half, axis=1) * sin_up
                 + pltpu.roll(t, half, axis=1) * sin_dn)
            if is_q:
                t = t * scale
            qkv_ref[:, col:col + HEAD_DIM] = t.astype(BF16)
    v0 = fw + aw + kvw
    qkv_ref[:, aw + kvw:aw + 2 * kvw] = jnp.dot(
        hb, w_ref[:, v0:v0 + kvw], preferred_element_type=F32).astype(BF16)
    g0 = v0 + kvw
    gw = gate_ref.shape[-1]
    for c0 in range(0, gw, chunk):
        gate_ref[:, c0:c0 + chunk] = jnp.dot(
            hb, w_ref[:, g0 + c0:g0 + c0 + chunk], preferred_element_type=F32).astype(BF16)


def _inproj(x, shift, scale, norm_g, q_g, k_g, w_in_b, tables):
    b, s, d = x.shape
    fw = FOURIER_GROUPS * FOURIER_GROUP_DIM
    aw = N_HEADS * HEAD_DIM
    kvw = N_KV_HEADS * HEAD_DIM
    gw = 2 * d
    tm = ROW_TILE
    row = lambda w: pl.BlockSpec((None, tm, w), lambda bi, i: (bi, i, 0))
    per_b = pl.BlockSpec((None, 1, d), lambda bi, i: (bi, 0, 0))
    const = lambda w: pl.BlockSpec((1, w), lambda bi, i: (0, 0))
    tab = pl.BlockSpec((tm, LANES), lambda bi, i: (i, 0))
    return pl.pallas_call(
        functools.partial(_inproj_kernel, fw=fw, aw=aw, kvw=kvw),
        grid=(b, s // tm),
        in_specs=[row(d), per_b, per_b, const(d), const(HEAD_DIM), const(HEAD_DIM), tab, tab, tab,
                  _resident(w_in_b.shape, lambda bi, i: (0, 0))],
        out_specs=[row(fw), row(aw + 2 * kvw), row(gw)],
        out_shape=[jax.ShapeDtypeStruct((b, s, fw), F32),
                   jax.ShapeDtypeStruct((b, s, aw + 2 * kvw), BF16),
                   jax.ShapeDtypeStruct((b, s, gw), BF16)],
        compiler_params=_cparams(("parallel", "parallel")),
        name="inproj",
    )(x, shift, scale, norm_g, q_g, k_g, *tables, w_in_b)


def _pack_pair(a, b):
    ab = lax.bitcast_convert_type(a.astype(BF16).astype(F32), I32)
    bb = lax.bitcast_convert_type(b.astype(BF16).astype(F32), I32)
    return ab | lax.shift_right_logical(bb, jnp.full(bb.shape, 16, I32))


def _unpack_pair(p):
    hi = lax.bitcast_convert_type(p & jnp.int32(-65536), F32)
    lo = lax.bitcast_convert_type(lax.shift_left(p, jnp.full(p.shape, 16, I32)), F32)
    return hi, lo


def _fft_tables(s):
    n1 = FFT_N1
    n2 = s // n1
    k1 = np.arange(n1, dtype=np.int64)
    nn = (n2 * np.arange(n1, dtype=np.int64))[None, None, :] + np.arange(n2, dtype=np.int64)[:, None, None]
    ang = ((k1[None, :, None] * nn) % s) * (2.0 * math.pi / s)
    sc1 = n1 ** -0.5
    g = np.concatenate([np.cos(ang) * sc1, -np.sin(ang) * sc1], axis=1).astype(BF16)
    k2 = np.arange(n2, dtype=np.int64)
    ang2 = ((k2[:, None] * k2[None, :]) % n2) * (2.0 * math.pi / n2)
    c2, s2 = np.cos(ang2) * n2 ** -0.5, np.sin(ang2) * n2 ** -0.5
    f2 = np.concatenate([np.concatenate([c2, s2], axis=1),
                         np.concatenate([-s2, c2], axis=1)], axis=0).astype(BF16)
    return jnp.asarray(g), jnp.asarray(f2)


def _fft_kernel(u_ref, g_ref, f2_ref, o_ref, y_scr, *, n1, n2, unroll):
    pitch = n1 + FFT_PITCH_PAD

    def stage1(i, carry):
        for uu in range(unroll):
            m = i * unroll + uu
            xm = u_ref[pl.ds(m, n1, stride=n2), :].astype(BF16)
            y = jnp.dot(g_ref[m], xm, preferred_element_type=F32)
            y_scr[pl.ds(pl.multiple_of(m * pitch, 8), n1), :] = _pack_pair(y[:n1], y[n1:])
        return carry

    lax.fori_loop(0, n2 // unroll, stage1, 0)
    f2 = f2_ref[...]

    def stage2(i, carry):
        for uu in range(unroll):
            k1 = i * unroll + uu
            yr, yi = _unpack_pair(y_scr[pl.ds(k1, n2, stride=pitch), :])
            rhs = jnp.concatenate([yr.astype(BF16), yi.astype(BF16)], axis=0)
            z = jnp.dot(f2, rhs, preferred_element_type=F32)
            y_scr[pl.ds(k1, n2, stride=pitch), :] = _pack_pair(z[:n2], z[n2:])
        return carry

    lax.fori_loop(0, n1 // unroll, stage2, 0)

    def compact(i, carry):
        for uu in range(unroll):
            k2 = i * unroll + uu
            o_ref[pl.ds(pl.multiple_of(k2 * n1, n1), n1), :] = y_scr[pl.ds(pl.multiple_of(k2 * pitch, 8), n1), :]
        return carry

    lax.fori_loop(0, n2 // unroll, compact, 0)


def _seq_fft(u):
    b, s, c = u.shape
    n1 = FFT_N1
    n2 = s // n1
    g, f2 = _fft_tables(s)
    unroll = min(FFT_UNROLL, n2)
    blk = pl.BlockSpec((None, s, LANES), lambda bi, j: (bi, 0, j))
    return pl.pallas_call(
        functools.partial(_fft_kernel, n1=n1, n2=n2, unroll=unroll),
        grid=(b, c // LANES),
        in_specs=[blk, _resident(g.shape, lambda bi, j: (0, 0, 0)), _resident(f2.shape, lambda bi, j: (0, 0))],
        out_specs=blk,
        out_shape=jax.ShapeDtypeStruct((b, s, c), I32),
        scratch_shapes=[pltpu.VMEM((n2 * (n1 + FFT_PITCH_PAD), LANES), I32)],
        compiler_params=_cparams(("parallel", "parallel")),
        name="seq_fft",
    )(u, g, f2)


def _attn_kernel(sink_ref, q_ref, kp_ref, kc_ref, kn_ref, vp_ref, vc_ref, vn_ref, o_ref,
                 s_scr, p_scr, inv_scr, *, tq):
    i = pl.program_id(1)
    first = i == 0
    last = i == pl.num_programs(1) - 1
    nsub = tq // WINDOW
    nq = Q_PER_KV * WINDOW
    nk = 3 * WINDOW
    c = lax.broadcasted_iota(I32, (nk, nq), 0)
    r = lax.broadcasted_iota(I32, (nk, nq), 1) & (WINDOW - 1)
    band = ((c >= r) & (c < WINDOW)) | ((c >= WINDOW) & (c < 2 * WINDOW)) | ((c >= 2 * WINDOW) & (c - 2 * WINDOW <= r))
    pairs = [(h, sb) for h in range(N_KV_HEADS) for sb in range(nsub)]
    kcat, vt = [], []
    for h in range(N_KV_HEADS):
        hs = slice(h * HEAD_DIM, (h + 1) * HEAD_DIM)
        kcat.append(jnp.concatenate([kp_ref[:, hs], kc_ref[:, hs], kn_ref[:, hs]], axis=0))
        vcat = jnp.concatenate([vp_ref[:, hs], vc_ref[:, hs], vn_ref[:, hs]], axis=0)
        vt.append(vcat.astype(F32).T.astype(BF16))

    for n, (h, sb) in enumerate(pairs):
        q4 = jnp.concatenate(
            [q_ref[sb * WINDOW:(sb + 1) * WINDOW,
                   (h * Q_PER_KV + g) * HEAD_DIM:(h * Q_PER_KV + g + 1) * HEAD_DIM]
             for g in range(Q_PER_KV)], axis=0)
        kw = kcat[h][sb * WINDOW:(sb + 3) * WINDOW]
        s_scr[n] = lax.dot_general(kw, q4, (((1,), (1,)), ((), ())), preferred_element_type=F32)

    for n, (h, sb) in enumerate(pairs):
        sc = jnp.where(band, s_scr[n], NEG)
        if sb == 0:
            sc = jnp.where((c < WINDOW) & first, NEG, sc)
        if sb == nsub - 1:
            sc = jnp.where((c >= 2 * WINDOW) & last, NEG, sc)
        sink = jnp.concatenate(
            [jnp.full((1, WINDOW), sink_ref[h * Q_PER_KV + g], F32) for g in range(Q_PER_KV)], axis=1)
        m = jnp.maximum(jnp.max(sc, axis=0, keepdims=True), sink)
        p = jnp.exp(sc - m)
        denom = jnp.sum(p, axis=0, keepdims=True) + jnp.exp(sink - m)
        p_scr[n] = p.astype(BF16)
        inv_scr[pl.ds(n, 1), :] = 1.0 / denom

    for n, (h, sb) in enumerate(pairs):
        ot = jnp.dot(vt[h][:, sb * WINDOW:(sb + 3) * WINDOW], p_scr[n], preferred_element_type=F32)
        o = (ot * inv_scr[pl.ds(n, 1), :]).T
        for g in range(Q_PER_KV):
            col = (h * Q_PER_KV + g) * HEAD_DIM
            o_ref[sb * WINDOW:(sb + 1) * WINDOW, col:col + HEAD_DIM] = (
                o[g * WINDOW:(g + 1) * WINDOW].astype(BF16))


def _attention(qkv, sink):
    b, s, _ = qkv.shape
    aw = N_HEADS * HEAD_DIM
    kvw = N_KV_HEADS * HEAD_DIM
    tq = ATTN_TILE
    per = tq // WINDOW
    nblk = s // WINDOW
    kcol = aw // kvw
    vcol = kcol + 1
    halo = lambda col, off: pl.BlockSpec(
        (None, WINDOW, kvw),
        lambda bi, i, sk: (bi, jnp.clip(i * per + off, 0, nblk - 1), col))
    main = lambda col: pl.BlockSpec((None, tq, kvw), lambda bi, i, sk: (bi, i, col))
    return pl.pallas_call(
        functools.partial(_attn_kernel, tq=tq),
        grid_spec=pltpu.PrefetchScalarGridSpec(
            num_scalar_prefetch=1,
            grid=(b, s // tq),
            in_specs=[pl.BlockSpec((None, tq, aw), lambda bi, i, sk: (bi, i, 0)),
                      halo(kcol, -1), main(kcol), halo(kcol, per),
                      halo(vcol, -1), main(vcol), halo(vcol, per)],
            out_specs=pl.BlockSpec((None, tq, aw), lambda bi, i, sk: (bi, i, 0)),
            scratch_shapes=[pltpu.VMEM((N_KV_HEADS * per, 3 * WINDOW, Q_PER_KV * WINDOW), F32),
                            pltpu.VMEM((N_KV_HEADS * per, 3 * WINDOW, Q_PER_KV * WINDOW), BF16),
                            pltpu.VMEM((N_KV_HEADS * per, Q_PER_KV * WINDOW), F32)]),
        out_shape=jax.ShapeDtypeStruct((b, s, aw), BF16),
        compiler_params=_cparams(("parallel", "parallel")),
        name="attention",
    )(sink, qkv, qkv, qkv, qkv, qkv, qkv, qkv)


ROW_WORDS = 1024
ROW_SUB = ROW_WORDS // LANES


def _store_tile_rows(ref, val, base=0, lead=()):
    r = val.shape[0]
    for j in range(ROW_SUB):
        ref[lead + (pl.ds(base * ROW_SUB + j, r, stride=ROW_SUB), slice(None))] = val[:, j * LANES:(j + 1) * LANES]


def _load_tile_rows(ref, base, r, lead=()):
    return jnp.concatenate(
        [ref[lead + (pl.ds(base * ROW_SUB + j, r, stride=ROW_SUB), slice(None))] for j in range(ROW_SUB)], axis=1)


def _tile_row(ref, row):
    return ref.at[pl.ds(pl.multiple_of(row * ROW_SUB, ROW_SUB), ROW_SUB), :]


def _post_kernel(wp_ref, o_ref, gate_ref, x_ref, g1_ref, sh2_ref, sc2_ref, n2g_ref, cc_ref, cs_ref,
                 wfo_ref, wao_ref, wout_ref, wr_ref, br_ref, cnt_in_ref,
                 x1_ref, h2_ref, route_ref, cnt_ref):
    tm = x_ref.shape[0]
    d = x_ref.shape[1]

    @pl.when(pl.program_id(0) == 0)
    def _():
        cnt_ref[...] = cnt_in_ref[...]

    re, im = _unpack_pair(wp_ref[...])
    re, im = re.astype(BF16), im.astype(BF16)
    cc, cs = cc_ref[...], cs_ref[...]
    gd = FOURIER_GROUP_DIM
    fm = jnp.concatenate(
        [(jnp.dot(re[:, g * gd:(g + 1) * gd], cc, preferred_element_type=F32)
          + jnp.dot(im[:, g * gd:(g + 1) * gd], cs, preferred_element_type=F32)).astype(BF16)
         for g in range(FOURIER_GROUPS)], axis=1)
    y_f = jnp.dot(fm, wfo_ref[...], preferred_element_type=F32)
    y_a = jnp.dot(o_ref[...], wao_ref[...], preferred_element_type=F32)
    merged = (_sigmoid(gate_ref[:, :d].astype(F32)) * y_f
              + _sigmoid(gate_ref[:, d:].astype(F32)) * y_a).astype(BF16)
    x1 = x_ref[...] + g1_ref[...] * jnp.dot(merged, wout_ref[...], preferred_element_type=F32)
    x1_ref[...] = x1
    ms = jnp.mean(x1 * x1, axis=-1, keepdims=True)
    h2 = (x1 * lax.rsqrt(ms + EPS) * n2g_ref[...]) * (1.0 + sc2_ref[...]) + sh2_ref[...]
    half = d // 2
    _store_tile_rows(h2_ref, _pack_pair(h2[:, :half], h2[:, half:]))

    h_hi = h2.astype(BF16)
    h_lo = (h2 - h_hi.astype(F32)).astype(BF16)
    wr = wr_ref[...]
    r_hi = jnp.dot(h_hi, wr, preferred_element_type=F32)
    r_lo = jnp.dot(h_lo, wr, preferred_element_type=F32)
    logits = ((r_hi[:, :ROUTE_LANES] + r_hi[:, ROUTE_LANES:])
              + (r_lo[:, :ROUTE_LANES] + r_lo[:, ROUTE_LANES:])) + br_ref[...]
    lane = lax.broadcasted_iota(I32, (tm, ROUTE_LANES), 1)
    big = jnp.int32(ROUTE_LANES)
    is_g = lane < N_GROUPS
    gl = jnp.where(is_g, logits, NEG)
    gmax = jnp.max(gl, axis=-1, keepdims=True)
    gidx = jnp.min(jnp.where(gl == gmax, lane, big), axis=-1, keepdims=True)
    p_g = 1.0 / jnp.sum(jnp.where(is_g, jnp.exp(gl - gmax), 0.0), axis=-1, keepdims=True)
    in_grp = (lane >= N_GROUPS) & (lane < N_GROUPS + N_EXPERTS) & (
        lax.shift_right_logical(lane - N_GROUPS, jnp.full(lane.shape, 3, I32)) == gidx)
    el = jnp.where(in_grp, logits, NEG)
    e1v = jnp.max(el, axis=-1, keepdims=True)
    e1i = jnp.min(jnp.where(el == e1v, lane, big), axis=-1, keepdims=True)
    el2 = jnp.where(lane == e1i, NEG, el)
    e2v = jnp.max(el2, axis=-1, keepdims=True)
    e2i = jnp.min(jnp.where(el2 == e2v, lane, big), axis=-1, keepdims=True)
    t = jnp.exp(e2v - e1v)
    w1 = p_g / (1.0 + t)
    w2 = w1 * t
    sel1, sel2 = lane == e1i, lane == e2i
    member = jnp.where(sel1 | sel2, 1.0, 0.0)
    rr = lax.broadcasted_iota(I32, (tm, tm), 0)
    cc_i = lax.broadcasted_iota(I32, (tm, tm), 1)
    tri = jnp.where(cc_i < rr, 1.0, 0.0).astype(BF16)
    prefix = jnp.dot(tri, member.astype(BF16), preferred_element_type=F32) + cnt_ref[...]
    rank1 = jnp.sum(jnp.where(sel1, prefix, 0.0), axis=-1, keepdims=True)
    rank2 = jnp.sum(jnp.where(sel2, prefix, 0.0), axis=-1, keepdims=True)
    cnt_ref[...] = cnt_ref[...] + jnp.sum(member, axis=0, keepdims=True)
    key_scale = float(1 << RANK_BITS)
    key1 = (e1i - N_GROUPS).astype(F32) * key_scale + rank1
    key2 = (e2i - N_GROUPS).astype(F32) * key_scale + rank2
    route = jnp.zeros((tm, ROUTE_LANES), F32)
    for k, val in enumerate((key1, key2, w1, w2)):
        route = jnp.where(lane == k, val, route)
    route_ref[...] = route


def _post(wp, attn, gates, x, g1, sh2, sc2, norm2_g, cc, cs, wfo_b, wao_b, wout_b, wr, br,
          counts_in):
    b, s, d = x.shape
    t = b * s
    tm = ROW_TILE
    per_b_tiles = s // tm
    flat = lambda a: a.reshape(t, a.shape[-1])
    row = lambda w: pl.BlockSpec((tm, w), lambda i: (i, 0))
    per_b = pl.BlockSpec((None, 1, d), lambda i: (i // per_b_tiles, 0, 0))
    const = lambda a: _resident(a.shape, lambda i: (0,) * a.ndim)
    fw = wp.shape[-1]
    aw = attn.shape[-1]
    return pl.pallas_call(
        _post_kernel,
        grid=(t // tm,),
        in_specs=[row(fw), row(aw), row(2 * d), row(d), per_b, per_b, per_b,
                  const(norm2_g), const(cc), const(cs), const(wfo_b), const(wao_b), const(wout_b),
                  const(wr), const(br), const(counts_in)],
        out_specs=[row(d), pl.BlockSpec((tm * ROW_SUB, LANES), lambda i: (i, 0)), row(ROUTE_LANES),
                   pl.BlockSpec((1, ROUTE_LANES), lambda i: (0, 0))],
        out_shape=[jax.ShapeDtypeStruct((t, d), F32), jax.ShapeDtypeStruct((t * ROW_SUB, LANES), I32),
                   jax.ShapeDtypeStruct((t, ROUTE_LANES), F32), jax.ShapeDtypeStruct((1, ROUTE_LANES), F32)],
        compiler_params=_cparams(("arbitrary",)),
        name="post_router",
    )(flat(wp), flat(attn), flat(gates), flat(x), g1, sh2, sc2, norm2_g, cc, cs, wfo_b, wao_b, wout_b, wr, br,
      counts_in)


def _sortrows_kernel(dest_ref, pstart_ref, pend_ref, *refs, tm, rows, tile_starts):
    n_groups = len(tile_starts)
    h_refs = refs[:n_groups]
    xs_hbm, stage, zbuf, sem, zsem = refs[n_groups:]
    j = pl.program_id(0)
    n = pl.num_programs(0)
    slot = j % 2
    blk_sub = rows * ROW_SUB

    def wait_slot(sl):
        nsub = TOP_K * tm * ROW_SUB
        pltpu.make_async_copy(stage.at[0, pl.ds(0, nsub), :], xs_hbm.at[pl.ds(0, nsub), :], sem.at[sl]).wait()

    @pl.when(j == 0)
    def _():
        zbuf[...] = jnp.zeros_like(zbuf)

        def zero_block(off):
            return pltpu.make_async_copy(zbuf, xs_hbm.at[pl.ds(pl.multiple_of(off * ROW_SUB, blk_sub), blk_sub), :],
                                         zsem)

        def per_expert(fn):
            def body(e, carry):
                @pl.when(pend_ref[e] > pstart_ref[e])
                def _():
                    fn(zero_block(pend_ref[e] - rows))
                return carry
            lax.fori_loop(0, N_EXPERTS, body, 0)

        n_used = pend_ref[N_EXPERTS - 1] // rows
        n_blocks = xs_hbm.shape[0] // blk_sub

        def tail(fn):
            lax.fori_loop(n_used, n_blocks, lambda b, c: (fn(zero_block(b * rows)), c)[1], 0)

        per_expert(lambda cp: cp.start())
        tail(lambda cp: cp.start())
        per_expert(lambda cp: cp.wait())
        tail(lambda cp: cp.wait())

    @pl.when(j >= 2)
    def _():
        wait_slot(slot)

    tile = h_refs[0][...]
    for g in range(1, n_groups):
        tile = jnp.where(j >= tile_starts[g], h_refs[g][...], tile)
    stage[slot, pl.ds(0, tm * ROW_SUB), :] = tile

    def body(gi, carry):
        r0 = gi * DMA_GROUP
        base = (j * tm + r0) * TOP_K
        dests = [dest_ref[base + q] for q in range(DMA_GROUP * TOP_K)]
        for q, dest in enumerate(dests):
            pltpu.make_async_copy(_tile_row(stage.at[slot], r0 + q // TOP_K), _tile_row(xs_hbm, dest),
                                  sem.at[slot]).start()
        return carry

    lax.fori_loop(0, tm // DMA_GROUP, body, 0)

    @pl.when(j == n - 1)
    def _():
        @pl.when(j >= 1)
        def _():
            wait_slot(1 - slot)

        wait_slot(slot)


def _sortrows(h2_groups, dest, pad_start, pad_end, n_rows):
    tm = ROW_TILE
    tiles = [h.shape[0] // (ROW_SUB * tm) for h in h2_groups]
    starts = [sum(tiles[:g]) for g in range(len(tiles))]
    spec = lambda g: pl.BlockSpec((tm * ROW_SUB, LANES),
                                  lambda j, *_: (jnp.clip(j - starts[g], 0, tiles[g] - 1), 0))
    return pl.pallas_call(
        functools.partial(_sortrows_kernel, tm=tm, rows=MOE_ROWS, tile_starts=tuple(starts)),
        grid_spec=pltpu.PrefetchScalarGridSpec(
            num_scalar_prefetch=3,
            grid=(sum(tiles),),
            in_specs=[spec(g) for g in range(len(tiles))],
            out_specs=pl.BlockSpec(memory_space=pl.ANY),
            scratch_shapes=[pltpu.VMEM((2, TOP_K * tm * ROW_SUB, LANES), I32),
                            pltpu.VMEM((MOE_ROWS * ROW_SUB, LANES), I32),
                            pltpu.SemaphoreType.DMA((2,)), pltpu.SemaphoreType.DMA(())]),
        out_shape=jax.ShapeDtypeStruct((n_rows * ROW_SUB, LANES), I32),
        compiler_params=_cparams(("arbitrary",), disable_bounds_checks=True, has_side_effects=True),
        name="sort_rows",
    )(dest, pad_start, pad_end, *h2_groups)


def _expert_kernel(pos_ref, seq_ref, nused_ref, xs_ref, w1_hbm, w3_hbm, w2_hbm, y_ref,
                   w1f, w3f, w2f, wsem, w1b, w3b, w2b, *, rows):
    j = pl.program_id(0)
    weights = ((w1_hbm, w1f), (w3_hbm, w3f), (w2_hbm, w2f))

    def start_weights(k, buf):
        e = seq_ref[k]
        for src, dst in weights:
            n = src.shape[1] // WEIGHT_CHUNKS
            for c in range(WEIGHT_CHUNKS):
                pltpu.make_async_copy(src.at[e, pl.ds(c * n, n), :], dst.at[buf, pl.ds(c * n, n), :],
                                      wsem.at[buf]).start()

    def wait_weights(k, buf):
        e = seq_ref[k]
        for src, dst in weights:
            pltpu.make_async_copy(src.at[e], dst.at[buf], wsem.at[buf]).wait()

    @pl.when(j == 0)
    def _():
        start_weights(0, 0)

    pos = pos_ref[j]
    pos_prev = pos_ref[jnp.maximum(j - 1, 0)]

    @pl.when((j == 0) | (pos != pos_prev))
    def _():
        buf = pos % 2
        wait_weights(pos, buf)

        @pl.when(pos + 1 < nused_ref[1])
        def _():
            start_weights(pos + 1, 1 - buf)

        w1b[...] = w1f[buf].astype(BF16)
        w3b[...] = w3f[buf].astype(BF16)
        w2b[...] = w2f[buf].astype(BF16)

    @pl.when(j < nused_ref[0])
    def _():
        hi, lo = _unpack_pair(_load_tile_rows(xs_ref, 0, rows))
        xb = jnp.concatenate([hi.astype(BF16), lo.astype(BF16)], axis=1)
        a = jnp.dot(xb, w1b[...], preferred_element_type=F32)
        g = jnp.dot(xb, w3b[...], preferred_element_type=F32)
        hid = (a * _sigmoid(a) * g).astype(BF16)
        y = jnp.dot(hid, w2b[...], preferred_element_type=F32)
        half = y.shape[1] // 2
        _store_tile_rows(y_ref, _pack_pair(y[:, :half], y[:, half:]))

    @pl.when(j >= nused_ref[0])
    def _():
        y_ref[...] = jnp.zeros_like(y_ref)


def _experts(xs, block_pos, expert_seq, n_used, w1, w3, w2):
    rows = MOE_ROWS
    blk = rows * ROW_SUB
    d, f = w1.shape[1], w1.shape[2]
    hbm = pl.BlockSpec(memory_space=pl.ANY)
    return pl.pallas_call(
        functools.partial(_expert_kernel, rows=rows),
        grid_spec=pltpu.PrefetchScalarGridSpec(
            num_scalar_prefetch=3,
            grid=(xs.shape[0] // blk,),
            in_specs=[pl.BlockSpec((blk, LANES), lambda j, bp, es, nu: (jnp.minimum(j, nu[0] - 1), 0)),
                      hbm, hbm, hbm],
            out_specs=pl.BlockSpec((blk, LANES), lambda j, *_: (j, 0)),
            scratch_shapes=[pltpu.VMEM((2, d, f), F32), pltpu.VMEM((2, d, f), F32), pltpu.VMEM((2, f, d), F32),
                            pltpu.SemaphoreType.DMA((2,)),
                            pltpu.VMEM((d, f), BF16), pltpu.VMEM((d, f), BF16), pltpu.VMEM((f, d), BF16)]),
        out_shape=jax.ShapeDtypeStruct(xs.shape, I32),
        compiler_params=_cparams(("arbitrary",)),
        name="experts",
    )(block_pos, expert_seq, n_used, xs, w1, w3, w2)


def _combine_kernel(dest_ref, y_hbm, x1_ref, route_ref, g2_ref, o_ref, ybuf, sem, *, tm):
    i = pl.program_id(0)
    n = pl.num_programs(0)
    slot = i % 2

    def start_gather(tile, sl):
        def body(gi, carry):
            r0 = gi * DMA_GROUP
            base = (tile * tm + r0) * TOP_K
            dests = [dest_ref[base + q] for q in range(DMA_GROUP * TOP_K)]
            for q, dest in enumerate(dests):
                pltpu.make_async_copy(_tile_row(y_hbm, dest),
                                      _tile_row(ybuf.at[sl], (q % TOP_K) * tm + r0 + q // TOP_K),
                                      sem.at[sl]).start()
            return carry

        lax.fori_loop(0, tm // DMA_GROUP, body, 0)

    @pl.when(i == 0)
    def _():
        start_gather(0, 0)

    @pl.when(i + 1 < n)
    def _():
        start_gather(i + 1, 1 - slot)

    pltpu.make_async_copy(y_hbm.at[pl.ds(0, TOP_K * tm * ROW_SUB), :], ybuf.at[slot], sem.at[slot]).wait()
    route = route_ref[...]
    w1 = route[:, 2:3]
    w2 = route[:, 3:4]
    hi1, lo1 = _unpack_pair(_load_tile_rows(ybuf, 0, tm, lead=(slot,)))
    hi2, lo2 = _unpack_pair(_load_tile_rows(ybuf, tm, tm, lead=(slot,)))
    half = o_ref.shape[1] // 2
    o_ref[:, :half] = x1_ref[:, :half] + g2_ref[:, :half] * (w1 * hi1 + w2 * hi2)
    o_ref[:, half:] = x1_ref[:, half:] + g2_ref[:, half:] * (w1 * lo1 + w2 * lo2)


def _combine(y, dest, x1, route, g2, s):
    t, d = x1.shape
    tm = ROW_TILE
    per_b_tiles = s // tm
    return pl.pallas_call(
        functools.partial(_combine_kernel, tm=tm),
        grid_spec=pltpu.PrefetchScalarGridSpec(
            num_scalar_prefetch=1,
            grid=(t // tm,),
            in_specs=[pl.BlockSpec(memory_space=pl.ANY),
                      pl.BlockSpec((tm, d), lambda i, ds: (i, 0)),
                      pl.BlockSpec((tm, ROUTE_LANES), lambda i, ds: (i, 0)),
                      pl.BlockSpec((None, 1, d), lambda i, ds: (i // per_b_tiles, 0, 0))],
            out_specs=pl.BlockSpec((tm, d), lambda i, ds: (i, 0)),
            scratch_shapes=[pltpu.VMEM((2, TOP_K * tm * ROW_SUB, LANES), I32), pltpu.SemaphoreType.DMA((2,))]),
        out_shape=jax.ShapeDtypeStruct((t, d), F32),
        compiler_params=_cparams(("arbitrary",), disable_bounds_checks=True),
        name="combine",
    )(dest, y, x1, route, g2)


def _dispatch_plan(route_keys, counts, t):
    rows = MOE_ROWS
    keys = route_keys.astype(I32)
    expert = lax.shift_right_logical(keys, jnp.full(keys.shape, RANK_BITS, I32))
    rank = keys & ((1 << RANK_BITS) - 1)
    cnt = counts[0, N_GROUPS:N_GROUPS + N_EXPERTS].astype(I32)
    padded = (cnt + rows - 1) // rows * rows
    pad_end = jnp.cumsum(padded)
    pad_start = pad_end - padded
    onehot = expert[..., None] == jnp.arange(N_EXPERTS, dtype=I32)
    dest = (jnp.sum(jnp.where(onehot, pad_start, 0), axis=-1) + rank).reshape(t * TOP_K)
    n_blocks = (t * TOP_K + N_EXPERTS * (rows - 1) + rows - 1) // rows
    n_used = pad_end[-1:] // rows
    blk = jnp.minimum(jnp.arange(n_blocks, dtype=I32), n_used - 1) * rows
    block_e = jnp.sum((pad_end[None, :] <= blk[:, None]).astype(I32), axis=1)
    owns = cnt > 0
    expert_seq = jnp.argsort(jnp.logical_not(owns), stable=True).astype(I32)
    seq_index = jnp.cumsum(owns.astype(I32)) - 1
    block_pos = jnp.sum(jnp.where(block_e[:, None] == jnp.arange(N_EXPERTS, dtype=I32), seq_index, 0), axis=1)
    used = jnp.concatenate([n_used, jnp.sum(owns.astype(I32), keepdims=True)]).astype(I32)
    return dest, pad_start, pad_end, block_pos.astype(I32), expert_seq, used, n_blocks * rows


def _layer(xs, mods, p):
    d = xs[0].shape[-1]
    t_all = sum(x.shape[0] * x.shape[1] for x in xs)
    counts = jnp.zeros((1, ROUTE_LANES), F32)
    rope = _rope_tables(max(x.shape[1] for x in xs))
    per_group, h2_groups, tok_off = [], [], 0
    for x, mod in zip(xs, mods):
        b, s, _ = x.shape
        sh1, sc1, g1, sh2, sc2, g2 = [m.reshape(b, 1, d) for m in jnp.split(mod, 6, axis=-1)]
        u, qkv, gates = _inproj(x, sh1, sc1, p["norm1_g"], p["q_norm_g"], p["k_norm_g"], p["w_in"], rope)
        wp = _seq_fft(u)
        attn = _attention(qkv, p["sink"])
        x1, h2, route, counts = _post(wp, attn, gates, x, g1, sh2, sc2, p["norm2_g"], p["cc"], p["cs"],
                                      p["w_fourier_out"], p["w_attn_out"], p["w_out"],
                                      p["w_router"], p["b_router"], counts)
        per_group.append((x1, route, g2, tok_off))
        h2_groups.append(h2)
        tok_off += b * s
    route_keys = jnp.concatenate([r[:, 0:TOP_K] for _, r, _, _ in per_group], axis=0)
    dest, pad_start, pad_end, block_pos, expert_seq, n_used, n_rows = _dispatch_plan(route_keys, counts, t_all)
    xs_sorted = _sortrows(h2_groups, dest, pad_start, pad_end, n_rows)
    y = _experts(xs_sorted, block_pos, expert_seq, n_used, p["w1"], p["w3"], p["w2"])
    outs = []
    for x, (x1, route, g2, off) in zip(xs, per_group):
        b, s, _ = x.shape
        dest_g = lax.slice(dest, (off * TOP_K,), ((off + b * s) * TOP_K,))
        outs.append(_combine(y, dest_g, x1, route, g2, s).reshape(b, s, d))
    return outs


def _channel_dft():
    n = FOURIER_GROUP_DIM
    k = np.arange(n, dtype=np.int64)
    ang = ((k[:, None] * k[None, :]) % n) * (2.0 * math.pi / n)
    return (jnp.asarray((np.cos(ang) * n ** -0.5).astype(BF16)),
            jnp.asarray((np.sin(ang) * n ** -0.5).astype(BF16)))


def kernel(x_prompt, x_sample, c_prompt, c_sample, w_ada, b_ada, norm1_g, w_in, q_norm_g, k_norm_g, sink,
           w_fourier_out, w_attn_out, w_out, norm2_g, w_group, b_group, w_expert, b_expert, w1, w3, w2):
    depth = w_ada.shape[0]
    d = x_prompt.shape[-1]
    bp = c_prompt.shape[0]
    bs = c_sample.shape[0]
    cc, cs = _channel_dft()
    xp, xs = x_prompt, x_sample
    for l in range(depth):
        c_all = jnp.concatenate([c_prompt, c_sample, jnp.zeros((8 - (bp + bs) % 8, d), F32)], axis=0)
        mod = _adaln(c_all, w_ada[l], b_ada[l])
        pad = ROUTE_LANES - N_GROUPS - N_EXPERTS
        w_router = jnp.concatenate([w_group[l], w_expert[l], jnp.zeros((d, pad), F32)], axis=1)
        w_router_hi = w_router.astype(BF16)
        p = {
            "norm1_g": norm1_g[l].reshape(1, d), "norm2_g": norm2_g[l].reshape(1, d),
            "q_norm_g": q_norm_g[l].reshape(1, HEAD_DIM), "k_norm_g": k_norm_g[l].reshape(1, HEAD_DIM),
            "sink": sink[l], "w_in": w_in[l].astype(BF16),
            "w_fourier_out": w_fourier_out[l].astype(BF16), "w_attn_out": w_attn_out[l].astype(BF16),
            "w_out": w_out[l].astype(BF16), "cc": cc, "cs": cs,
            "w_router": jnp.concatenate(
                [w_router_hi, (w_router - w_router_hi.astype(F32)).astype(BF16)], axis=1),
            "b_router": jnp.concatenate([b_group[l], b_expert[l], jnp.zeros((pad,), F32)]).reshape(1, ROUTE_LANES),
            "w1": w1[l], "w3": w3[l], "w2": w2[l],
        }
        xp, xs = _layer([xp, xs], [mod[:bp], mod[bp:bp + bs]], p)
    return xp, xs
```

```python
import functools
import math

import jax
import jax.numpy as jnp
import numpy as np
from jax import lax
from jax.experimental import pallas as pl
from jax.experimental.pallas import tpu as pltpu

F32 = jnp.float32
BF16 = jnp.bfloat16
I32 = jnp.int32

HEAD_DIM = 128
N_KV_HEADS = 2
Q_PER_KV = 4
N_HEADS = N_KV_HEADS * Q_PER_KV
ROT_DIM = 32
ROPE_THETA = 500000.0
WINDOW = 128
FOURIER_GROUPS = 4
FOURIER_GROUP_DIM = 256
N_GROUPS = 8
EXPERTS_PER_GROUP = 8
N_EXPERTS = N_GROUPS * EXPERTS_PER_GROUP
TOP_K = 2
EPS = 1e-6

LANES = 128
V7X_VMEM_LIMIT = 56 * 1024 * 1024

ROW_TILE = 256
ATTN_TILE = 1024
FFT_N1 = 128
FFT_UNROLL = 16
FFT_PITCH_PAD = 8
MOE_ROWS = 512
ROUTE_LANES = 128
RANK_BITS = 17
DMA_GROUP = 8
WEIGHT_CHUNKS = 8
NEG = -1e30


def _cparams(sem, **kw):
    return pltpu.CompilerParams(dimension_semantics=sem, vmem_limit_bytes=V7X_VMEM_LIMIT, **kw)


def _resident(shape, index_map):
    return pl.BlockSpec(shape, index_map, pipeline_mode=pl.Buffered(1))


def _sigmoid(x):
    return 0.5 * jnp.tanh(0.5 * x) + 0.5


def _adaln_kernel(c_ref, w_ref, b_ref, o_ref):
    c = c_ref[...]
    s = c * _sigmoid(c)
    o_ref[...] = jnp.dot(s, w_ref[...], precision=lax.Precision.HIGHEST,
                         preferred_element_type=F32) + b_ref[...]


def _adaln(c, w_ada, b_ada):
    r, d = c.shape
    n = w_ada.shape[1]
    tn = 1024
    return pl.pallas_call(
        _adaln_kernel,
        grid=(n // tn,),
        in_specs=[pl.BlockSpec((r, d), lambda j: (0, 0)),
                  pl.BlockSpec((d, tn), lambda j: (0, j)),
                  pl.BlockSpec((1, tn), lambda j: (0, j))],
        out_specs=pl.BlockSpec((r, tn), lambda j: (0, j)),
        out_shape=jax.ShapeDtypeStruct((r, n), F32),
        compiler_params=_cparams(("parallel",)),
        name="adaln",
    )(c, w_ada, b_ada.reshape(1, n))


def _rope_tables(s):
    half = ROT_DIM // 2
    inv_freq = ROPE_THETA ** (-jnp.arange(0, ROT_DIM, 2, dtype=F32) / ROT_DIM)
    ang = jnp.arange(s, dtype=F32)[:, None] * inv_freq[None, :]
    cos, sin = jnp.cos(ang), jnp.sin(ang)
    pad = jnp.zeros((s, LANES - ROT_DIM), F32)
    cos_f = jnp.concatenate([cos, cos, jnp.ones((s, LANES - ROT_DIM), F32)], axis=1)
    sin_up = jnp.concatenate([-sin, jnp.zeros((s, half), F32), pad], axis=1)
    sin_dn = jnp.concatenate([jnp.zeros((s, half), F32), sin, pad], axis=1)
    return cos_f, sin_up, sin_dn


def _inproj_kernel(x_ref, sh_ref, sc_ref, g_ref, qg_ref, kg_ref, cos_ref, sup_ref, sdn_ref, w_ref,
                   u_ref, qkv_ref, gate_ref, *, fw, aw, kvw):
    x = x_ref[...]
    ms = jnp.mean(x * x, axis=-1, keepdims=True)
    h = (x * lax.rsqrt(ms + EPS) * g_ref[...]) * (1.0 + sc_ref[...]) + sh_ref[...]
    hb = h.astype(BF16)
    chunk = 512
    for c0 in range(0, fw, chunk):
        u_ref[:, c0:c0 + chunk] = jnp.dot(hb, w_ref[:, c0:c0 + chunk], preferred_element_type=F32)
    cos_f, sin_up, sin_dn = cos_ref[...], sup_ref[...], sdn_ref[...]
    half = ROT_DIM // 2
    scale = HEAD_DIM ** -0.5
    for c0 in list(range(0, aw, chunk)) + [aw]:
        width = chunk if c0 < aw else kvw
        acc = jnp.dot(hb, w_ref[:, fw + c0:fw + c0 + width], preferred_element_type=F32)
        for hh in range(width // HEAD_DIM):
            col = c0 + hh * HEAD_DIM
            t = acc[:, hh * HEAD_DIM:(hh + 1) * HEAD_DIM]
            is_q = col < aw
            gain = qg_ref[...] if is_q else kg_ref[...]
            t = t * lax.rsqrt(jnp.mean(t * t, axis=-1, keepdims=True) + EPS) * gain
            t = (t * cos_f + pltpu.roll(t, LANES - half, axis=1) * sin_up
                 + pltpu.roll(t, half, axis=1) * sin_dn)
            if is_q:
                t = t * scale
            qkv_ref[:, col:col + HEAD_DIM] = t.astype(BF16)
    v0 = fw + aw + kvw
    qkv_ref[:, aw + kvw:aw + 2 * kvw] = jnp.dot(
        hb, w_ref[:, v0:v0 + kvw], preferred_element_type=F32).astype(BF16)
    g0 = v0 + kvw
    gw = gate_ref.shape[-1]
    for c0 in range(0, gw, chunk):
        gate_ref[:, c0:c0 + chunk] = jnp.dot(
            hb, w_ref[:, g0 + c0:g0 + c0 + chunk], preferred_element_type=F32).astype(BF16)


def _inproj(x, shift, scale, norm_g, q_g, k_g, w_in_b, tables):
    b, s, d = x.shape
    fw = FOURIER_GROUPS * FOURIER_GROUP_DIM
    aw = N_HEADS * HEAD_DIM
    kvw = N_KV_HEADS * HEAD_DIM
    gw = 2 * d
    tm = ROW_TILE
    row = lambda w: pl.BlockSpec((None, tm, w), lambda bi, i: (bi, i, 0))
    per_b = pl.BlockSpec((None, 1, d), lambda bi, i: (bi, 0, 0))
    const = lambda w: pl.BlockSpec((1, w), lambda bi, i: (0, 0))
    tab = pl.BlockSpec((tm, LANES), lambda bi, i: (i, 0))
    return pl.pallas_call(
        functools.partial(_inproj_kernel, fw=fw, aw=aw, kvw=kvw),
        grid=(b, s // tm),
        in_specs=[row(d), per_b, per_b, const(d), const(HEAD_DIM), const(HEAD_DIM), tab, tab, tab,
                  _resident(w_in_b.shape, lambda bi, i: (0, 0))],
        out_specs=[row(fw), row(aw + 2 * kvw), row(gw)],
        out_shape=[jax.ShapeDtypeStruct((b, s, fw), F32),
                   jax.ShapeDtypeStruct((b, s, aw + 2 * kvw), BF16),
                   jax.ShapeDtypeStruct((b, s, gw), BF16)],
        compiler_params=_cparams(("parallel", "parallel")),
        name="inproj",
    )(x, shift, scale, norm_g, q_g, k_g, *tables, w_in_b)


def _pack_pair(a, b):
    ab = lax.bitcast_convert_type(a.astype(BF16).astype(F32), I32)
    bb = lax.bitcast_convert_type(b.astype(BF16).astype(F32), I32)
    return ab | lax.shift_right_logical(bb, jnp.full(bb.shape, 16, I32))


def _unpack_pair(p):
    hi = lax.bitcast_convert_type(p & jnp.int32(-65536), F32)
    lo = lax.bitcast_convert_type(lax.shift_left(p, jnp.full(p.shape, 16, I32)), F32)
    return hi, lo


def _fft_tables(s):
    n1 = FFT_N1
    n2 = s // n1
    k1 = np.arange(n1, dtype=np.int64)
    nn = (n2 * np.arange(n1, dtype=np.int64))[None, None, :] + np.arange(n2, dtype=np.int64)[:, None, None]
    ang = ((k1[None, :, None] * nn) % s) * (2.0 * math.pi / s)
    sc1 = n1 ** -0.5
    g = np.concatenate([np.cos(ang) * sc1, -np.sin(ang) * sc1], axis=1).astype(BF16)
    k2 = np.arange(n2, dtype=np.int64)
    ang2 = ((k2[:, None] * k2[None, :]) % n2) * (2.0 * math.pi / n2)
    c2, s2 = np.cos(ang2) * n2 ** -0.5, np.sin(ang2) * n2 ** -0.5
    f2 = np.concatenate([np.concatenate([c2, s2], axis=1),
                         np.concatenate([-s2, c2], axis=1)], axis=0).astype(BF16)
    return jnp.asarray(g), jnp.asarray(f2)


def _fft_kernel(u_ref, g_ref, f2_ref, o_ref, y_scr, *, n1, n2, unroll):
    pitch = n1 + FFT_PITCH_PAD

    def stage1(i, carry):
        for uu in range(unroll):
            m = i * unroll + uu
            xm = u_ref[pl.ds(m, n1, stride=n2), :].astype(BF16)
            y = jnp.dot(g_ref[m], xm, preferred_element_type=F32)
            y_scr[pl.ds(pl.multiple_of(m * pitch, 8), n1), :] = _pack_pair(y[:n1], y[n1:])
        return carry

    lax.fori_loop(0, n2 // unroll, stage1, 0)
    f2 = f2_ref[...]

    def stage2(i, carry):
        for uu in range(unroll):
            k1 = i * unroll + uu
            yr, yi = _unpack_pair(y_scr[pl.ds(k1, n2, stride=pitch), :])
            rhs = jnp.concatenate([yr.astype(BF16), yi.astype(BF16)], axis=0)
            z = jnp.dot(f2, rhs, preferred_element_type=F32)
            y_scr[pl.ds(k1, n2, stride=pitch), :] = _pack_pair(z[:n2], z[n2:])
        return carry

    lax.fori_loop(0, n1 // unroll, stage2, 0)

    def compact(i, carry):
        for uu in range(unroll):
            k2 = i * unroll + uu
            o_ref[pl.ds(pl.multiple_of(k2 * n1, n1), n1), :] = y_scr[pl.ds(pl.multiple_of(k2 * pitch, 8), n1), :]
        return carry

    lax.fori_loop(0, n2 // unroll, compact, 0)


def _seq_fft(u):
    b, s, c = u.shape
    n1 = FFT_N1
    n2 = s // n1
    g, f2 = _fft_tables(s)
    unroll = min(FFT_UNROLL, n2)
    blk = pl.BlockSpec((None, s, LANES), lambda bi, j: (bi, 0, j))
    return pl.pallas_call(
        functools.partial(_fft_kernel, n1=n1, n2=n2, unroll=unroll),
        grid=(b, c // LANES),
        in_specs=[blk, _resident(g.shape, lambda bi, j: (0, 0, 0)), _resident(f2.shape, lambda bi, j: (0, 0))],
        out_specs=blk,
        out_shape=jax.ShapeDtypeStruct((b, s, c), I32),
        scratch_shapes=[pltpu.VMEM((n2 * (n1 + FFT_PITCH_PAD), LANES), I32)],
        compiler_params=_cparams(("parallel", "parallel")),
        name="seq_fft",
    )(u, g, f2)


def _attn_kernel(sink_ref, q_ref, kp_ref, kc_ref, kn_ref, vp_ref, vc_ref, vn_ref, o_ref,
                 s_scr, p_scr, inv_scr, *, tq):
    i = pl.program_id(1)
    first = i == 0
    last = i == pl.num_programs(1) - 1
    nsub = tq // WINDOW
    nq = Q_PER_KV * WINDOW
    nk = 3 * WINDOW
    c = lax.broadcasted_iota(I32, (nk, nq), 0)
    r = lax.broadcasted_iota(I32, (nk, nq), 1) & (WINDOW - 1)
    band = ((c >= r) & (c < WINDOW)) | ((c >= WINDOW) & (c < 2 * WINDOW)) | ((c >= 2 * WINDOW) & (c - 2 * WINDOW <= r))
    pairs = [(h, sb) for h in range(N_KV_HEADS) for sb in range(nsub)]
    kcat, vt = [], []
    for h in range(N_KV_HEADS):
        hs = slice(h * HEAD_DIM, (h + 1) * HEAD_DIM)
        kcat.append(jnp.concatenate([kp_ref[:, hs], kc_ref[:, hs], kn_ref[:, hs]], axis=0))
        vcat = jnp.concatenate([vp_ref[:, hs], vc_ref[:, hs], vn_ref[:, hs]], axis=0)
        vt.append(vcat.astype(F32).T.astype(BF16))

    for n, (h, sb) in enumerate(pairs):
        q4 = jnp.concatenate(
            [q_ref[sb * WINDOW:(sb + 1) * WINDOW,
                   (h * Q_PER_KV + g) * HEAD_DIM:(h * Q_PER_KV + g + 1) * HEAD_DIM]
             for g in range(Q_PER_KV)], axis=0)
        kw = kcat[h][sb * WINDOW:(sb + 3) * WINDOW]
        s_scr[n] = lax.dot_general(kw, q4, (((1,), (1,)), ((), ())), preferred_element_type=F32)

    for n, (h, sb) in enumerate(pairs):
        sc = jnp.where(band, s_scr[n], NEG)
        if sb == 0:
            sc = jnp.where((c < WINDOW) & first, NEG, sc)
        if sb == nsub - 1:
            sc = jnp.where((c >= 2 * WINDOW) & last, NEG, sc)
        sink = jnp.concatenate(
            [jnp.full((1, WINDOW), sink_ref[h * Q_PER_KV + g], F32) for g in range(Q_PER_KV)], axis=1)
        m = jnp.maximum(jnp.max(sc, axis=0, keepdims=True), sink)
        p = jnp.exp(sc - m)
        denom = jnp.sum(p, axis=0, keepdims=True) + jnp.exp(sink - m)
        p_scr[n] = p.astype(BF16)
        inv_scr[pl.ds(n, 1), :] = 1.0 / denom

    for n, (h, sb) in enumerate(pairs):
        ot = jnp.dot(vt[h][:, sb * WINDOW:(sb + 3) * WINDOW], p_scr[n], preferred_element_type=F32)
        o = (ot * inv_scr[pl.ds(n, 1), :]).T
        for g in range(Q_PER_KV):
            col = (h * Q_PER_KV + g) * HEAD_DIM
            o_ref[sb * WINDOW:(sb + 1) * WINDOW, col:col + HEAD_DIM] = (
                o[g * WINDOW:(g + 1) * WINDOW].astype(BF16))


def _attention(qkv, sink):
    b, s, _ = qkv.shape
    aw = N_HEADS * HEAD_DIM
    kvw = N_KV_HEADS * HEAD_DIM
    tq = min(ATTN_TILE, s)
    per = tq // WINDOW
    nblk = s // WINDOW
    kcol = aw // kvw
    vcol = kcol + 1
    halo = lambda col, off: pl.BlockSpec(
        (None, WINDOW, kvw),
        lambda bi, i, sk: (bi, jnp.clip(i * per + off, 0, nblk - 1), col))
    main = lambda col: pl.BlockSpec((None, tq, kvw), lambda bi, i, sk: (bi, i, col))
    return pl.pallas_call(
        functools.partial(_attn_kernel, tq=tq),
        grid_spec=pltpu.PrefetchScalarGridSpec(
            num_scalar_prefetch=1,
            grid=(b, s // tq),
            in_specs=[pl.BlockSpec((None, tq, aw), lambda bi, i, sk: (bi, i, 0)),
                      halo(kcol, -1), main(kcol), halo(kcol, per),
                      halo(vcol, -1), main(vcol), halo(vcol, per)],
            out_specs=pl.BlockSpec((None, tq, aw), lambda bi, i, sk: (bi, i, 0)),
            scratch_shapes=[pltpu.VMEM((N_KV_HEADS * per, 3 * WINDOW, Q_PER_KV * WINDOW), F32),
                            pltpu.VMEM((N_KV_HEADS * per, 3 * WINDOW, Q_PER_KV * WINDOW), BF16),
                            pltpu.VMEM((N_KV_HEADS * per, Q_PER_KV * WINDOW), F32)]),
        out_shape=jax.ShapeDtypeStruct((b, s, aw), BF16),
        compiler_params=_cparams(("parallel", "parallel")),
        name="attention",
    )(sink, qkv, qkv, qkv, qkv, qkv, qkv, qkv)


ROW_WORDS = 1024
ROW_SUB = ROW_WORDS // LANES


def _store_tile_rows(ref, val, base=0, lead=()):
    r = val.shape[0]
    for j in range(ROW_SUB):
        ref[lead + (pl.ds(base * ROW_SUB + j, r, stride=ROW_SUB), slice(None))] = val[:, j * LANES:(j + 1) * LANES]


def _load_tile_rows(ref, base, r, lead=()):
    return jnp.concatenate(
        [ref[lead + (pl.ds(base * ROW_SUB + j, r, stride=ROW_SUB), slice(None))] for j in range(ROW_SUB)], axis=1)


def _tile_row(ref, row):
    return ref.at[pl.ds(pl.multiple_of(row * ROW_SUB, ROW_SUB), ROW_SUB), :]


def _post_kernel(wp_ref, o_ref, gate_ref, x_ref, g1_ref, sh2_ref, sc2_ref, n2g_ref, cc_ref, cs_ref,
                 wfo_ref, wao_ref, wout_ref, wr_ref, br_ref, cnt_in_ref,
                 x1_ref, h2_ref, route_ref, cnt_ref):
    tm = x_ref.shape[0]
    d = x_ref.shape[1]

    @pl.when(pl.program_id(0) == 0)
    def _():
        cnt_ref[...] = cnt_in_ref[...]

    re, im = _unpack_pair(wp_ref[...])
    re, im = re.astype(BF16), im.astype(BF16)
    cc, cs = cc_ref[...], cs_ref[...]
    gd = FOURIER_GROUP_DIM
    fm = jnp.concatenate(
        [(jnp.dot(re[:, g * gd:(g + 1) * gd], cc, preferred_element_type=F32)
          + jnp.dot(im[:, g * gd:(g + 1) * gd], cs, preferred_element_type=F32)).astype(BF16)
         for g in range(FOURIER_GROUPS)], axis=1)
    y_f = jnp.dot(fm, wfo_ref[...], preferred_element_type=F32)
    y_a = jnp.dot(o_ref[...], wao_ref[...], preferred_element_type=F32)
    merged = (_sigmoid(gate_ref[:, :d].astype(F32)) * y_f
              + _sigmoid(gate_ref[:, d:].astype(F32)) * y_a).astype(BF16)
    x1 = x_ref[...] + g1_ref[...] * jnp.dot(merged, wout_ref[...], preferred_element_type=F32)
    x1_ref[...] = x1
    ms = jnp.mean(x1 * x1, axis=-1, keepdims=True)
    h2 = (x1 * lax.rsqrt(ms + EPS) * n2g_ref[...]) * (1.0 + sc2_ref[...]) + sh2_ref[...]
    half = d // 2
    _store_tile_rows(h2_ref, _pack_pair(h2[:, :half], h2[:, half:]))

    h_hi = h2.astype(BF16)
    h_lo = (h2 - h_hi.astype(F32)).astype(BF16)
    wr = wr_ref[...]
    r_hi = jnp.dot(h_hi, wr, preferred_element_type=F32)
    r_lo = jnp.dot(h_lo, wr, preferred_element_type=F32)
    logits = ((r_hi[:, :ROUTE_LANES] + r_hi[:, ROUTE_LANES:])
              + (r_lo[:, :ROUTE_LANES] + r_lo[:, ROUTE_LANES:])) + br_ref[...]
    lane = lax.broadcasted_iota(I32, (tm, ROUTE_LANES), 1)
    big = jnp.int32(ROUTE_LANES)
    is_g = lane < N_GROUPS
    gl = jnp.where(is_g, logits, NEG)
    gmax = jnp.max(gl, axis=-1, keepdims=True)
    gidx = jnp.min(jnp.where(gl == gmax, lane, big), axis=-1, keepdims=True)
    p_g = 1.0 / jnp.sum(jnp.where(is_g, jnp.exp(gl - gmax), 0.0), axis=-1, keepdims=True)
    in_grp = (lane >= N_GROUPS) & (lane < N_GROUPS + N_EXPERTS) & (
        lax.shift_right_logical(lane - N_GROUPS, jnp.full(lane.shape, 3, I32)) == gidx)
    el = jnp.where(in_grp, logits, NEG)
    e1v = jnp.max(el, axis=-1, keepdims=True)
    e1i = jnp.min(jnp.where(el == e1v, lane, big), axis=-1, keepdims=True)
    el2 = jnp.where(lane == e1i, NEG, el)
    e2v = jnp.max(el2, axis=-1, keepdims=True)
    e2i = jnp.min(jnp.where(el2 == e2v, lane, big), axis=-1, keepdims=True)
    t = jnp.exp(e2v - e1v)
    w1 = p_g / (1.0 + t)
    w2 = w1 * t
    sel1, sel2 = lane == e1i, lane == e2i
    member = jnp.where(sel1 | sel2, 1.0, 0.0)
    rr = lax.broadcasted_iota(I32, (tm, tm), 0)
    cc_i = lax.broadcasted_iota(I32, (tm, tm), 1)
    tri = jnp.where(cc_i < rr, 1.0, 0.0).astype(BF16)
    prefix = jnp.dot(tri, member.astype(BF16), preferred_element_type=F32) + cnt_ref[...]
    rank1 = jnp.sum(jnp.where(sel1, prefix, 0.0), axis=-1, keepdims=True)
    rank2 = jnp.sum(jnp.where(sel2, prefix, 0.0), axis=-1, keepdims=True)
    cnt_ref[...] = cnt_ref[...] + jnp.sum(member, axis=0, keepdims=True)
    key_scale = float(1 << RANK_BITS)
    key1 = (e1i - N_GROUPS).astype(F32) * key_scale + rank1
    key2 = (e2i - N_GROUPS).astype(F32) * key_scale + rank2
    route = jnp.zeros((tm, ROUTE_LANES), F32)
    for k, val in enumerate((key1, key2, w1, w2)):
        route = jnp.where(lane == k, val, route)
    route_ref[...] = route


def _post(wp, attn, gates, x, g1, sh2, sc2, norm2_g, cc, cs, wfo_b, wao_b, wout_b, wr, br,
          counts_in):
    b, s, d = x.shape
    t = b * s
    tm = ROW_TILE
    per_b_tiles = s // tm
    flat = lambda a: a.reshape(t, a.shape[-1])
    row = lambda w: pl.BlockSpec((tm, w), lambda i: (i, 0))
    per_b = pl.BlockSpec((None, 1, d), lambda i: (i // per_b_tiles, 0, 0))
    const = lambda a: _resident(a.shape, lambda i: (0,) * a.ndim)
    fw = wp.shape[-1]
    aw = attn.shape[-1]
    return pl.pallas_call(
        _post_kernel,
        grid=(t // tm,),
        in_specs=[row(fw), row(aw), row(2 * d), row(d), per_b, per_b, per_b,
                  const(norm2_g), const(cc), const(cs), const(wfo_b), const(wao_b), const(wout_b),
                  const(wr), const(br), const(counts_in)],
        out_specs=[row(d), pl.BlockSpec((tm * ROW_SUB, LANES), lambda i: (i, 0)), row(ROUTE_LANES),
                   pl.BlockSpec((1, ROUTE_LANES), lambda i: (0, 0))],
        out_shape=[jax.ShapeDtypeStruct((t, d), F32), jax.ShapeDtypeStruct((t * ROW_SUB, LANES), I32),
                   jax.ShapeDtypeStruct((t, ROUTE_LANES), F32), jax.ShapeDtypeStruct((1, ROUTE_LANES), F32)],
        compiler_params=_cparams(("arbitrary",)),
        name="post_router",
    )(flat(wp), flat(attn), flat(gates), flat(x), g1, sh2, sc2, norm2_g, cc, cs, wfo_b, wao_b, wout_b, wr, br,
      counts_in)


def _sortrows_kernel(dest_ref, pstart_ref, pend_ref, *refs, tm, rows, tile_starts):
    n_groups = len(tile_starts)
    h_refs = refs[:n_groups]
    xs_hbm, stage, zbuf, sem, zsem = refs[n_groups:]
    j = pl.program_id(0)
    n = pl.num_programs(0)
    slot = j % 2
    blk_sub = rows * ROW_SUB

    def wait_slot(sl):
        nsub = TOP_K * tm * ROW_SUB
        pltpu.make_async_copy(stage.at[0, pl.ds(0, nsub), :], xs_hbm.at[pl.ds(0, nsub), :], sem.at[sl]).wait()

    @pl.when(j == 0)
    def _():
        zbuf[...] = jnp.zeros_like(zbuf)

        def zero_block(off):
            return pltpu.make_async_copy(zbuf, xs_hbm.at[pl.ds(pl.multiple_of(off * ROW_SUB, blk_sub), blk_sub), :],
                                         zsem)

        def per_expert(fn):
            def body(e, carry):
                @pl.when(pend_ref[e] > pstart_ref[e])
                def _():
                    fn(zero_block(pend_ref[e] - rows))
                return carry
            lax.fori_loop(0, N_EXPERTS, body, 0)

        n_used = pend_ref[N_EXPERTS - 1] // rows
        n_blocks = xs_hbm.shape[0] // blk_sub

        def tail(fn):
            lax.fori_loop(n_used, n_blocks, lambda b, c: (fn(zero_block(b * rows)), c)[1], 0)

        per_expert(lambda cp: cp.start())
        tail(lambda cp: cp.start())
        per_expert(lambda cp: cp.wait())
        tail(lambda cp: cp.wait())

    @pl.when(j >= 2)
    def _():
        wait_slot(slot)

    tile = h_refs[0][...]
    for g in range(1, n_groups):
        tile = jnp.where(j >= tile_starts[g], h_refs[g][...], tile)
    stage[slot, pl.ds(0, tm * ROW_SUB), :] = tile

    def body(gi, carry):
        r0 = gi * DMA_GROUP
        base = (j * tm + r0) * TOP_K
        dests = [dest_ref[base + q] for q in range(DMA_GROUP * TOP_K)]
        for q, dest in enumerate(dests):
            pltpu.make_async_copy(_tile_row(stage.at[slot], r0 + q // TOP_K), _tile_row(xs_hbm, dest),
                                  sem.at[slot]).start()
        return carry

    lax.fori_loop(0, tm // DMA_GROUP, body, 0)

    @pl.when(j == n - 1)
    def _():
        @pl.when(j >= 1)
        def _():
            wait_slot(1 - slot)

        wait_slot(slot)


def _sortrows(h2_groups, dest, pad_start, pad_end, n_rows):
    tm = ROW_TILE
    tiles = [h.shape[0] // (ROW_SUB * tm) for h in h2_groups]
    starts = [sum(tiles[:g]) for g in range(len(tiles))]
    spec = lambda g: pl.BlockSpec((tm * ROW_SUB, LANES),
                                  lambda j, *_: (jnp.clip(j - starts[g], 0, tiles[g] - 1), 0))
    return pl.pallas_call(
        functools.partial(_sortrows_kernel, tm=tm, rows=MOE_ROWS, tile_starts=tuple(starts)),
        grid_spec=pltpu.PrefetchScalarGridSpec(
            num_scalar_prefetch=3,
            grid=(sum(tiles),),
            in_specs=[spec(g) for g in range(len(tiles))],
            out_specs=pl.BlockSpec(memory_space=pl.ANY),
            scratch_shapes=[pltpu.VMEM((2, TOP_K * tm * ROW_SUB, LANES), I32),
                            pltpu.VMEM((MOE_ROWS * ROW_SUB, LANES), I32),
                            pltpu.SemaphoreType.DMA((2,)), pltpu.SemaphoreType.DMA(())]),
        out_shape=jax.ShapeDtypeStruct((n_rows * ROW_SUB, LANES), I32),
        compiler_params=_cparams(("arbitrary",), disable_bounds_checks=True, has_side_effects=True),
        name="sort_rows",
    )(dest, pad_start, pad_end, *h2_groups)


def _expert_kernel(pos_ref, seq_ref, nused_ref, xs_ref, w1_hbm, w3_hbm, w2_hbm, y_ref,
                   w1f, w3f, w2f, wsem, w1b, w3b, w2b, *, rows):
    j = pl.program_id(0)
    weights = ((w1_hbm, w1f), (w3_hbm, w3f), (w2_hbm, w2f))

    def start_weights(k, buf):
        e = seq_ref[k]
        for src, dst in weights:
            n = src.shape[1] // WEIGHT_CHUNKS
            for c in range(WEIGHT_CHUNKS):
                pltpu.make_async_copy(src.at[e, pl.ds(c * n, n), :], dst.at[buf, pl.ds(c * n, n), :],
                                      wsem.at[buf]).start()

    def wait_weights(k, buf):
        e = seq_ref[k]
        for src, dst in weights:
            pltpu.make_async_copy(src.at[e], dst.at[buf], wsem.at[buf]).wait()

    @pl.when(j == 0)
    def _():
        start_weights(0, 0)

    pos = pos_ref[j]
    pos_prev = pos_ref[jnp.maximum(j - 1, 0)]

    @pl.when((j == 0) | (pos != pos_prev))
    def _():
        buf = pos % 2
        wait_weights(pos, buf)

        @pl.when(pos + 1 < nused_ref[1])
        def _():
            start_weights(pos + 1, 1 - buf)

        w1b[...] = w1f[buf].astype(BF16)
        w3b[...] = w3f[buf].astype(BF16)
        w2b[...] = w2f[buf].astype(BF16)

    @pl.when(j < nused_ref[0])
    def _():
        hi, lo = _unpack_pair(_load_tile_rows(xs_ref, 0, rows))
        xb = jnp.concatenate([hi.astype(BF16), lo.astype(BF16)], axis=1)
        a = jnp.dot(xb, w1b[...], preferred_element_type=F32)
        g = jnp.dot(xb, w3b[...], preferred_element_type=F32)
        hid = (a * _sigmoid(a) * g).astype(BF16)
        y = jnp.dot(hid, w2b[...], preferred_element_type=F32)
        half = y.shape[1] // 2
        _store_tile_rows(y_ref, _pack_pair(y[:, :half], y[:, half:]))

    @pl.when(j >= nused_ref[0])
    def _():
        y_ref[...] = jnp.zeros_like(y_ref)


def _experts(xs, block_pos, expert_seq, n_used, w1, w3, w2):
    rows = MOE_ROWS
    blk = rows * ROW_SUB
    d, f = w1.shape[1], w1.shape[2]
    hbm = pl.BlockSpec(memory_space=pl.ANY)
    return pl.pallas_call(
        functools.partial(_expert_kernel, rows=rows),
        grid_spec=pltpu.PrefetchScalarGridSpec(
            num_scalar_prefetch=3,
            grid=(xs.shape[0] // blk,),
            in_specs=[pl.BlockSpec((blk, LANES), lambda j, bp, es, nu: (jnp.minimum(j, nu[0] - 1), 0)),
                      hbm, hbm, hbm],
            out_specs=pl.BlockSpec((blk, LANES), lambda j, *_: (j, 0)),
            scratch_shapes=[pltpu.VMEM((2, d, f), F32), pltpu.VMEM((2, d, f), F32), pltpu.VMEM((2, f, d), F32),
                            pltpu.SemaphoreType.DMA((2,)),
                            pltpu.VMEM((d, f), BF16), pltpu.VMEM((d, f), BF16), pltpu.VMEM((f, d), BF16)]),
        out_shape=jax.ShapeDtypeStruct(xs.shape, I32),
        compiler_params=_cparams(("arbitrary",)),
        name="experts",
    )(block_pos, expert_seq, n_used, xs, w1, w3, w2)


def _combine_kernel(dest_ref, y_hbm, x1_ref, route_ref, g2_ref, o_ref, ybuf, sem, *, tm):
    i = pl.program_id(0)
    n = pl.num_programs(0)
    slot = i % 2

    def start_gather(tile, sl):
        def body(gi, carry):
            r0 = gi * DMA_GROUP
            base = (tile * tm + r0) * TOP_K
            dests = [dest_ref[base + q] for q in range(DMA_GROUP * TOP_K)]
            for q, dest in enumerate(dests):
                pltpu.make_async_copy(_tile_row(y_hbm, dest),
                                      _tile_row(ybuf.at[sl], (q % TOP_K) * tm + r0 + q // TOP_K),
                                      sem.at[sl]).start()
            return carry

        lax.fori_loop(0, tm // DMA_GROUP, body, 0)

    @pl.when(i == 0)
    def _():
        start_gather(0, 0)

    @pl.when(i + 1 < n)
    def _():
        start_gather(i + 1, 1 - slot)

    pltpu.make_async_copy(y_hbm.at[pl.ds(0, TOP_K * tm * ROW_SUB), :], ybuf.at[slot], sem.at[slot]).wait()
    route = route_ref[...]
    w1 = route[:, 2:3]
    w2 = route[:, 3:4]
    hi1, lo1 = _unpack_pair(_load_tile_rows(ybuf, 0, tm, lead=(slot,)))
    hi2, lo2 = _unpack_pair(_load_tile_rows(ybuf, tm, tm, lead=(slot,)))
    half = o_ref.shape[1] // 2
    o_ref[:, :half] = x1_ref[:, :half] + g2_ref[:, :half] * (w1 * hi1 + w2 * hi2)
    o_ref[:, half:] = x1_ref[:, half:] + g2_ref[:, half:] * (w1 * lo1 + w2 * lo2)


def _combine(y, dest, x1, route, g2, s):
    t, d = x1.shape
    tm = ROW_TILE
    per_b_tiles = s // tm
    return pl.pallas_call(
        functools.partial(_combine_kernel, tm=tm),
        grid_spec=pltpu.PrefetchScalarGridSpec(
            num_scalar_prefetch=1,
            grid=(t // tm,),
            in_specs=[pl.BlockSpec(memory_space=pl.ANY),
                      pl.BlockSpec((tm, d), lambda i, ds: (i, 0)),
                      pl.BlockSpec((tm, ROUTE_LANES), lambda i, ds: (i, 0)),
                      pl.BlockSpec((None, 1, d), lambda i, ds: (i // per_b_tiles, 0, 0))],
            out_specs=pl.BlockSpec((tm, d), lambda i, ds: (i, 0)),
            scratch_shapes=[pltpu.VMEM((2, TOP_K * tm * ROW_SUB, LANES), I32), pltpu.SemaphoreType.DMA((2,))]),
        out_shape=jax.ShapeDtypeStruct((t, d), F32),
        compiler_params=_cparams(("arbitrary",), disable_bounds_checks=True),
        name="combine",
    )(dest, y, x1, route, g2)


def _dispatch_plan(route_keys, counts, t):
    rows = MOE_ROWS
    keys = route_keys.astype(I32)
    expert = lax.shift_right_logical(keys, jnp.full(keys.shape, RANK_BITS, I32))
    rank = keys & ((1 << RANK_BITS) - 1)
    cnt = counts[0, N_GROUPS:N_GROUPS + N_EXPERTS].astype(I32)
    padded = (cnt + rows - 1) // rows * rows
    pad_end = jnp.cumsum(padded)
    pad_start = pad_end - padded
    onehot = expert[..., None] == jnp.arange(N_EXPERTS, dtype=I32)
    dest = (jnp.sum(jnp.where(onehot, pad_start, 0), axis=-1) + rank).reshape(t * TOP_K)
    n_blocks = (t * TOP_K + N_EXPERTS * (rows - 1) + rows - 1) // rows
    n_used = pad_end[-1:] // rows
    blk = jnp.minimum(jnp.arange(n_blocks, dtype=I32), n_used - 1) * rows
    block_e = jnp.sum((pad_end[None, :] <= blk[:, None]).astype(I32), axis=1)
    owns = cnt > 0
    expert_seq = jnp.argsort(jnp.logical_not(owns), stable=True).astype(I32)
    seq_index = jnp.cumsum(owns.astype(I32)) - 1
    block_pos = jnp.sum(jnp.where(block_e[:, None] == jnp.arange(N_EXPERTS, dtype=I32), seq_index, 0), axis=1)
    used = jnp.concatenate([n_used, jnp.sum(owns.astype(I32), keepdims=True)]).astype(I32)
    return dest, pad_start, pad_end, block_pos.astype(I32), expert_seq, used, n_blocks * rows


def _layer(xs, mods, p):
    d = xs[0].shape[-1]
    t_all = sum(x.shape[0] * x.shape[1] for x in xs)
    counts = jnp.zeros((1, ROUTE_LANES), F32)
    rope = _rope_tables(max(x.shape[1] for x in xs))
    per_group, h2_groups, tok_off = [], [], 0
    for x, mod in zip(xs, mods):
        b, s, _ = x.shape
        sh1, sc1, g1, sh2, sc2, g2 = [m.reshape(b, 1, d) for m in jnp.split(mod, 6, axis=-1)]
        u, qkv, gates = _inproj(x, sh1, sc1, p["norm1_g"], p["q_norm_g"], p["k_norm_g"], p["w_in"], rope)
        wp = _seq_fft(u)
        attn = _attention(qkv, p["sink"])
        x1, h2, route, counts = _post(wp, attn, gates, x, g1, sh2, sc2, p["norm2_g"], p["cc"], p["cs"],
                                      p["w_fourier_out"], p["w_attn_out"], p["w_out"],
                                      p["w_router"], p["b_router"], counts)
        per_group.append((x1, route, g2, tok_off))
        h2_groups.append(h2)
        tok_off += b * s
    route_keys = jnp.concatenate([r[:, 0:TOP_K] for _, r, _, _ in per_group], axis=0)
    dest, pad_start, pad_end, block_pos, expert_seq, n_used, n_rows = _dispatch_plan(route_keys, counts, t_all)
    xs_sorted = _sortrows(h2_groups, dest, pad_start, pad_end, n_rows)
    y = _experts(xs_sorted, block_pos, expert_seq, n_used, p["w1"], p["w3"], p["w2"])
    outs = []
    for x, (x1, route, g2, off) in zip(xs, per_group):
        b, s, _ = x.shape
        dest_g = lax.slice(dest, (off * TOP_K,), ((off + b * s) * TOP_K,))
        outs.append(_combine(y, dest_g, x1, route, g2, s).reshape(b, s, d))
    return outs


def _channel_dft():
    n = FOURIER_GROUP_DIM
    k = np.arange(n, dtype=np.int64)
    ang = ((k[:, None] * k[None, :]) % n) * (2.0 * math.pi / n)
    return (jnp.asarray((np.cos(ang) * n ** -0.5).astype(BF16)),
            jnp.asarray((np.sin(ang) * n ** -0.5).astype(BF16)))


def kernel(x_prompt, x_sample, c_prompt, c_sample, w_ada, b_ada, norm1_g, w_in, q_norm_g, k_norm_g, sink,
           w_fourier_out, w_attn_out, w_out, norm2_g, w_group, b_group, w_expert, b_expert, w1, w3, w2):
    depth = w_ada.shape[0]
    d = x_prompt.shape[-1]
    bp = c_prompt.shape[0]
    bs = c_sample.shape[0]
    cc, cs = _channel_dft()
    xp, xs = x_prompt, x_sample
    for l in range(depth):
        c_all = jnp.concatenate([c_prompt, c_sample, jnp.zeros((8 - (bp + bs) % 8, d), F32)], axis=0)
        mod = _adaln(c_all, w_ada[l], b_ada[l])
        pad = ROUTE_LANES - N_GROUPS - N_EXPERTS
        w_router = jnp.concatenate([w_group[l], w_expert[l], jnp.zeros((d, pad), F32)], axis=1)
        w_router_hi = w_router.astype(BF16)
        p = {
            "norm1_g": norm1_g[l].reshape(1, d), "norm2_g": norm2_g[l].reshape(1, d),
            "q_norm_g": q_norm_g[l].reshape(1, HEAD_DIM), "k_norm_g": k_norm_g[l].reshape(1, HEAD_DIM),
            "sink": sink[l], "w_in": w_in[l].astype(BF16),
            "w_fourier_out": w_fourier_out[l].astype(BF16), "w_attn_out": w_attn_out[l].astype(BF16),
            "w_out": w_out[l].astype(BF16), "cc": cc, "cs": cs,
            "w_router": jnp.concatenate(
                [w_router_hi, (w_router - w_router_hi.astype(F32)).astype(BF16)], axis=1),
            "b_router": jnp.concatenate([b_group[l], b_expert[l], jnp.zeros((pad,), F32)]).reshape(1, ROUTE_LANES),
            "w1": w1[l], "w3": w3[l], "w2": w2[l],
        }
        xp, xs = _layer([xp, xs], [mod[:bp], mod[bp:bp + bs]], p)
    return xp, xs
```

```python
import functools
import math

import jax
import jax.numpy as jnp
import numpy as np
from jax import lax
from jax.experimental import pallas as pl
from jax.experimental.pallas import tpu as pltpu

F32 = jnp.float32
BF16 = jnp.bfloat16
I32 = jnp.int32

HEAD_DIM = 128
N_KV_HEADS = 2
Q_PER_KV = 4
N_HEADS = N_KV_HEADS * Q_PER_KV
ROT_DIM = 32
ROPE_THETA = 500000.0
WINDOW = 128
FOURIER_GROUPS = 4
FOURIER_GROUP_DIM = 256
N_GROUPS = 8
EXPERTS_PER_GROUP = 8
N_EXPERTS = N_GROUPS * EXPERTS_PER_GROUP
TOP_K = 2
EPS = 1e-6

LANES = 128
V7X_VMEM_LIMIT = 56 * 1024 * 1024

ROW_TILE = 256
ATTN_TILE = 1024
FFT_N1 = 128
FFT_UNROLL = 16
FFT_PITCH_PAD = 8
MOE_ROWS = 512
ROUTE_LANES = 128
RANK_BITS = 17
DMA_GROUP = 8
WEIGHT_CHUNKS = 8
DMA_QUEUES = 2
NEG = -1e30


def _cparams(sem, **kw):
    return pltpu.CompilerParams(dimension_semantics=sem, vmem_limit_bytes=V7X_VMEM_LIMIT, **kw)


def _resident(shape, index_map):
    return pl.BlockSpec(shape, index_map, pipeline_mode=pl.Buffered(1))


def _sigmoid(x):
    return 0.5 * jnp.tanh(0.5 * x) + 0.5


def _adaln_kernel(c_ref, w_ref, b_ref, o_ref):
    c = c_ref[...]
    s = c * _sigmoid(c)
    o_ref[...] = jnp.dot(s, w_ref[...], precision=lax.Precision.HIGHEST,
                         preferred_element_type=F32) + b_ref[...]


def _adaln(c, w_ada, b_ada):
    r, d = c.shape
    n = w_ada.shape[1]
    tn = 1024
    return pl.pallas_call(
        _adaln_kernel,
        grid=(n // tn,),
        in_specs=[pl.BlockSpec((r, d), lambda j: (0, 0)),
                  pl.BlockSpec((d, tn), lambda j: (0, j)),
                  pl.BlockSpec((1, tn), lambda j: (0, j))],
        out_specs=pl.BlockSpec((r, tn), lambda j: (0, j)),
        out_shape=jax.ShapeDtypeStruct((r, n), F32),
        compiler_params=_cparams(("parallel",)),
        name="adaln",
    )(c, w_ada, b_ada.reshape(1, n))


def _rope_tables(s):
    half = ROT_DIM // 2
    inv_freq = ROPE_THETA ** (-jnp.arange(0, ROT_DIM, 2, dtype=F32) / ROT_DIM)
    ang = jnp.arange(s, dtype=F32)[:, None] * inv_freq[None, :]
    cos, sin = jnp.cos(ang), jnp.sin(ang)
    pad = jnp.zeros((s, LANES - ROT_DIM), F32)
    cos_f = jnp.concatenate([cos, cos, jnp.ones((s, LANES - ROT_DIM), F32)], axis=1)
    sin_up = jnp.concatenate([-sin, jnp.zeros((s, half), F32), pad], axis=1)
    sin_dn = jnp.concatenate([jnp.zeros((s, half), F32), sin, pad], axis=1)
    return cos_f, sin_up, sin_dn


def _inproj_kernel(x_ref, sh_ref, sc_ref, g_ref, qg_ref, kg_ref, cos_ref, sup_ref, sdn_ref, w_ref,
                   u_ref, qkv_ref, gate_ref, *, fw, aw, kvw):
    x = x_ref[...]
    ms = jnp.mean(x * x, axis=-1, keepdims=True)
    h = (x * lax.rsqrt(ms + EPS) * g_ref[...]) * (1.0 + sc_ref[...]) + sh_ref[...]
    hb = h.astype(BF16)
    chunk = 512
    for c0 in range(0, fw, chunk):
        u_ref[:, c0:c0 + chunk] = jnp.dot(hb, w_ref[:, c0:c0 + chunk], preferred_element_type=F32)
    cos_f, sin_up, sin_dn = cos_ref[...], sup_ref[...], sdn_ref[...]
    half = ROT_DIM // 2
    scale = HEAD_DIM ** -0.5
    for c0 in list(range(0, aw, chunk)) + [aw]:
        width = chunk if c0 < aw else kvw
        acc = jnp.dot(hb, w_ref[:, fw + c0:fw + c0 + width], preferred_element_type=F32)
        for hh in range(width // HEAD_DIM):
            col = c0 + hh * HEAD_DIM
            t = acc[:, hh * HEAD_DIM:(hh + 1) * HEAD_DIM]
            is_q = col < aw
            gain = qg_ref[...] if is_q else kg_ref[...]
            t = t * lax.rsqrt(jnp.mean(t * t, axis=-1, keepdims=True) + EPS) * gain
            t = (t * cos_f + pltpu.roll(t, LANES - half, axis=1) * sin_up
                 + pltpu.roll(t, half, axis=1) * sin_dn)
            if is_q:
                t = t * scale
            qkv_ref[:, col:col + HEAD_DIM] = t.astype(BF16)
    v0 = fw + aw + kvw
    qkv_ref[:, aw + kvw:aw + 2 * kvw] = jnp.dot(
        hb, w_ref[:, v0:v0 + kvw], preferred_element_type=F32).astype(BF16)
    g0 = v0 + kvw
    gw = gate_ref.shape[-1]
    for c0 in range(0, gw, chunk):
        gate_ref[:, c0:c0 + chunk] = jnp.dot(
            hb, w_ref[:, g0 + c0:g0 + c0 + chunk], preferred_element_type=F32).astype(BF16)


def _inproj(x, shift, scale, norm_g, q_g, k_g, w_in_b, tables):
    b, s, d = x.shape
    fw = FOURIER_GROUPS * FOURIER_GROUP_DIM
    aw = N_HEADS * HEAD_DIM
    kvw = N_KV_HEADS * HEAD_DIM
    gw = 2 * d
    tm = ROW_TILE
    row = lambda w: pl.BlockSpec((None, tm, w), lambda bi, i: (bi, i, 0))
    per_b = pl.BlockSpec((None, 1, d), lambda bi, i: (bi, 0, 0))
    const = lambda w: pl.BlockSpec((1, w), lambda bi, i: (0, 0))
    tab = pl.BlockSpec((tm, LANES), lambda bi, i: (i, 0))
    return pl.pallas_call(
        functools.partial(_inproj_kernel, fw=fw, aw=aw, kvw=kvw),
        grid=(b, s // tm),
        in_specs=[row(d), per_b, per_b, const(d), const(HEAD_DIM), const(HEAD_DIM), tab, tab, tab,
                  _resident(w_in_b.shape, lambda bi, i: (0, 0))],
        out_specs=[row(fw), row(aw + 2 * kvw), row(gw)],
        out_shape=[jax.ShapeDtypeStruct((b, s, fw), F32),
                   jax.ShapeDtypeStruct((b, s, aw + 2 * kvw), BF16),
                   jax.ShapeDtypeStruct((b, s, gw), BF16)],
        compiler_params=_cparams(("parallel", "parallel")),
        name="inproj",
    )(x, shift, scale, norm_g, q_g, k_g, *tables, w_in_b)


def _pack_pair(a, b):
    ab = lax.bitcast_convert_type(a.astype(BF16).astype(F32), I32)
    bb = lax.bitcast_convert_type(b.astype(BF16).astype(F32), I32)
    return ab | lax.shift_right_logical(bb, jnp.full(bb.shape, 16, I32))


def _unpack_pair(p):
    hi = lax.bitcast_convert_type(p & jnp.int32(-65536), F32)
    lo = lax.bitcast_convert_type(lax.shift_left(p, jnp.full(p.shape, 16, I32)), F32)
    return hi, lo


def _fft_tables(s):
    n1 = FFT_N1
    n2 = s // n1
    k1 = np.arange(n1, dtype=np.int64)
    nn = (n2 * np.arange(n1, dtype=np.int64))[None, None, :] + np.arange(n2, dtype=np.int64)[:, None, None]
    ang = ((k1[None, :, None] * nn) % s) * (2.0 * math.pi / s)
    sc1 = n1 ** -0.5
    g = np.concatenate([np.cos(ang) * sc1, -np.sin(ang) * sc1], axis=1).astype(BF16)
    k2 = np.arange(n2, dtype=np.int64)
    ang2 = ((k2[:, None] * k2[None, :]) % n2) * (2.0 * math.pi / n2)
    c2, s2 = np.cos(ang2) * n2 ** -0.5, np.sin(ang2) * n2 ** -0.5
    f2 = np.concatenate([np.concatenate([c2, s2], axis=1),
                         np.concatenate([-s2, c2], axis=1)], axis=0).astype(BF16)
    return jnp.asarray(g), jnp.asarray(f2)


def _fft_kernel(u_ref, g_ref, f2_ref, o_ref, y_scr, *, n1, n2, unroll):
    pitch = n1 + FFT_PITCH_PAD

    def stage1(i, carry):
        for uu in range(unroll):
            m = i * unroll + uu
            xm = u_ref[pl.ds(m, n1, stride=n2), :].astype(BF16)
            y = jnp.dot(g_ref[m], xm, preferred_element_type=F32)
            y_scr[pl.ds(pl.multiple_of(m * pitch, 8), n1), :] = _pack_pair(y[:n1], y[n1:])
        return carry

    lax.fori_loop(0, n2 // unroll, stage1, 0)
    f2 = f2_ref[...]

    def stage2(i, carry):
        for uu in range(unroll):
            k1 = i * unroll + uu
            yr, yi = _unpack_pair(y_scr[pl.ds(k1, n2, stride=pitch), :])
            rhs = jnp.concatenate([yr.astype(BF16), yi.astype(BF16)], axis=0)
            z = jnp.dot(f2, rhs, preferred_element_type=F32)
            y_scr[pl.ds(k1, n2, stride=pitch), :] = _pack_pair(z[:n2], z[n2:])
        return carry

    lax.fori_loop(0, n1 // unroll, stage2, 0)

    def compact(i, carry):
        for uu in range(unroll):
            k2 = i * unroll + uu
            o_ref[pl.ds(pl.multiple_of(k2 * n1, n1), n1), :] = y_scr[pl.ds(pl.multiple_of(k2 * pitch, 8), n1), :]
        return carry

    lax.fori_loop(0, n2 // unroll, compact, 0)


def _seq_fft(u):
    b, s, c = u.shape
    n1 = FFT_N1
    n2 = s // n1
    g, f2 = _fft_tables(s)
    unroll = min(FFT_UNROLL, n2)
    blk = pl.BlockSpec((None, s, LANES), lambda bi, j: (bi, 0, j))
    return pl.pallas_call(
        functools.partial(_fft_kernel, n1=n1, n2=n2, unroll=unroll),
        grid=(b, c // LANES),
        in_specs=[blk, _resident(g.shape, lambda bi, j: (0, 0, 0)), _resident(f2.shape, lambda bi, j: (0, 0))],
        out_specs=blk,
        out_shape=jax.ShapeDtypeStruct((b, s, c), I32),
        scratch_shapes=[pltpu.VMEM((n2 * (n1 + FFT_PITCH_PAD), LANES), I32)],
        compiler_params=_cparams(("parallel", "parallel")),
        name="seq_fft",
    )(u, g, f2)


def _attn_kernel(sink_ref, q_ref, kp_ref, kc_ref, kn_ref, vp_ref, vc_ref, vn_ref, o_ref,
                 s_scr, p_scr, inv_scr, *, tq):
    i = pl.program_id(1)
    first = i == 0
    last = i == pl.num_programs(1) - 1
    nsub = tq // WINDOW
    nq = Q_PER_KV * WINDOW
    nk = 3 * WINDOW
    c = lax.broadcasted_iota(I32, (nk, nq), 0)
    r = lax.broadcasted_iota(I32, (nk, nq), 1) & (WINDOW - 1)
    band = ((c >= r) & (c < WINDOW)) | ((c >= WINDOW) & (c < 2 * WINDOW)) | ((c >= 2 * WINDOW) & (c - 2 * WINDOW <= r))
    pairs = [(h, sb) for h in range(N_KV_HEADS) for sb in range(nsub)]
    kcat, vt = [], []
    for h in range(N_KV_HEADS):
        hs = slice(h * HEAD_DIM, (h + 1) * HEAD_DIM)
        kcat.append(jnp.concatenate([kp_ref[:, hs], kc_ref[:, hs], kn_ref[:, hs]], axis=0))
        vcat = jnp.concatenate([vp_ref[:, hs], vc_ref[:, hs], vn_ref[:, hs]], axis=0)
        vt.append(vcat.astype(F32).T.astype(BF16))

    for n, (h, sb) in enumerate(pairs):
        q4 = jnp.concatenate(
            [q_ref[sb * WINDOW:(sb + 1) * WINDOW,
                   (h * Q_PER_KV + g) * HEAD_DIM:(h * Q_PER_KV + g + 1) * HEAD_DIM]
             for g in range(Q_PER_KV)], axis=0)
        kw = kcat[h][sb * WINDOW:(sb + 3) * WINDOW]
        s_scr[n] = lax.dot_general(kw, q4, (((1,), (1,)), ((), ())), preferred_element_type=F32)

    for n, (h, sb) in enumerate(pairs):
        sc = jnp.where(band, s_scr[n], NEG)
        if sb == 0:
            sc = jnp.where((c < WINDOW) & first, NEG, sc)
        if sb == nsub - 1:
            sc = jnp.where((c >= 2 * WINDOW) & last, NEG, sc)
        sink = jnp.concatenate(
            [jnp.full((1, WINDOW), sink_ref[h * Q_PER_KV + g], F32) for g in range(Q_PER_KV)], axis=1)
        m = jnp.maximum(jnp.max(sc, axis=0, keepdims=True), sink)
        p = jnp.exp(sc - m)
        denom = jnp.sum(p, axis=0, keepdims=True) + jnp.exp(sink - m)
        p_scr[n] = p.astype(BF16)
        inv_scr[pl.ds(n, 1), :] = 1.0 / denom

    for n, (h, sb) in enumerate(pairs):
        ot = jnp.dot(vt[h][:, sb * WINDOW:(sb + 3) * WINDOW], p_scr[n], preferred_element_type=F32)
        o = (ot * inv_scr[pl.ds(n, 1), :]).T
        for g in range(Q_PER_KV):
            col = (h * Q_PER_KV + g) * HEAD_DIM
            o_ref[sb * WINDOW:(sb + 1) * WINDOW, col:col + HEAD_DIM] = (
                o[g * WINDOW:(g + 1) * WINDOW].astype(BF16))


def _attention(qkv, sink):
    b, s, _ = qkv.shape
    aw = N_HEADS * HEAD_DIM
    kvw = N_KV_HEADS * HEAD_DIM
    tq = min(ATTN_TILE, s)
    per = tq // WINDOW
    nblk = s // WINDOW
    kcol = aw // kvw
    vcol = kcol + 1
    halo = lambda col, off: pl.BlockSpec(
        (None, WINDOW, kvw),
        lambda bi, i, sk: (bi, jnp.clip(i * per + off, 0, nblk - 1), col))
    main = lambda col: pl.BlockSpec((None, tq, kvw), lambda bi, i, sk: (bi, i, col))
    return pl.pallas_call(
        functools.partial(_attn_kernel, tq=tq),
        grid_spec=pltpu.PrefetchScalarGridSpec(
            num_scalar_prefetch=1,
            grid=(b, s // tq),
            in_specs=[pl.BlockSpec((None, tq, aw), lambda bi, i, sk: (bi, i, 0)),
                      halo(kcol, -1), main(kcol), halo(kcol, per),
                      halo(vcol, -1), main(vcol), halo(vcol, per)],
            out_specs=pl.BlockSpec((None, tq, aw), lambda bi, i, sk: (bi, i, 0)),
            scratch_shapes=[pltpu.VMEM((N_KV_HEADS * per, 3 * WINDOW, Q_PER_KV * WINDOW), F32),
                            pltpu.VMEM((N_KV_HEADS * per, 3 * WINDOW, Q_PER_KV * WINDOW), BF16),
                            pltpu.VMEM((N_KV_HEADS * per, Q_PER_KV * WINDOW), F32)]),
        out_shape=jax.ShapeDtypeStruct((b, s, aw), BF16),
        compiler_params=_cparams(("parallel", "parallel")),
        name="attention",
    )(sink, qkv, qkv, qkv, qkv, qkv, qkv, qkv)


ROW_WORDS = 1024
ROW_SUB = ROW_WORDS // LANES


def _store_tile_rows(ref, val, base=0, lead=()):
    r = val.shape[0]
    for j in range(ROW_SUB):
        ref[lead + (pl.ds(base * ROW_SUB + j, r, stride=ROW_SUB), slice(None))] = val[:, j * LANES:(j + 1) * LANES]


def _load_tile_rows(ref, base, r, lead=()):
    return jnp.concatenate(
        [ref[lead + (pl.ds(base * ROW_SUB + j, r, stride=ROW_SUB), slice(None))] for j in range(ROW_SUB)], axis=1)


def _tile_row(ref, row):
    return ref.at[pl.ds(pl.multiple_of(row * ROW_SUB, ROW_SUB), ROW_SUB), :]


def _post_kernel(wp_ref, o_ref, gate_ref, x_ref, g1_ref, sh2_ref, sc2_ref, n2g_ref, cc_ref, cs_ref,
                 wfo_ref, wao_ref, wout_ref, wr_ref, br_ref, cnt_in_ref,
                 x1_ref, h2_ref, route_ref, cnt_ref):
    tm = x_ref.shape[0]
    d = x_ref.shape[1]

    @pl.when(pl.program_id(0) == 0)
    def _():
        cnt_ref[...] = cnt_in_ref[...]

    re, im = _unpack_pair(wp_ref[...])
    re, im = re.astype(BF16), im.astype(BF16)
    cc, cs = cc_ref[...], cs_ref[...]
    gd = FOURIER_GROUP_DIM
    fm = jnp.concatenate(
        [(jnp.dot(re[:, g * gd:(g + 1) * gd], cc, preferred_element_type=F32)
          + jnp.dot(im[:, g * gd:(g + 1) * gd], cs, preferred_element_type=F32)).astype(BF16)
         for g in range(FOURIER_GROUPS)], axis=1)
    y_f = jnp.dot(fm, wfo_ref[...], preferred_element_type=F32)
    y_a = jnp.dot(o_ref[...], wao_ref[...], preferred_element_type=F32)
    merged = (_sigmoid(gate_ref[:, :d].astype(F32)) * y_f
              + _sigmoid(gate_ref[:, d:].astype(F32)) * y_a).astype(BF16)
    x1 = x_ref[...] + g1_ref[...] * jnp.dot(merged, wout_ref[...], preferred_element_type=F32)
    x1_ref[...] = x1
    ms = jnp.mean(x1 * x1, axis=-1, keepdims=True)
    h2 = (x1 * lax.rsqrt(ms + EPS) * n2g_ref[...]) * (1.0 + sc2_ref[...]) + sh2_ref[...]
    half = d // 2
    _store_tile_rows(h2_ref, _pack_pair(h2[:, :half], h2[:, half:]))

    h_hi = h2.astype(BF16)
    h_lo = (h2 - h_hi.astype(F32)).astype(BF16)
    wr = wr_ref[...]
    r_hi = jnp.dot(h_hi, wr, preferred_element_type=F32)
    r_lo = jnp.dot(h_lo, wr, preferred_element_type=F32)
    logits = ((r_hi[:, :ROUTE_LANES] + r_hi[:, ROUTE_LANES:])
              + (r_lo[:, :ROUTE_LANES] + r_lo[:, ROUTE_LANES:])) + br_ref[...]
    lane = lax.broadcasted_iota(I32, (tm, ROUTE_LANES), 1)
    big = jnp.int32(ROUTE_LANES)
    is_g = lane < N_GROUPS
    gl = jnp.where(is_g, logits, NEG)
    gmax = jnp.max(gl, axis=-1, keepdims=True)
    gidx = jnp.min(jnp.where(gl == gmax, lane, big), axis=-1, keepdims=True)
    p_g = 1.0 / jnp.sum(jnp.where(is_g, jnp.exp(gl - gmax), 0.0), axis=-1, keepdims=True)
    in_grp = (lane >= N_GROUPS) & (lane < N_GROUPS + N_EXPERTS) & (
        lax.shift_right_logical(lane - N_GROUPS, jnp.full(lane.shape, 3, I32)) == gidx)
    el = jnp.where(in_grp, logits, NEG)
    e1v = jnp.max(el, axis=-1, keepdims=True)
    e1i = jnp.min(jnp.where(el == e1v, lane, big), axis=-1, keepdims=True)
    el2 = jnp.where(lane == e1i, NEG, el)
    e2v = jnp.max(el2, axis=-1, keepdims=True)
    e2i = jnp.min(jnp.where(el2 == e2v, lane, big), axis=-1, keepdims=True)
    t = jnp.exp(e2v - e1v)
    w1 = p_g / (1.0 + t)
    w2 = w1 * t
    sel1, sel2 = lane == e1i, lane == e2i
    member = jnp.where(sel1 | sel2, 1.0, 0.0)
    rr = lax.broadcasted_iota(I32, (tm, tm), 0)
    cc_i = lax.broadcasted_iota(I32, (tm, tm), 1)
    tri = jnp.where(cc_i < rr, 1.0, 0.0).astype(BF16)
    prefix = jnp.dot(tri, member.astype(BF16), preferred_element_type=F32) + cnt_ref[...]
    rank1 = jnp.sum(jnp.where(sel1, prefix, 0.0), axis=-1, keepdims=True)
    rank2 = jnp.sum(jnp.where(sel2, prefix, 0.0), axis=-1, keepdims=True)
    cnt_ref[...] = cnt_ref[...] + jnp.sum(member, axis=0, keepdims=True)
    key_scale = float(1 << RANK_BITS)
    key1 = (e1i - N_GROUPS).astype(F32) * key_scale + rank1
    key2 = (e2i - N_GROUPS).astype(F32) * key_scale + rank2
    route = jnp.zeros((tm, ROUTE_LANES), F32)
    for k, val in enumerate((key1, key2, w1, w2)):
        route = jnp.where(lane == k, val, route)
    route_ref[...] = route


def _post(wp, attn, gates, x, g1, sh2, sc2, norm2_g, cc, cs, wfo_b, wao_b, wout_b, wr, br,
          counts_in):
    b, s, d = x.shape
    t = b * s
    tm = ROW_TILE
    per_b_tiles = s // tm
    flat = lambda a: a.reshape(t, a.shape[-1])
    row = lambda w: pl.BlockSpec((tm, w), lambda i: (i, 0))
    per_b = pl.BlockSpec((None, 1, d), lambda i: (i // per_b_tiles, 0, 0))
    const = lambda a: _resident(a.shape, lambda i: (0,) * a.ndim)
    fw = wp.shape[-1]
    aw = attn.shape[-1]
    return pl.pallas_call(
        _post_kernel,
        grid=(t // tm,),
        in_specs=[row(fw), row(aw), row(2 * d), row(d), per_b, per_b, per_b,
                  const(norm2_g), const(cc), const(cs), const(wfo_b), const(wao_b), const(wout_b),
                  const(wr), const(br), const(counts_in)],
        out_specs=[row(d), pl.BlockSpec((tm * ROW_SUB, LANES), lambda i: (i, 0)), row(ROUTE_LANES),
                   pl.BlockSpec((1, ROUTE_LANES), lambda i: (0, 0))],
        out_shape=[jax.ShapeDtypeStruct((t, d), F32), jax.ShapeDtypeStruct((t * ROW_SUB, LANES), I32),
                   jax.ShapeDtypeStruct((t, ROUTE_LANES), F32), jax.ShapeDtypeStruct((1, ROUTE_LANES), F32)],
        compiler_params=_cparams(("arbitrary",)),
        name="post_router",
    )(flat(wp), flat(attn), flat(gates), flat(x), g1, sh2, sc2, norm2_g, cc, cs, wfo_b, wao_b, wout_b, wr, br,
      counts_in)


def _sortrows_kernel(dest_ref, pstart_ref, pend_ref, *refs, tm, rows, tile_starts):
    n_groups = len(tile_starts)
    h_refs = refs[:n_groups]
    xs_hbm, stage, zbuf, sem, zsem = refs[n_groups:]
    j = pl.program_id(0)
    n = pl.num_programs(0)
    slot = j % 2
    blk_sub = rows * ROW_SUB

    def wait_slot(sl):
        nsub = TOP_K * tm * ROW_SUB
        pltpu.make_async_copy(stage.at[0, pl.ds(0, nsub), :], xs_hbm.at[pl.ds(0, nsub), :], sem.at[sl]).wait()

    @pl.when(j == 0)
    def _():
        zbuf[...] = jnp.zeros_like(zbuf)

        def zero_block(off):
            return pltpu.make_async_copy(zbuf, xs_hbm.at[pl.ds(pl.multiple_of(off * ROW_SUB, blk_sub), blk_sub), :],
                                         zsem)

        def per_expert(fn):
            def body(e, carry):
                @pl.when(pend_ref[e] > pstart_ref[e])
                def _():
                    fn(zero_block(pend_ref[e] - rows))
                return carry
            lax.fori_loop(0, N_EXPERTS, body, 0)

        n_used = pend_ref[N_EXPERTS - 1] // rows
        n_blocks = xs_hbm.shape[0] // blk_sub

        def tail(fn):
            lax.fori_loop(n_used, n_blocks, lambda b, c: (fn(zero_block(b * rows)), c)[1], 0)

        per_expert(lambda cp: cp.start())
        tail(lambda cp: cp.start())
        per_expert(lambda cp: cp.wait())
        tail(lambda cp: cp.wait())

    @pl.when(j >= 2)
    def _():
        wait_slot(slot)

    tile = h_refs[0][...]
    for g in range(1, n_groups):
        tile = jnp.where(j >= tile_starts[g], h_refs[g][...], tile)
    stage[slot, pl.ds(0, tm * ROW_SUB), :] = tile

    def body(gi, carry):
        r0 = gi * DMA_GROUP
        base = (j * tm + r0) * TOP_K
        dests = [dest_ref[base + q] for q in range(DMA_GROUP * TOP_K)]
        for q, dest in enumerate(dests):
            pltpu.make_async_copy(_tile_row(stage.at[slot], r0 + q // TOP_K), _tile_row(xs_hbm, dest),
                                  sem.at[slot]).start(priority=q % DMA_QUEUES)
        return carry

    lax.fori_loop(0, tm // DMA_GROUP, body, 0)

    @pl.when(j == n - 1)
    def _():
        @pl.when(j >= 1)
        def _():
            wait_slot(1 - slot)

        wait_slot(slot)


def _sortrows(h2_groups, dest, pad_start, pad_end, n_rows):
    tm = ROW_TILE
    tiles = [h.shape[0] // (ROW_SUB * tm) for h in h2_groups]
    starts = [sum(tiles[:g]) for g in range(len(tiles))]
    spec = lambda g: pl.BlockSpec((tm * ROW_SUB, LANES),
                                  lambda j, *_: (jnp.clip(j - starts[g], 0, tiles[g] - 1), 0))
    return pl.pallas_call(
        functools.partial(_sortrows_kernel, tm=tm, rows=MOE_ROWS, tile_starts=tuple(starts)),
        grid_spec=pltpu.PrefetchScalarGridSpec(
            num_scalar_prefetch=3,
            grid=(sum(tiles),),
            in_specs=[spec(g) for g in range(len(tiles))],
            out_specs=pl.BlockSpec(memory_space=pl.ANY),
            scratch_shapes=[pltpu.VMEM((2, TOP_K * tm * ROW_SUB, LANES), I32),
                            pltpu.VMEM((MOE_ROWS * ROW_SUB, LANES), I32),
                            pltpu.SemaphoreType.DMA((2,)), pltpu.SemaphoreType.DMA(())]),
        out_shape=jax.ShapeDtypeStruct((n_rows * ROW_SUB, LANES), I32),
        compiler_params=_cparams(("arbitrary",), disable_bounds_checks=True, has_side_effects=True),
        name="sort_rows",
    )(dest, pad_start, pad_end, *h2_groups)


def _expert_kernel(pos_ref, seq_ref, nused_ref, xs_ref, w1_hbm, w3_hbm, w2_hbm, y_ref,
                   w1f, w3f, w2f, wsem, w1b, w3b, w2b, *, rows):
    j = pl.program_id(0)
    weights = ((w1_hbm, w1f), (w3_hbm, w3f), (w2_hbm, w2f))

    def start_weights(k, buf):
        e = seq_ref[k]
        for src, dst in weights:
            n = src.shape[1] // WEIGHT_CHUNKS
            for c in range(WEIGHT_CHUNKS):
                pltpu.make_async_copy(src.at[e, pl.ds(c * n, n), :], dst.at[buf, pl.ds(c * n, n), :],
                                      wsem.at[buf]).start()

    def wait_weights(k, buf):
        e = seq_ref[k]
        for src, dst in weights:
            pltpu.make_async_copy(src.at[e], dst.at[buf], wsem.at[buf]).wait()

    @pl.when(j == 0)
    def _():
        start_weights(0, 0)

    pos = pos_ref[j]
    pos_prev = pos_ref[jnp.maximum(j - 1, 0)]

    @pl.when((j == 0) | (pos != pos_prev))
    def _():
        buf = pos % 2
        wait_weights(pos, buf)

        @pl.when(pos + 1 < nused_ref[1])
        def _():
            start_weights(pos + 1, 1 - buf)

        w1b[...] = w1f[buf].astype(BF16)
        w3b[...] = w3f[buf].astype(BF16)
        w2b[...] = w2f[buf].astype(BF16)

    @pl.when(j < nused_ref[0])
    def _():
        hi, lo = _unpack_pair(_load_tile_rows(xs_ref, 0, rows))
        xb = jnp.concatenate([hi.astype(BF16), lo.astype(BF16)], axis=1)
        a = jnp.dot(xb, w1b[...], preferred_element_type=F32)
        g = jnp.dot(xb, w3b[...], preferred_element_type=F32)
        hid = (a * _sigmoid(a) * g).astype(BF16)
        y = jnp.dot(hid, w2b[...], preferred_element_type=F32)
        half = y.shape[1] // 2
        _store_tile_rows(y_ref, _pack_pair(y[:, :half], y[:, half:]))

    @pl.when(j >= nused_ref[0])
    def _():
        y_ref[...] = jnp.zeros_like(y_ref)


def _experts(xs, block_pos, expert_seq, n_used, w1, w3, w2):
    rows = MOE_ROWS
    blk = rows * ROW_SUB
    d, f = w1.shape[1], w1.shape[2]
    hbm = pl.BlockSpec(memory_space=pl.ANY)
    return pl.pallas_call(
        functools.partial(_expert_kernel, rows=rows),
        grid_spec=pltpu.PrefetchScalarGridSpec(
            num_scalar_prefetch=3,
            grid=(xs.shape[0] // blk,),
            in_specs=[pl.BlockSpec((blk, LANES), lambda j, bp, es, nu: (jnp.minimum(j, nu[0] - 1), 0)),
                      hbm, hbm, hbm],
            out_specs=pl.BlockSpec((blk, LANES), lambda j, *_: (j, 0)),
            scratch_shapes=[pltpu.VMEM((2, d, f), F32), pltpu.VMEM((2, d, f), F32), pltpu.VMEM((2, f, d), F32),
                            pltpu.SemaphoreType.DMA((2,)),
                            pltpu.VMEM((d, f), BF16), pltpu.VMEM((d, f), BF16), pltpu.VMEM((f, d), BF16)]),
        out_shape=jax.ShapeDtypeStruct(xs.shape, I32),
        compiler_params=_cparams(("arbitrary",)),
        name="experts",
    )(block_pos, expert_seq, n_used, xs, w1, w3, w2)


def _combine_kernel(dest_ref, y_hbm, x1_ref, route_ref, g2_ref, o_ref, ybuf, sem, *, tm):
    i = pl.program_id(0)
    n = pl.num_programs(0)
    slot = i % 2

    def start_gather(tile, sl):
        def body(gi, carry):
            r0 = gi * DMA_GROUP
            base = (tile * tm + r0) * TOP_K
            dests = [dest_ref[base + q] for q in range(DMA_GROUP * TOP_K)]
            for q, dest in enumerate(dests):
                pltpu.make_async_copy(_tile_row(y_hbm, dest),
                                      _tile_row(ybuf.at[sl], (q % TOP_K) * tm + r0 + q // TOP_K),
                                      sem.at[sl]).start(priority=q % DMA_QUEUES)
            return carry

        lax.fori_loop(0, tm // DMA_GROUP, body, 0)

    @pl.when(i == 0)
    def _():
        start_gather(0, 0)

    @pl.when(i + 1 < n)
    def _():
        start_gather(i + 1, 1 - slot)

    pltpu.make_async_copy(y_hbm.at[pl.ds(0, TOP_K * tm * ROW_SUB), :], ybuf.at[slot], sem.at[slot]).wait()
    route = route_ref[...]
    w1 = route[:, 2:3]
    w2 = route[:, 3:4]
    hi1, lo1 = _unpack_pair(_load_tile_rows(ybuf, 0, tm, lead=(slot,)))
    hi2, lo2 = _unpack_pair(_load_tile_rows(ybuf, tm, tm, lead=(slot,)))
    half = o_ref.shape[1] // 2
    o_ref[:, :half] = x1_ref[:, :half] + g2_ref[:, :half] * (w1 * hi1 + w2 * hi2)
    o_ref[:, half:] = x1_ref[:, half:] + g2_ref[:, half:] * (w1 * lo1 + w2 * lo2)


def _combine(y, dest, x1, route, g2, s):
    t, d = x1.shape
    tm = ROW_TILE
    per_b_tiles = s // tm
    return pl.pallas_call(
        functools.partial(_combine_kernel, tm=tm),
        grid_spec=pltpu.PrefetchScalarGridSpec(
            num_scalar_prefetch=1,
            grid=(t // tm,),
            in_specs=[pl.BlockSpec(memory_space=pl.ANY),
                      pl.BlockSpec((tm, d), lambda i, ds: (i, 0)),
                      pl.BlockSpec((tm, ROUTE_LANES), lambda i, ds: (i, 0)),
                      pl.BlockSpec((None, 1, d), lambda i, ds: (i // per_b_tiles, 0, 0))],
            out_specs=pl.BlockSpec((tm, d), lambda i, ds: (i, 0)),
            scratch_shapes=[pltpu.VMEM((2, TOP_K * tm * ROW_SUB, LANES), I32), pltpu.SemaphoreType.DMA((2,))]),
        out_shape=jax.ShapeDtypeStruct((t, d), F32),
        compiler_params=_cparams(("arbitrary",), disable_bounds_checks=True),
        name="combine",
    )(dest, y, x1, route, g2)


def _dispatch_plan(route_keys, counts, t):
    rows = MOE_ROWS
    keys = route_keys.astype(I32)
    expert = lax.shift_right_logical(keys, jnp.full(keys.shape, RANK_BITS, I32))
    rank = keys & ((1 << RANK_BITS) - 1)
    cnt = counts[0, N_GROUPS:N_GROUPS + N_EXPERTS].astype(I32)
    padded = (cnt + rows - 1) // rows * rows
    pad_end = jnp.cumsum(padded)
    pad_start = pad_end - padded
    onehot = expert[..., None] == jnp.arange(N_EXPERTS, dtype=I32)
    dest = (jnp.sum(jnp.where(onehot, pad_start, 0), axis=-1) + rank).reshape(t * TOP_K)
    n_blocks = (t * TOP_K + N_EXPERTS * (rows - 1) + rows - 1) // rows
    n_used = pad_end[-1:] // rows
    blk = jnp.minimum(jnp.arange(n_blocks, dtype=I32), n_used - 1) * rows
    block_e = jnp.sum((pad_end[None, :] <= blk[:, None]).astype(I32), axis=1)
    owns = cnt > 0
    expert_seq = jnp.argsort(jnp.logical_not(owns), stable=True).astype(I32)
    seq_index = jnp.cumsum(owns.astype(I32)) - 1
    block_pos = jnp.sum(jnp.where(block_e[:, None] == jnp.arange(N_EXPERTS, dtype=I32), seq_index, 0), axis=1)
    used = jnp.concatenate([n_used, jnp.sum(owns.astype(I32), keepdims=True)]).astype(I32)
    return dest, pad_start, pad_end, block_pos.astype(I32), expert_seq, used, n_blocks * rows


def _layer(xs, mods, p):
    d = xs[0].shape[-1]
    t_all = sum(x.shape[0] * x.shape[1] for x in xs)
    counts = jnp.zeros((1, ROUTE_LANES), F32)
    rope = _rope_tables(max(x.shape[1] for x in xs))
    per_group, h2_groups, tok_off = [], [], 0
    for x, mod in zip(xs, mods):
        b, s, _ = x.shape
        sh1, sc1, g1, sh2, sc2, g2 = [m.reshape(b, 1, d) for m in jnp.split(mod, 6, axis=-1)]
        u, qkv, gates = _inproj(x, sh1, sc1, p["norm1_g"], p["q_norm_g"], p["k_norm_g"], p["w_in"], rope)
        wp = _seq_fft(u)
        attn = _attention(qkv, p["sink"])
        x1, h2, route, counts = _post(wp, attn, gates, x, g1, sh2, sc2, p["norm2_g"], p["cc"], p["cs"],
                                      p["w_fourier_out"], p["w_attn_out"], p["w_out"],
                                      p["w_router"], p["b_router"], counts)
        per_group.append((x1, route, g2, tok_off))
        h2_groups.append(h2)
        tok_off += b * s
    route_keys = jnp.concatenate([r[:, 0:TOP_K] for _, r, _, _ in per_group], axis=0)
    dest, pad_start, pad_end, block_pos, expert_seq, n_used, n_rows = _dispatch_plan(route_keys, counts, t_all)
    xs_sorted = _sortrows(h2_groups, dest, pad_start, pad_end, n_rows)
    y = _experts(xs_sorted, block_pos, expert_seq, n_used, p["w1"], p["w3"], p["w2"])
    outs = []
    for x, (x1, route, g2, off) in zip(xs, per_group):
        b, s, _ = x.shape
        dest_g = lax.slice(dest, (off * TOP_K,), ((off + b * s) * TOP_K,))
        outs.append(_combine(y, dest_g, x1, route, g2, s).reshape(b, s, d))
    return outs


def _channel_dft():
    n = FOURIER_GROUP_DIM
    k = np.arange(n, dtype=np.int64)
    ang = ((k[:, None] * k[None, :]) % n) * (2.0 * math.pi / n)
    return (jnp.asarray((np.cos(ang) * n ** -0.5).astype(BF16)),
            jnp.asarray((np.sin(ang) * n ** -0.5).astype(BF16)))


def kernel(x_prompt, x_sample, c_prompt, c_sample, w_ada, b_ada, norm1_g, w_in, q_norm_g, k_norm_g, sink,
           w_fourier_out, w_attn_out, w_out, norm2_g, w_group, b_group, w_expert, b_expert, w1, w3, w2):
    depth = w_ada.shape[0]
    d = x_prompt.shape[-1]
    bp = c_prompt.shape[0]
    bs = c_sample.shape[0]
    cc, cs = _channel_dft()
    xp, xs = x_prompt, x_sample
    for l in range(depth):
        c_all = jnp.concatenate([c_prompt, c_sample, jnp.zeros((8 - (bp + bs) % 8, d), F32)], axis=0)
        mod = _adaln(c_all, w_ada[l], b_ada[l])
        pad = ROUTE_LANES - N_GROUPS - N_EXPERTS
        w_router = jnp.concatenate([w_group[l], w_expert[l], jnp.zeros((d, pad), F32)], axis=1)
        w_router_hi = w_router.astype(BF16)
        p = {
            "norm1_g": norm1_g[l].reshape(1, d), "norm2_g": norm2_g[l].reshape(1, d),
            "q_norm_g": q_norm_g[l].reshape(1, HEAD_DIM), "k_norm_g": k_norm_g[l].reshape(1, HEAD_DIM),
            "sink": sink[l], "w_in": w_in[l].astype(BF16),
            "w_fourier_out": w_fourier_out[l].astype(BF16), "w_attn_out": w_attn_out[l].astype(BF16),
            "w_out": w_out[l].astype(BF16), "cc": cc, "cs": cs,
            "w_router": jnp.concatenate(
                [w_router_hi, (w_router - w_router_hi.astype(F32)).astype(BF16)], axis=1),
            "b_router": jnp.concatenate([b_group[l], b_expert[l], jnp.zeros((pad,), F32)]).reshape(1, ROUTE_LANES),
            "w1": w1[l], "w3": w3[l], "w2": w2[l],
        }
        xp, xs = _layer([xp, xs], [mod[:bp], mod[bp:bp + bs]], p)
    return xp, xs
```

```python
import functools
import math

import jax
import jax.numpy as jnp
import numpy as np
from jax import lax
from jax.experimental import pallas as pl
from jax.experimental.pallas import tpu as pltpu

F32 = jnp.float32
BF16 = jnp.bfloat16
I32 = jnp.int32

HEAD_DIM = 128
N_KV_HEADS = 2
Q_PER_KV = 4
N_HEADS = N_KV_HEADS * Q_PER_KV
ROT_DIM = 32
ROPE_THETA = 500000.0
WINDOW = 128
FOURIER_GROUPS = 4
FOURIER_GROUP_DIM = 256
N_GROUPS = 8
EXPERTS_PER_GROUP = 8
N_EXPERTS = N_GROUPS * EXPERTS_PER_GROUP
TOP_K = 2
EPS = 1e-6

LANES = 128
V7X_VMEM_LIMIT = 56 * 1024 * 1024

ROW_TILE = 256
SHUFFLE_TILE = 512
ATTN_TILE = 1024
FFT_N1 = 128
FFT_UNROLL = 16
FFT_PITCH_PAD = 8
MOE_ROWS = 512
ROUTE_LANES = 128
RANK_BITS = 17
DMA_GROUP = 8
WEIGHT_CHUNKS = 8
DMA_QUEUES = 2
NEG = -1e30


def _cparams(sem, **kw):
    return pltpu.CompilerParams(dimension_semantics=sem, vmem_limit_bytes=V7X_VMEM_LIMIT, **kw)


def _resident(shape, index_map):
    return pl.BlockSpec(shape, index_map, pipeline_mode=pl.Buffered(1))


def _sigmoid(x):
    return 0.5 * jnp.tanh(0.5 * x) + 0.5


def _adaln_kernel(c_ref, w_ref, b_ref, o_ref):
    c = c_ref[...]
    s = c * _sigmoid(c)
    o_ref[...] = jnp.dot(s, w_ref[...], precision=lax.Precision.HIGHEST,
                         preferred_element_type=F32) + b_ref[...]


def _adaln(c, w_ada, b_ada):
    r, d = c.shape
    n = w_ada.shape[1]
    tn = 1024
    return pl.pallas_call(
        _adaln_kernel,
        grid=(n // tn,),
        in_specs=[pl.BlockSpec((r, d), lambda j: (0, 0)),
                  pl.BlockSpec((d, tn), lambda j: (0, j)),
                  pl.BlockSpec((1, tn), lambda j: (0, j))],
        out_specs=pl.BlockSpec((r, tn), lambda j: (0, j)),
        out_shape=jax.ShapeDtypeStruct((r, n), F32),
        compiler_params=_cparams(("parallel",)),
        name="adaln",
    )(c, w_ada, b_ada.reshape(1, n))


def _rope_tables(s):
    half = ROT_DIM // 2
    inv_freq = ROPE_THETA ** (-jnp.arange(0, ROT_DIM, 2, dtype=F32) / ROT_DIM)
    ang = jnp.arange(s, dtype=F32)[:, None] * inv_freq[None, :]
    cos, sin = jnp.cos(ang), jnp.sin(ang)
    pad = jnp.zeros((s, LANES - ROT_DIM), F32)
    cos_f = jnp.concatenate([cos, cos, jnp.ones((s, LANES - ROT_DIM), F32)], axis=1)
    sin_up = jnp.concatenate([-sin, jnp.zeros((s, half), F32), pad], axis=1)
    sin_dn = jnp.concatenate([jnp.zeros((s, half), F32), sin, pad], axis=1)
    return cos_f, sin_up, sin_dn


def _inproj_kernel(x_ref, sh_ref, sc_ref, g_ref, qg_ref, kg_ref, cos_ref, sup_ref, sdn_ref, w_ref,
                   u_ref, qkv_ref, gate_ref, *, fw, aw, kvw):
    x = x_ref[...]
    ms = jnp.mean(x * x, axis=-1, keepdims=True)
    h = (x * lax.rsqrt(ms + EPS) * g_ref[...]) * (1.0 + sc_ref[...]) + sh_ref[...]
    hb = h.astype(BF16)
    chunk = 512
    for c0 in range(0, fw, chunk):
        u_ref[:, c0:c0 + chunk] = jnp.dot(hb, w_ref[:, c0:c0 + chunk], preferred_element_type=F32)
    cos_f, sin_up, sin_dn = cos_ref[...], sup_ref[...], sdn_ref[...]
    half = ROT_DIM // 2
    scale = HEAD_DIM ** -0.5
    for c0 in list(range(0, aw, chunk)) + [aw]:
        width = chunk if c0 < aw else kvw
        acc = jnp.dot(hb, w_ref[:, fw + c0:fw + c0 + width], preferred_element_type=F32)
        for hh in range(width // HEAD_DIM):
            col = c0 + hh * HEAD_DIM
            t = acc[:, hh * HEAD_DIM:(hh + 1) * HEAD_DIM]
            is_q = col < aw
            gain = qg_ref[...] if is_q else kg_ref[...]
            t = t * lax.rsqrt(jnp.mean(t * t, axis=-1, keepdims=True) + EPS) * gain
            t = (t * cos_f + pltpu.roll(t, LANES - half, axis=1) * sin_up
                 + pltpu.roll(t, half, axis=1) * sin_dn)
            if is_q:
                t = t * scale
            qkv_ref[:, col:col + HEAD_DIM] = t.astype(BF16)
    v0 = fw + aw + kvw
    qkv_ref[:, aw + kvw:aw + 2 * kvw] = jnp.dot(
        hb, w_ref[:, v0:v0 + kvw], preferred_element_type=F32).astype(BF16)
    g0 = v0 + kvw
    gw = gate_ref.shape[-1]
    for c0 in range(0, gw, chunk):
        gate_ref[:, c0:c0 + chunk] = jnp.dot(
            hb, w_ref[:, g0 + c0:g0 + c0 + chunk], preferred_element_type=F32).astype(BF16)


def _inproj(x, shift, scale, norm_g, q_g, k_g, w_in_b, tables):
    b, s, d = x.shape
    fw = FOURIER_GROUPS * FOURIER_GROUP_DIM
    aw = N_HEADS * HEAD_DIM
    kvw = N_KV_HEADS * HEAD_DIM
    gw = 2 * d
    tm = ROW_TILE
    row = lambda w: pl.BlockSpec((None, tm, w), lambda bi, i: (bi, i, 0))
    per_b = pl.BlockSpec((None, 1, d), lambda bi, i: (bi, 0, 0))
    const = lambda w: pl.BlockSpec((1, w), lambda bi, i: (0, 0))
    tab = pl.BlockSpec((tm, LANES), lambda bi, i: (i, 0))
    return pl.pallas_call(
        functools.partial(_inproj_kernel, fw=fw, aw=aw, kvw=kvw),
        grid=(b, s // tm),
        in_specs=[row(d), per_b, per_b, const(d), const(HEAD_DIM), const(HEAD_DIM), tab, tab, tab,
                  _resident(w_in_b.shape, lambda bi, i: (0, 0))],
        out_specs=[row(fw), row(aw + 2 * kvw), row(gw)],
        out_shape=[jax.ShapeDtypeStruct((b, s, fw), F32),
                   jax.ShapeDtypeStruct((b, s, aw + 2 * kvw), BF16),
                   jax.ShapeDtypeStruct((b, s, gw), BF16)],
        compiler_params=_cparams(("parallel", "parallel")),
        name="inproj",
    )(x, shift, scale, norm_g, q_g, k_g, *tables, w_in_b)


def _pack_pair(a, b):
    ab = lax.bitcast_convert_type(a.astype(BF16).astype(F32), I32)
    bb = lax.bitcast_convert_type(b.astype(BF16).astype(F32), I32)
    return ab | lax.shift_right_logical(bb, jnp.full(bb.shape, 16, I32))


def _unpack_pair(p):
    hi = lax.bitcast_convert_type(p & jnp.int32(-65536), F32)
    lo = lax.bitcast_convert_type(lax.shift_left(p, jnp.full(p.shape, 16, I32)), F32)
    return hi, lo


def _fft_tables(s):
    n1 = FFT_N1
    n2 = s // n1
    k1 = np.arange(n1, dtype=np.int64)
    nn = (n2 * np.arange(n1, dtype=np.int64))[None, None, :] + np.arange(n2, dtype=np.int64)[:, None, None]
    ang = ((k1[None, :, None] * nn) % s) * (2.0 * math.pi / s)
    sc1 = n1 ** -0.5
    g = np.concatenate([np.cos(ang) * sc1, -np.sin(ang) * sc1], axis=1).astype(BF16)
    k2 = np.arange(n2, dtype=np.int64)
    ang2 = ((k2[:, None] * k2[None, :]) % n2) * (2.0 * math.pi / n2)
    c2, s2 = np.cos(ang2) * n2 ** -0.5, np.sin(ang2) * n2 ** -0.5
    f2 = np.concatenate([np.concatenate([c2, s2], axis=1),
                         np.concatenate([-s2, c2], axis=1)], axis=0).astype(BF16)
    return jnp.asarray(g), jnp.asarray(f2)


def _fft_kernel(u_ref, g_ref, f2_ref, o_ref, y_scr, *, n1, n2, unroll):
    pitch = n1 + FFT_PITCH_PAD

    def stage1(i, carry):
        for uu in range(unroll):
            m = i * unroll + uu
            xm = u_ref[pl.ds(m, n1, stride=n2), :].astype(BF16)
            y = jnp.dot(g_ref[m], xm, preferred_element_type=F32)
            y_scr[pl.ds(pl.multiple_of(m * pitch, 8), n1), :] = _pack_pair(y[:n1], y[n1:])
        return carry

    lax.fori_loop(0, n2 // unroll, stage1, 0)
    f2 = f2_ref[...]

    def stage2(i, carry):
        for uu in range(unroll):
            k1 = i * unroll + uu
            yr, yi = _unpack_pair(y_scr[pl.ds(k1, n2, stride=pitch), :])
            rhs = jnp.concatenate([yr.astype(BF16), yi.astype(BF16)], axis=0)
            z = jnp.dot(f2, rhs, preferred_element_type=F32)
            y_scr[pl.ds(k1, n2, stride=pitch), :] = _pack_pair(z[:n2], z[n2:])
        return carry

    lax.fori_loop(0, n1 // unroll, stage2, 0)

    def compact(i, carry):
        for uu in range(unroll):
            k2 = i * unroll + uu
            o_ref[pl.ds(pl.multiple_of(k2 * n1, n1), n1), :] = y_scr[pl.ds(pl.multiple_of(k2 * pitch, 8), n1), :]
        return carry

    lax.fori_loop(0, n2 // unroll, compact, 0)


def _seq_fft(u):
    b, s, c = u.shape
    n1 = FFT_N1
    n2 = s // n1
    g, f2 = _fft_tables(s)
    unroll = min(FFT_UNROLL, n2)
    blk = pl.BlockSpec((None, s, LANES), lambda bi, j: (bi, 0, j))
    return pl.pallas_call(
        functools.partial(_fft_kernel, n1=n1, n2=n2, unroll=unroll),
        grid=(b, c // LANES),
        in_specs=[blk, _resident(g.shape, lambda bi, j: (0, 0, 0)), _resident(f2.shape, lambda bi, j: (0, 0))],
        out_specs=blk,
        out_shape=jax.ShapeDtypeStruct((b, s, c), I32),
        scratch_shapes=[pltpu.VMEM((n2 * (n1 + FFT_PITCH_PAD), LANES), I32)],
        compiler_params=_cparams(("parallel", "parallel")),
        name="seq_fft",
    )(u, g, f2)


def _attn_kernel(sink_ref, q_ref, kp_ref, kc_ref, kn_ref, vp_ref, vc_ref, vn_ref, o_ref,
                 s_scr, p_scr, inv_scr, *, tq):
    i = pl.program_id(1)
    first = i == 0
    last = i == pl.num_programs(1) - 1
    nsub = tq // WINDOW
    nq = Q_PER_KV * WINDOW
    nk = 3 * WINDOW
    c = lax.broadcasted_iota(I32, (nk, nq), 0)
    r = lax.broadcasted_iota(I32, (nk, nq), 1) & (WINDOW - 1)
    band = ((c >= r) & (c < WINDOW)) | ((c >= WINDOW) & (c < 2 * WINDOW)) | ((c >= 2 * WINDOW) & (c - 2 * WINDOW <= r))
    pairs = [(h, sb) for h in range(N_KV_HEADS) for sb in range(nsub)]
    kcat, vt = [], []
    for h in range(N_KV_HEADS):
        hs = slice(h * HEAD_DIM, (h + 1) * HEAD_DIM)
        kcat.append(jnp.concatenate([kp_ref[:, hs], kc_ref[:, hs], kn_ref[:, hs]], axis=0))
        vcat = jnp.concatenate([vp_ref[:, hs], vc_ref[:, hs], vn_ref[:, hs]], axis=0)
        vt.append(vcat.astype(F32).T.astype(BF16))

    for n, (h, sb) in enumerate(pairs):
        q4 = jnp.concatenate(
            [q_ref[sb * WINDOW:(sb + 1) * WINDOW,
                   (h * Q_PER_KV + g) * HEAD_DIM:(h * Q_PER_KV + g + 1) * HEAD_DIM]
             for g in range(Q_PER_KV)], axis=0)
        kw = kcat[h][sb * WINDOW:(sb + 3) * WINDOW]
        s_scr[n] = lax.dot_general(kw, q4, (((1,), (1,)), ((), ())), preferred_element_type=F32)

    for n, (h, sb) in enumerate(pairs):
        sc = jnp.where(band, s_scr[n], NEG)
        if sb == 0:
            sc = jnp.where((c < WINDOW) & first, NEG, sc)
        if sb == nsub - 1:
            sc = jnp.where((c >= 2 * WINDOW) & last, NEG, sc)
        sink = jnp.concatenate(
            [jnp.full((1, WINDOW), sink_ref[h * Q_PER_KV + g], F32) for g in range(Q_PER_KV)], axis=1)
        m = jnp.maximum(jnp.max(sc, axis=0, keepdims=True), sink)
        p = jnp.exp(sc - m)
        denom = jnp.sum(p, axis=0, keepdims=True) + jnp.exp(sink - m)
        p_scr[n] = p.astype(BF16)
        inv_scr[pl.ds(n, 1), :] = 1.0 / denom

    for n, (h, sb) in enumerate(pairs):
        ot = jnp.dot(vt[h][:, sb * WINDOW:(sb + 3) * WINDOW], p_scr[n], preferred_element_type=F32)
        o = (ot * inv_scr[pl.ds(n, 1), :]).T
        for g in range(Q_PER_KV):
            col = (h * Q_PER_KV + g) * HEAD_DIM
            o_ref[sb * WINDOW:(sb + 1) * WINDOW, col:col + HEAD_DIM] = (
                o[g * WINDOW:(g + 1) * WINDOW].astype(BF16))


def _attention(qkv, sink):
    b, s, _ = qkv.shape
    aw = N_HEADS * HEAD_DIM
    kvw = N_KV_HEADS * HEAD_DIM
    tq = min(ATTN_TILE, s)
    per = tq // WINDOW
    nblk = s // WINDOW
    kcol = aw // kvw
    vcol = kcol + 1
    halo = lambda col, off: pl.BlockSpec(
        (None, WINDOW, kvw),
        lambda bi, i, sk: (bi, jnp.clip(i * per + off, 0, nblk - 1), col))
    main = lambda col: pl.BlockSpec((None, tq, kvw), lambda bi, i, sk: (bi, i, col))
    return pl.pallas_call(
        functools.partial(_attn_kernel, tq=tq),
        grid_spec=pltpu.PrefetchScalarGridSpec(
            num_scalar_prefetch=1,
            grid=(b, s // tq),
            in_specs=[pl.BlockSpec((None, tq, aw), lambda bi, i, sk: (bi, i, 0)),
                      halo(kcol, -1), main(kcol), halo(kcol, per),
                      halo(vcol, -1), main(vcol), halo(vcol, per)],
            out_specs=pl.BlockSpec((None, tq, aw), lambda bi, i, sk: (bi, i, 0)),
            scratch_shapes=[pltpu.VMEM((N_KV_HEADS * per, 3 * WINDOW, Q_PER_KV * WINDOW), F32),
                            pltpu.VMEM((N_KV_HEADS * per, 3 * WINDOW, Q_PER_KV * WINDOW), BF16),
                            pltpu.VMEM((N_KV_HEADS * per, Q_PER_KV * WINDOW), F32)]),
        out_shape=jax.ShapeDtypeStruct((b, s, aw), BF16),
        compiler_params=_cparams(("parallel", "parallel")),
        name="attention",
    )(sink, qkv, qkv, qkv, qkv, qkv, qkv, qkv)


ROW_WORDS = 1024
ROW_SUB = ROW_WORDS // LANES


def _store_tile_rows(ref, val, base=0, lead=()):
    r = val.shape[0]
    for j in range(ROW_SUB):
        ref[lead + (pl.ds(base * ROW_SUB + j, r, stride=ROW_SUB), slice(None))] = val[:, j * LANES:(j + 1) * LANES]


def _load_tile_rows(ref, base, r, lead=()):
    return jnp.concatenate(
        [ref[lead + (pl.ds(base * ROW_SUB + j, r, stride=ROW_SUB), slice(None))] for j in range(ROW_SUB)], axis=1)


def _tile_row(ref, row):
    return ref.at[pl.ds(pl.multiple_of(row * ROW_SUB, ROW_SUB), ROW_SUB), :]


def _post_kernel(wp_ref, o_ref, gate_ref, x_ref, g1_ref, sh2_ref, sc2_ref, n2g_ref, cc_ref, cs_ref,
                 wfo_ref, wao_ref, wout_ref, wr_ref, br_ref, cnt_in_ref,
                 x1_ref, h2_ref, route_ref, cnt_ref):
    tm = x_ref.shape[0]
    d = x_ref.shape[1]

    @pl.when(pl.program_id(0) == 0)
    def _():
        cnt_ref[...] = cnt_in_ref[...]

    re, im = _unpack_pair(wp_ref[...])
    re, im = re.astype(BF16), im.astype(BF16)
    cc, cs = cc_ref[...], cs_ref[...]
    gd = FOURIER_GROUP_DIM
    fm = jnp.concatenate(
        [(jnp.dot(re[:, g * gd:(g + 1) * gd], cc, preferred_element_type=F32)
          + jnp.dot(im[:, g * gd:(g + 1) * gd], cs, preferred_element_type=F32)).astype(BF16)
         for g in range(FOURIER_GROUPS)], axis=1)
    y_f = jnp.dot(fm, wfo_ref[...], preferred_element_type=F32)
    y_a = jnp.dot(o_ref[...], wao_ref[...], preferred_element_type=F32)
    merged = (_sigmoid(gate_ref[:, :d].astype(F32)) * y_f
              + _sigmoid(gate_ref[:, d:].astype(F32)) * y_a).astype(BF16)
    x1 = x_ref[...] + g1_ref[...] * jnp.dot(merged, wout_ref[...], preferred_element_type=F32)
    x1_ref[...] = x1
    ms = jnp.mean(x1 * x1, axis=-1, keepdims=True)
    h2 = (x1 * lax.rsqrt(ms + EPS) * n2g_ref[...]) * (1.0 + sc2_ref[...]) + sh2_ref[...]
    half = d // 2
    _store_tile_rows(h2_ref, _pack_pair(h2[:, :half], h2[:, half:]))

    h_hi = h2.astype(BF16)
    h_lo = (h2 - h_hi.astype(F32)).astype(BF16)
    wr = wr_ref[...]
    r_hi = jnp.dot(h_hi, wr, preferred_element_type=F32)
    r_lo = jnp.dot(h_lo, wr, preferred_element_type=F32)
    logits = ((r_hi[:, :ROUTE_LANES] + r_hi[:, ROUTE_LANES:])
              + (r_lo[:, :ROUTE_LANES] + r_lo[:, ROUTE_LANES:])) + br_ref[...]
    lane = lax.broadcasted_iota(I32, (tm, ROUTE_LANES), 1)
    big = jnp.int32(ROUTE_LANES)
    is_g = lane < N_GROUPS
    gl = jnp.where(is_g, logits, NEG)
    gmax = jnp.max(gl, axis=-1, keepdims=True)
    gidx = jnp.min(jnp.where(gl == gmax, lane, big), axis=-1, keepdims=True)
    p_g = 1.0 / jnp.sum(jnp.where(is_g, jnp.exp(gl - gmax), 0.0), axis=-1, keepdims=True)
    in_grp = (lane >= N_GROUPS) & (lane < N_GROUPS + N_EXPERTS) & (
        lax.shift_right_logical(lane - N_GROUPS, jnp.full(lane.shape, 3, I32)) == gidx)
    el = jnp.where(in_grp, logits, NEG)
    e1v = jnp.max(el, axis=-1, keepdims=True)
    e1i = jnp.min(jnp.where(el == e1v, lane, big), axis=-1, keepdims=True)
    el2 = jnp.where(lane == e1i, NEG, el)
    e2v = jnp.max(el2, axis=-1, keepdims=True)
    e2i = jnp.min(jnp.where(el2 == e2v, lane, big), axis=-1, keepdims=True)
    t = jnp.exp(e2v - e1v)
    w1 = p_g / (1.0 + t)
    w2 = w1 * t
    sel1, sel2 = lane == e1i, lane == e2i
    member = jnp.where(sel1 | sel2, 1.0, 0.0)
    rr = lax.broadcasted_iota(I32, (tm, tm), 0)
    cc_i = lax.broadcasted_iota(I32, (tm, tm), 1)
    tri = jnp.where(cc_i < rr, 1.0, 0.0).astype(BF16)
    prefix = jnp.dot(tri, member.astype(BF16), preferred_element_type=F32) + cnt_ref[...]
    rank1 = jnp.sum(jnp.where(sel1, prefix, 0.0), axis=-1, keepdims=True)
    rank2 = jnp.sum(jnp.where(sel2, prefix, 0.0), axis=-1, keepdims=True)
    cnt_ref[...] = cnt_ref[...] + jnp.sum(member, axis=0, keepdims=True)
    key_scale = float(1 << RANK_BITS)
    key1 = (e1i - N_GROUPS).astype(F32) * key_scale + rank1
    key2 = (e2i - N_GROUPS).astype(F32) * key_scale + rank2
    route = jnp.zeros((tm, ROUTE_LANES), F32)
    for k, val in enumerate((key1, key2, w1, w2)):
        route = jnp.where(lane == k, val, route)
    route_ref[...] = route


def _post(wp, attn, gates, x, g1, sh2, sc2, norm2_g, cc, cs, wfo_b, wao_b, wout_b, wr, br,
          counts_in):
    b, s, d = x.shape
    t = b * s
    tm = ROW_TILE
    per_b_tiles = s // tm
    flat = lambda a: a.reshape(t, a.shape[-1])
    row = lambda w: pl.BlockSpec((tm, w), lambda i: (i, 0))
    per_b = pl.BlockSpec((None, 1, d), lambda i: (i // per_b_tiles, 0, 0))
    const = lambda a: _resident(a.shape, lambda i: (0,) * a.ndim)
    fw = wp.shape[-1]
    aw = attn.shape[-1]
    return pl.pallas_call(
        _post_kernel,
        grid=(t // tm,),
        in_specs=[row(fw), row(aw), row(2 * d), row(d), per_b, per_b, per_b,
                  const(norm2_g), const(cc), const(cs), const(wfo_b), const(wao_b), const(wout_b),
                  const(wr), const(br), const(counts_in)],
        out_specs=[row(d), pl.BlockSpec((tm * ROW_SUB, LANES), lambda i: (i, 0)), row(ROUTE_LANES),
                   pl.BlockSpec((1, ROUTE_LANES), lambda i: (0, 0))],
        out_shape=[jax.ShapeDtypeStruct((t, d), F32), jax.ShapeDtypeStruct((t * ROW_SUB, LANES), I32),
                   jax.ShapeDtypeStruct((t, ROUTE_LANES), F32), jax.ShapeDtypeStruct((1, ROUTE_LANES), F32)],
        compiler_params=_cparams(("arbitrary",)),
        name="post_router",
    )(flat(wp), flat(attn), flat(gates), flat(x), g1, sh2, sc2, norm2_g, cc, cs, wfo_b, wao_b, wout_b, wr, br,
      counts_in)


def _sortrows_kernel(dest_ref, pstart_ref, pend_ref, *refs, tm, rows, tile_starts):
    n_groups = len(tile_starts)
    h_refs = refs[:n_groups]
    xs_hbm, stage, zbuf, sem, zsem = refs[n_groups:]
    j = pl.program_id(0)
    n = pl.num_programs(0)
    slot = j % 2
    blk_sub = rows * ROW_SUB

    def wait_slot(sl):
        nsub = TOP_K * tm * ROW_SUB
        pltpu.make_async_copy(stage.at[0, pl.ds(0, nsub), :], xs_hbm.at[pl.ds(0, nsub), :], sem.at[sl]).wait()

    @pl.when(j == 0)
    def _():
        zbuf[...] = jnp.zeros_like(zbuf)

        def zero_block(off):
            return pltpu.make_async_copy(zbuf, xs_hbm.at[pl.ds(pl.multiple_of(off * ROW_SUB, blk_sub), blk_sub), :],
                                         zsem)

        def per_expert(fn):
            def body(e, carry):
                @pl.when(pend_ref[e] > pstart_ref[e])
                def _():
                    fn(zero_block(pend_ref[e] - rows))
                return carry
            lax.fori_loop(0, N_EXPERTS, body, 0)

        n_used = pend_ref[N_EXPERTS - 1] // rows
        n_blocks = xs_hbm.shape[0] // blk_sub

        def tail(fn):
            lax.fori_loop(n_used, n_blocks, lambda b, c: (fn(zero_block(b * rows)), c)[1], 0)

        per_expert(lambda cp: cp.start())
        tail(lambda cp: cp.start())
        per_expert(lambda cp: cp.wait())
        tail(lambda cp: cp.wait())

    @pl.when(j >= 2)
    def _():
        wait_slot(slot)

    tile = h_refs[0][...]
    for g in range(1, n_groups):
        tile = jnp.where(j >= tile_starts[g], h_refs[g][...], tile)
    stage[slot, pl.ds(0, tm * ROW_SUB), :] = tile

    def body(gi, carry):
        r0 = gi * DMA_GROUP
        base = (j * tm + r0) * TOP_K
        dests = [dest_ref[base + q] for q in range(DMA_GROUP * TOP_K)]
        for q, dest in enumerate(dests):
            pltpu.make_async_copy(_tile_row(stage.at[slot], r0 + q // TOP_K), _tile_row(xs_hbm, dest),
                                  sem.at[slot]).start(priority=q % DMA_QUEUES)
        return carry

    lax.fori_loop(0, tm // DMA_GROUP, body, 0)

    @pl.when(j == n - 1)
    def _():
        @pl.when(j >= 1)
        def _():
            wait_slot(1 - slot)

        wait_slot(slot)


def _sortrows(h2_groups, dest, pad_start, pad_end, n_rows):
    tm = SHUFFLE_TILE
    tiles = [h.shape[0] // (ROW_SUB * tm) for h in h2_groups]
    starts = [sum(tiles[:g]) for g in range(len(tiles))]
    spec = lambda g: pl.BlockSpec((tm * ROW_SUB, LANES),
                                  lambda j, *_: (jnp.clip(j - starts[g], 0, tiles[g] - 1), 0))
    return pl.pallas_call(
        functools.partial(_sortrows_kernel, tm=tm, rows=MOE_ROWS, tile_starts=tuple(starts)),
        grid_spec=pltpu.PrefetchScalarGridSpec(
            num_scalar_prefetch=3,
            grid=(sum(tiles),),
            in_specs=[spec(g) for g in range(len(tiles))],
            out_specs=pl.BlockSpec(memory_space=pl.ANY),
            scratch_shapes=[pltpu.VMEM((2, TOP_K * tm * ROW_SUB, LANES), I32),
                            pltpu.VMEM((MOE_ROWS * ROW_SUB, LANES), I32),
                            pltpu.SemaphoreType.DMA((2,)), pltpu.SemaphoreType.DMA(())]),
        out_shape=jax.ShapeDtypeStruct((n_rows * ROW_SUB, LANES), I32),
        compiler_params=_cparams(("arbitrary",), disable_bounds_checks=True, has_side_effects=True),
        name="sort_rows",
    )(dest, pad_start, pad_end, *h2_groups)


def _expert_kernel(pos_ref, seq_ref, nused_ref, xs_ref, w1_hbm, w3_hbm, w2_hbm, y_ref,
                   w1f, w3f, w2f, wsem, w1b, w3b, w2b, *, rows):
    j = pl.program_id(0)
    weights = ((w1_hbm, w1f), (w3_hbm, w3f), (w2_hbm, w2f))

    def start_weights(k, buf):
        e = seq_ref[k]
        for src, dst in weights:
            n = src.shape[1] // WEIGHT_CHUNKS
            for c in range(WEIGHT_CHUNKS):
                pltpu.make_async_copy(src.at[e, pl.ds(c * n, n), :], dst.at[buf, pl.ds(c * n, n), :],
                                      wsem.at[buf]).start(priority=1)

    def wait_weights(k, buf):
        e = seq_ref[k]
        for src, dst in weights:
            pltpu.make_async_copy(src.at[e], dst.at[buf], wsem.at[buf]).wait()

    @pl.when(j == 0)
    def _():
        start_weights(0, 0)

    pos = pos_ref[j]
    pos_prev = pos_ref[jnp.maximum(j - 1, 0)]

    @pl.when((j == 0) | (pos != pos_prev))
    def _():
        buf = pos % 2
        wait_weights(pos, buf)

        @pl.when(pos + 1 < nused_ref[1])
        def _():
            start_weights(pos + 1, 1 - buf)

        w1b[...] = w1f[buf].astype(BF16)
        w3b[...] = w3f[buf].astype(BF16)
        w2b[...] = w2f[buf].astype(BF16)

    @pl.when(j < nused_ref[0])
    def _():
        hi, lo = _unpack_pair(_load_tile_rows(xs_ref, 0, rows))
        xb = jnp.concatenate([hi.astype(BF16), lo.astype(BF16)], axis=1)
        a = jnp.dot(xb, w1b[...], preferred_element_type=F32)
        g = jnp.dot(xb, w3b[...], preferred_element_type=F32)
        hid = (a * _sigmoid(a) * g).astype(BF16)
        y = jnp.dot(hid, w2b[...], preferred_element_type=F32)
        half = y.shape[1] // 2
        _store_tile_rows(y_ref, _pack_pair(y[:, :half], y[:, half:]))

    @pl.when(j >= nused_ref[0])
    def _():
        y_ref[...] = jnp.zeros_like(y_ref)


def _experts(xs, block_pos, expert_seq, n_used, w1, w3, w2):
    rows = MOE_ROWS
    blk = rows * ROW_SUB
    d, f = w1.shape[1], w1.shape[2]
    hbm = pl.BlockSpec(memory_space=pl.ANY)
    return pl.pallas_call(
        functools.partial(_expert_kernel, rows=rows),
        grid_spec=pltpu.PrefetchScalarGridSpec(
            num_scalar_prefetch=3,
            grid=(xs.shape[0] // blk,),
            in_specs=[pl.BlockSpec((blk, LANES), lambda j, bp, es, nu: (jnp.minimum(j, nu[0] - 1), 0)),
                      hbm, hbm, hbm],
            out_specs=pl.BlockSpec((blk, LANES), lambda j, *_: (j, 0)),
            scratch_shapes=[pltpu.VMEM((2, d, f), F32), pltpu.VMEM((2, d, f), F32), pltpu.VMEM((2, f, d), F32),
                            pltpu.SemaphoreType.DMA((2,)),
                            pltpu.VMEM((d, f), BF16), pltpu.VMEM((d, f), BF16), pltpu.VMEM((f, d), BF16)]),
        out_shape=jax.ShapeDtypeStruct(xs.shape, I32),
        compiler_params=_cparams(("arbitrary",)),
        name="experts",
    )(block_pos, expert_seq, n_used, xs, w1, w3, w2)


def _combine_kernel(dest_ref, y_hbm, x1_ref, route_ref, g2_ref, o_ref, ybuf, sem, *, tm):
    i = pl.program_id(0)
    n = pl.num_programs(0)
    slot = i % 2

    def start_gather(tile, sl):
        def body(gi, carry):
            r0 = gi * DMA_GROUP
            base = (tile * tm + r0) * TOP_K
            dests = [dest_ref[base + q] for q in range(DMA_GROUP * TOP_K)]
            for q, dest in enumerate(dests):
                pltpu.make_async_copy(_tile_row(y_hbm, dest),
                                      _tile_row(ybuf.at[sl], (q % TOP_K) * tm + r0 + q // TOP_K),
                                      sem.at[sl]).start(priority=q % DMA_QUEUES)
            return carry

        lax.fori_loop(0, tm // DMA_GROUP, body, 0)

    @pl.when(i == 0)
    def _():
        start_gather(0, 0)

    @pl.when(i + 1 < n)
    def _():
        start_gather(i + 1, 1 - slot)

    pltpu.make_async_copy(y_hbm.at[pl.ds(0, TOP_K * tm * ROW_SUB), :], ybuf.at[slot], sem.at[slot]).wait()
    route = route_ref[...]
    w1 = route[:, 2:3]
    w2 = route[:, 3:4]
    hi1, lo1 = _unpack_pair(_load_tile_rows(ybuf, 0, tm, lead=(slot,)))
    hi2, lo2 = _unpack_pair(_load_tile_rows(ybuf, tm, tm, lead=(slot,)))
    half = o_ref.shape[1] // 2
    o_ref[:, :half] = x1_ref[:, :half] + g2_ref[:, :half] * (w1 * hi1 + w2 * hi2)
    o_ref[:, half:] = x1_ref[:, half:] + g2_ref[:, half:] * (w1 * lo1 + w2 * lo2)


def _combine(y, dest, x1, route, g2, s):
    t, d = x1.shape
    tm = SHUFFLE_TILE
    per_b_tiles = s // tm
    return pl.pallas_call(
        functools.partial(_combine_kernel, tm=tm),
        grid_spec=pltpu.PrefetchScalarGridSpec(
            num_scalar_prefetch=1,
            grid=(t // tm,),
            in_specs=[pl.BlockSpec(memory_space=pl.ANY),
                      pl.BlockSpec((tm, d), lambda i, ds: (i, 0)),
                      pl.BlockSpec((tm, ROUTE_LANES), lambda i, ds: (i, 0)),
                      pl.BlockSpec((None, 1, d), lambda i, ds: (i // per_b_tiles, 0, 0))],
            out_specs=pl.BlockSpec((tm, d), lambda i, ds: (i, 0)),
            scratch_shapes=[pltpu.VMEM((2, TOP_K * tm * ROW_SUB, LANES), I32), pltpu.SemaphoreType.DMA((2,))]),
        out_shape=jax.ShapeDtypeStruct((t, d), F32),
        compiler_params=_cparams(("arbitrary",), disable_bounds_checks=True),
        name="combine",
    )(dest, y, x1, route, g2)


def _dispatch_plan(route_keys, counts, t):
    rows = MOE_ROWS
    keys = route_keys.astype(I32)
    expert = lax.shift_right_logical(keys, jnp.full(keys.shape, RANK_BITS, I32))
    rank = keys & ((1 << RANK_BITS) - 1)
    cnt = counts[0, N_GROUPS:N_GROUPS + N_EXPERTS].astype(I32)
    padded = (cnt + rows - 1) // rows * rows
    pad_end = jnp.cumsum(padded)
    pad_start = pad_end - padded
    onehot = expert[..., None] == jnp.arange(N_EXPERTS, dtype=I32)
    dest = (jnp.sum(jnp.where(onehot, pad_start, 0), axis=-1) + rank).reshape(t * TOP_K)
    n_blocks = (t * TOP_K + N_EXPERTS * (rows - 1) + rows - 1) // rows
    n_used = pad_end[-1:] // rows
    blk = jnp.minimum(jnp.arange(n_blocks, dtype=I32), n_used - 1) * rows
    block_e = jnp.sum((pad_end[None, :] <= blk[:, None]).astype(I32), axis=1)
    owns = cnt > 0
    expert_seq = jnp.argsort(jnp.logical_not(owns), stable=True).astype(I32)
    seq_index = jnp.cumsum(owns.astype(I32)) - 1
    block_pos = jnp.sum(jnp.where(block_e[:, None] == jnp.arange(N_EXPERTS, dtype=I32), seq_index, 0), axis=1)
    used = jnp.concatenate([n_used, jnp.sum(owns.astype(I32), keepdims=True)]).astype(I32)
    return dest, pad_start, pad_end, block_pos.astype(I32), expert_seq, used, n_blocks * rows


def _layer(xs, mods, p):
    d = xs[0].shape[-1]
    t_all = sum(x.shape[0] * x.shape[1] for x in xs)
    counts = jnp.zeros((1, ROUTE_LANES), F32)
    rope = _rope_tables(max(x.shape[1] for x in xs))
    per_group, h2_groups, tok_off = [], [], 0
    for x, mod in zip(xs, mods):
        b, s, _ = x.shape
        sh1, sc1, g1, sh2, sc2, g2 = [m.reshape(b, 1, d) for m in jnp.split(mod, 6, axis=-1)]
        u, qkv, gates = _inproj(x, sh1, sc1, p["norm1_g"], p["q_norm_g"], p["k_norm_g"], p["w_in"], rope)
        wp = _seq_fft(u)
        attn = _attention(qkv, p["sink"])
        x1, h2, route, counts = _post(wp, attn, gates, x, g1, sh2, sc2, p["norm2_g"], p["cc"], p["cs"],
                                      p["w_fourier_out"], p["w_attn_out"], p["w_out"],
                                      p["w_router"], p["b_router"], counts)
        per_group.append((x1, route, g2, tok_off))
        h2_groups.append(h2)
        tok_off += b * s
    route_keys = jnp.concatenate([r[:, 0:TOP_K] for _, r, _, _ in per_group], axis=0)
    dest, pad_start, pad_end, block_pos, expert_seq, n_used, n_rows = _dispatch_plan(route_keys, counts, t_all)
    xs_sorted = _sortrows(h2_groups, dest, pad_start, pad_end, n_rows)
    y = _experts(xs_sorted, block_pos, expert_seq, n_used, p["w1"], p["w3"], p["w2"])
    outs = []
    for x, (x1, route, g2, off) in zip(xs, per_group):
        b, s, _ = x.shape
        dest_g = lax.slice(dest, (off * TOP_K,), ((off + b * s) * TOP_K,))
        outs.append(_combine(y, dest_g, x1, route, g2, s).reshape(b, s, d))
    return outs


def _channel_dft():
    n = FOURIER_GROUP_DIM
    k = np.arange(n, dtype=np.int64)
    ang = ((k[:, None] * k[None, :]) % n) * (2.0 * math.pi / n)
    return (jnp.asarray((np.cos(ang) * n ** -0.5).astype(BF16)),
            jnp.asarray((np.sin(ang) * n ** -0.5).astype(BF16)))


def kernel(x_prompt, x_sample, c_prompt, c_sample, w_ada, b_ada, norm1_g, w_in, q_norm_g, k_norm_g, sink,
           w_fourier_out, w_attn_out, w_out, norm2_g, w_group, b_group, w_expert, b_expert, w1, w3, w2):
    depth = w_ada.shape[0]
    d = x_prompt.shape[-1]
    bp = c_prompt.shape[0]
    bs = c_sample.shape[0]
    cc, cs = _channel_dft()
    xp, xs = x_prompt, x_sample
    for l in range(depth):
        c_all = jnp.concatenate([c_prompt, c_sample, jnp.zeros((8 - (bp + bs) % 8, d), F32)], axis=0)
        mod = _adaln(c_all, w_ada[l], b_ada[l])
        pad = ROUTE_LANES - N_GROUPS - N_EXPERTS
        w_router = jnp.concatenate([w_group[l], w_expert[l], jnp.zeros((d, pad), F32)], axis=1)
        w_router_hi = w_router.astype(BF16)
        p = {
            "norm1_g": norm1_g[l].reshape(1, d), "norm2_g": norm2_g[l].reshape(1, d),
            "q_norm_g": q_norm_g[l].reshape(1, HEAD_DIM), "k_norm_g": k_norm_g[l].reshape(1, HEAD_DIM),
            "sink": sink[l], "w_in": w_in[l].astype(BF16),
            "w_fourier_out": w_fourier_out[l].astype(BF16), "w_attn_out": w_attn_out[l].astype(BF16),
            "w_out": w_out[l].astype(BF16), "cc": cc, "cs": cs,
            "w_router": jnp.concatenate(
                [w_router_hi, (w_router - w_router_hi.astype(F32)).astype(BF16)], axis=1),
            "b_router": jnp.concatenate([b_group[l], b_expert[l], jnp.zeros((pad,), F32)]).reshape(1, ROUTE_LANES),
            "w1": w1[l], "w3": w3[l], "w2": w2[l],
        }
        xp, xs = _layer([xp, xs], [mod[:bp], mod[bp:bp + bs]], p)
    return xp, xs
```

```python
import functools
import math

import jax
import jax.numpy as jnp
import numpy as np
from jax import lax
from jax.experimental import pallas as pl
from jax.experimental.pallas import tpu as pltpu

F32 = jnp.float32
BF16 = jnp.bfloat16
I32 = jnp.int32

HEAD_DIM = 128
N_KV_HEADS = 2
Q_PER_KV = 4
N_HEADS = N_KV_HEADS * Q_PER_KV
ROT_DIM = 32
ROPE_THETA = 500000.0
WINDOW = 128
FOURIER_GROUPS = 4
FOURIER_GROUP_DIM = 256
N_GROUPS = 8
EXPERTS_PER_GROUP = 8
N_EXPERTS = N_GROUPS * EXPERTS_PER_GROUP
TOP_K = 2
EPS = 1e-6

LANES = 128
V7X_VMEM_LIMIT = 56 * 1024 * 1024

ROW_TILE = 256
SORT_TILE = 512
ATTN_TILE = 1024
FFT_N1 = 128
FFT_UNROLL = 16
FFT_PITCH_PAD = 8
MOE_ROWS = 512
ROUTE_LANES = 128
RANK_BITS = 17
DMA_GROUP = 8
WEIGHT_CHUNKS = 8
DMA_QUEUES = 2
NEG = -1e30


def _cparams(sem, **kw):
    return pltpu.CompilerParams(dimension_semantics=sem, vmem_limit_bytes=V7X_VMEM_LIMIT, **kw)


def _resident(shape, index_map):
    return pl.BlockSpec(shape, index_map, pipeline_mode=pl.Buffered(1))


def _sigmoid(x):
    return 0.5 * jnp.tanh(0.5 * x) + 0.5


def _adaln_kernel(c_ref, w_ref, b_ref, o_ref):
    c = c_ref[...]
    s = c * _sigmoid(c)
    o_ref[...] = jnp.dot(s, w_ref[...], precision=lax.Precision.HIGHEST,
                         preferred_element_type=F32) + b_ref[...]


def _adaln(c, w_ada, b_ada):
    r, d = c.shape
    n = w_ada.shape[1]
    tn = 1024
    return pl.pallas_call(
        _adaln_kernel,
        grid=(n // tn,),
        in_specs=[pl.BlockSpec((r, d), lambda j: (0, 0)),
                  pl.BlockSpec((d, tn), lambda j: (0, j)),
                  pl.BlockSpec((1, tn), lambda j: (0, j))],
        out_specs=pl.BlockSpec((r, tn), lambda j: (0, j)),
        out_shape=jax.ShapeDtypeStruct((r, n), F32),
        compiler_params=_cparams(("parallel",)),
        name="adaln",
    )(c, w_ada, b_ada.reshape(1, n))


def _rope_tables(s):
    half = ROT_DIM // 2
    inv_freq = ROPE_THETA ** (-jnp.arange(0, ROT_DIM, 2, dtype=F32) / ROT_DIM)
    ang = jnp.arange(s, dtype=F32)[:, None] * inv_freq[None, :]
    cos, sin = jnp.cos(ang), jnp.sin(ang)
    pad = jnp.zeros((s, LANES - ROT_DIM), F32)
    cos_f = jnp.concatenate([cos, cos, jnp.ones((s, LANES - ROT_DIM), F32)], axis=1)
    sin_up = jnp.concatenate([-sin, jnp.zeros((s, half), F32), pad], axis=1)
    sin_dn = jnp.concatenate([jnp.zeros((s, half), F32), sin, pad], axis=1)
    return cos_f, sin_up, sin_dn


def _inproj_kernel(x_ref, sh_ref, sc_ref, g_ref, qg_ref, kg_ref, cos_ref, sup_ref, sdn_ref, w_ref,
                   u_ref, qkv_ref, gate_ref, *, fw, aw, kvw):
    x = x_ref[...]
    ms = jnp.mean(x * x, axis=-1, keepdims=True)
    h = (x * lax.rsqrt(ms + EPS) * g_ref[...]) * (1.0 + sc_ref[...]) + sh_ref[...]
    hb = h.astype(BF16)
    chunk = 512
    for c0 in range(0, fw, chunk):
        u_ref[:, c0:c0 + chunk] = jnp.dot(hb, w_ref[:, c0:c0 + chunk], preferred_element_type=F32)
    cos_f, sin_up, sin_dn = cos_ref[...], sup_ref[...], sdn_ref[...]
    half = ROT_DIM // 2
    scale = HEAD_DIM ** -0.5
    for c0 in list(range(0, aw, chunk)) + [aw]:
        width = chunk if c0 < aw else kvw
        acc = jnp.dot(hb, w_ref[:, fw + c0:fw + c0 + width], preferred_element_type=F32)
        for hh in range(width // HEAD_DIM):
            col = c0 + hh * HEAD_DIM
            t = acc[:, hh * HEAD_DIM:(hh + 1) * HEAD_DIM]
            is_q = col < aw
            gain = qg_ref[...] if is_q else kg_ref[...]
            t = t * lax.rsqrt(jnp.mean(t * t, axis=-1, keepdims=True) + EPS) * gain
            t = (t * cos_f + pltpu.roll(t, LANES - half, axis=1) * sin_up
                 + pltpu.roll(t, half, axis=1) * sin_dn)
            if is_q:
                t = t * scale
            qkv_ref[:, col:col + HEAD_DIM] = t.astype(BF16)
    v0 = fw + aw + kvw
    qkv_ref[:, aw + kvw:aw + 2 * kvw] = jnp.dot(
        hb, w_ref[:, v0:v0 + kvw], preferred_element_type=F32).astype(BF16)
    g0 = v0 + kvw
    gw = gate_ref.shape[-1]
    for c0 in range(0, gw, chunk):
        gate_ref[:, c0:c0 + chunk] = jnp.dot(
            hb, w_ref[:, g0 + c0:g0 + c0 + chunk], preferred_element_type=F32).astype(BF16)


def _inproj(x, shift, scale, norm_g, q_g, k_g, w_in_b, tables):
    b, s, d = x.shape
    fw = FOURIER_GROUPS * FOURIER_GROUP_DIM
    aw = N_HEADS * HEAD_DIM
    kvw = N_KV_HEADS * HEAD_DIM
    gw = 2 * d
    tm = ROW_TILE
    row = lambda w: pl.BlockSpec((None, tm, w), lambda bi, i: (bi, i, 0))
    per_b = pl.BlockSpec((None, 1, d), lambda bi, i: (bi, 0, 0))
    const = lambda w: pl.BlockSpec((1, w), lambda bi, i: (0, 0))
    tab = pl.BlockSpec((tm, LANES), lambda bi, i: (i, 0))
    return pl.pallas_call(
        functools.partial(_inproj_kernel, fw=fw, aw=aw, kvw=kvw),
        grid=(b, s // tm),
        in_specs=[row(d), per_b, per_b, const(d), const(HEAD_DIM), const(HEAD_DIM), tab, tab, tab,
                  _resident(w_in_b.shape, lambda bi, i: (0, 0))],
        out_specs=[row(fw), row(aw + 2 * kvw), row(gw)],
        out_shape=[jax.ShapeDtypeStruct((b, s, fw), F32),
                   jax.ShapeDtypeStruct((b, s, aw + 2 * kvw), BF16),
                   jax.ShapeDtypeStruct((b, s, gw), BF16)],
        compiler_params=_cparams(("parallel", "parallel")),
        name="inproj",
    )(x, shift, scale, norm_g, q_g, k_g, *tables, w_in_b)


def _pack_pair(a, b):
    ab = lax.bitcast_convert_type(a.astype(BF16).astype(F32), I32)
    bb = lax.bitcast_convert_type(b.astype(BF16).astype(F32), I32)
    return ab | lax.shift_right_logical(bb, jnp.full(bb.shape, 16, I32))


def _unpack_pair(p):
    hi = lax.bitcast_convert_type(p & jnp.int32(-65536), F32)
    lo = lax.bitcast_convert_type(lax.shift_left(p, jnp.full(p.shape, 16, I32)), F32)
    return hi, lo


def _fft_tables(s):
    n1 = FFT_N1
    n2 = s // n1
    k1 = np.arange(n1, dtype=np.int64)
    nn = (n2 * np.arange(n1, dtype=np.int64))[None, None, :] + np.arange(n2, dtype=np.int64)[:, None, None]
    ang = ((k1[None, :, None] * nn) % s) * (2.0 * math.pi / s)
    sc1 = n1 ** -0.5
    g = np.concatenate([np.cos(ang) * sc1, -np.sin(ang) * sc1], axis=1).astype(BF16)
    k2 = np.arange(n2, dtype=np.int64)
    ang2 = ((k2[:, None] * k2[None, :]) % n2) * (2.0 * math.pi / n2)
    c2, s2 = np.cos(ang2) * n2 ** -0.5, np.sin(ang2) * n2 ** -0.5
    f2 = np.concatenate([np.concatenate([c2, s2], axis=1),
                         np.concatenate([-s2, c2], axis=1)], axis=0).astype(BF16)
    return jnp.asarray(g), jnp.asarray(f2)


def _fft_kernel(u_ref, g_ref, f2_ref, o_ref, y_scr, *, n1, n2, unroll):
    pitch = n1 + FFT_PITCH_PAD

    def stage1(i, carry):
        for uu in range(unroll):
            m = i * unroll + uu
            xm = u_ref[pl.ds(m, n1, stride=n2), :].astype(BF16)
            y = jnp.dot(g_ref[m], xm, preferred_element_type=F32)
            y_scr[pl.ds(pl.multiple_of(m * pitch, 8), n1), :] = _pack_pair(y[:n1], y[n1:])
        return carry

    lax.fori_loop(0, n2 // unroll, stage1, 0)
    f2 = f2_ref[...]

    def stage2(i, carry):
        for uu in range(unroll):
            k1 = i * unroll + uu
            yr, yi = _unpack_pair(y_scr[pl.ds(k1, n2, stride=pitch), :])
            rhs = jnp.concatenate([yr.astype(BF16), yi.astype(BF16)], axis=0)
            z = jnp.dot(f2, rhs, preferred_element_type=F32)
            y_scr[pl.ds(k1, n2, stride=pitch), :] = _pack_pair(z[:n2], z[n2:])
        return carry

    lax.fori_loop(0, n1 // unroll, stage2, 0)

    def compact(i, carry):
        for uu in range(unroll):
            k2 = i * unroll + uu
            o_ref[pl.ds(pl.multiple_of(k2 * n1, n1), n1), :] = y_scr[pl.ds(pl.multiple_of(k2 * pitch, 8), n1), :]
        return carry

    lax.fori_loop(0, n2 // unroll, compact, 0)


def _seq_fft(u):
    b, s, c = u.shape
    n1 = FFT_N1
    n2 = s // n1
    g, f2 = _fft_tables(s)
    unroll = min(FFT_UNROLL, n2)
    blk = pl.BlockSpec((None, s, LANES), lambda bi, j: (bi, 0, j))
    return pl.pallas_call(
        functools.partial(_fft_kernel, n1=n1, n2=n2, unroll=unroll),
        grid=(b, c // LANES),
        in_specs=[blk, _resident(g.shape, lambda bi, j: (0, 0, 0)), _resident(f2.shape, lambda bi, j: (0, 0))],
        out_specs=blk,
        out_shape=jax.ShapeDtypeStruct((b, s, c), I32),
        scratch_shapes=[pltpu.VMEM((n2 * (n1 + FFT_PITCH_PAD), LANES), I32)],
        compiler_params=_cparams(("parallel", "parallel")),
        name="seq_fft",
    )(u, g, f2)


def _attn_kernel(sink_ref, q_ref, kp_ref, kc_ref, kn_ref, vp_ref, vc_ref, vn_ref, o_ref,
                 s_scr, p_scr, inv_scr, *, tq):
    i = pl.program_id(1)
    first = i == 0
    last = i == pl.num_programs(1) - 1
    nsub = tq // WINDOW
    nq = Q_PER_KV * WINDOW
    nk = 3 * WINDOW
    c = lax.broadcasted_iota(I32, (nk, nq), 0)
    r = lax.broadcasted_iota(I32, (nk, nq), 1) & (WINDOW - 1)
    band = ((c >= r) & (c < WINDOW)) | ((c >= WINDOW) & (c < 2 * WINDOW)) | ((c >= 2 * WINDOW) & (c - 2 * WINDOW <= r))
    pairs = [(h, sb) for h in range(N_KV_HEADS) for sb in range(nsub)]
    kcat, vt = [], []
    for h in range(N_KV_HEADS):
        hs = slice(h * HEAD_DIM, (h + 1) * HEAD_DIM)
        kcat.append(jnp.concatenate([kp_ref[:, hs], kc_ref[:, hs], kn_ref[:, hs]], axis=0))
        vcat = jnp.concatenate([vp_ref[:, hs], vc_ref[:, hs], vn_ref[:, hs]], axis=0)
        vt.append(vcat.astype(F32).T.astype(BF16))

    for n, (h, sb) in enumerate(pairs):
        q4 = jnp.concatenate(
            [q_ref[sb * WINDOW:(sb + 1) * WINDOW,
                   (h * Q_PER_KV + g) * HEAD_DIM:(h * Q_PER_KV + g + 1) * HEAD_DIM]
             for g in range(Q_PER_KV)], axis=0)
        kw = kcat[h][sb * WINDOW:(sb + 3) * WINDOW]
        s_scr[n] = lax.dot_general(kw, q4, (((1,), (1,)), ((), ())), preferred_element_type=F32)

    for n, (h, sb) in enumerate(pairs):
        sc = jnp.where(band, s_scr[n], NEG)
        if sb == 0:
            sc = jnp.where((c < WINDOW) & first, NEG, sc)
        if sb == nsub - 1:
            sc = jnp.where((c >= 2 * WINDOW) & last, NEG, sc)
        sink = jnp.concatenate(
            [jnp.full((1, WINDOW), sink_ref[h * Q_PER_KV + g], F32) for g in range(Q_PER_KV)], axis=1)
        m = jnp.maximum(jnp.max(sc, axis=0, keepdims=True), sink)
        p = jnp.exp(sc - m)
        denom = jnp.sum(p, axis=0, keepdims=True) + jnp.exp(sink - m)
        p_scr[n] = p.astype(BF16)
        inv_scr[pl.ds(n, 1), :] = 1.0 / denom

    for n, (h, sb) in enumerate(pairs):
        ot = jnp.dot(vt[h][:, sb * WINDOW:(sb + 3) * WINDOW], p_scr[n], preferred_element_type=F32)
        o = (ot * inv_scr[pl.ds(n, 1), :]).T
        for g in range(Q_PER_KV):
            col = (h * Q_PER_KV + g) * HEAD_DIM
            o_ref[sb * WINDOW:(sb + 1) * WINDOW, col:col + HEAD_DIM] = (
                o[g * WINDOW:(g + 1) * WINDOW].astype(BF16))


def _attention(qkv, sink):
    b, s, _ = qkv.shape
    aw = N_HEADS * HEAD_DIM
    kvw = N_KV_HEADS * HEAD_DIM
    tq = min(ATTN_TILE, s)
    per = tq // WINDOW
    nblk = s // WINDOW
    kcol = aw // kvw
    vcol = kcol + 1
    halo = lambda col, off: pl.BlockSpec(
        (None, WINDOW, kvw),
        lambda bi, i, sk: (bi, jnp.clip(i * per + off, 0, nblk - 1), col))
    main = lambda col: pl.BlockSpec((None, tq, kvw), lambda bi, i, sk: (bi, i, col))
    return pl.pallas_call(
        functools.partial(_attn_kernel, tq=tq),
        grid_spec=pltpu.PrefetchScalarGridSpec(
            num_scalar_prefetch=1,
            grid=(b, s // tq),
            in_specs=[pl.BlockSpec((None, tq, aw), lambda bi, i, sk: (bi, i, 0)),
                      halo(kcol, -1), main(kcol), halo(kcol, per),
                      halo(vcol, -1), main(vcol), halo(vcol, per)],
            out_specs=pl.BlockSpec((None, tq, aw), lambda bi, i, sk: (bi, i, 0)),
            scratch_shapes=[pltpu.VMEM((N_KV_HEADS * per, 3 * WINDOW, Q_PER_KV * WINDOW), F32),
                            pltpu.VMEM((N_KV_HEADS * per, 3 * WINDOW, Q_PER_KV * WINDOW), BF16),
                            pltpu.VMEM((N_KV_HEADS * per, Q_PER_KV * WINDOW), F32)]),
        out_shape=jax.ShapeDtypeStruct((b, s, aw), BF16),
        compiler_params=_cparams(("parallel", "parallel")),
        name="attention",
    )(sink, qkv, qkv, qkv, qkv, qkv, qkv, qkv)


ROW_WORDS = 1024
ROW_SUB = ROW_WORDS // LANES


def _store_tile_rows(ref, val, base=0, lead=()):
    r = val.shape[0]
    for j in range(ROW_SUB):
        ref[lead + (pl.ds(base * ROW_SUB + j, r, stride=ROW_SUB), slice(None))] = val[:, j * LANES:(j + 1) * LANES]


def _load_tile_rows(ref, base, r, lead=()):
    return jnp.concatenate(
        [ref[lead + (pl.ds(base * ROW_SUB + j, r, stride=ROW_SUB), slice(None))] for j in range(ROW_SUB)], axis=1)


def _tile_row(ref, row):
    return ref.at[pl.ds(pl.multiple_of(row * ROW_SUB, ROW_SUB), ROW_SUB), :]


def _post_kernel(wp_ref, o_ref, gate_ref, x_ref, g1_ref, sh2_ref, sc2_ref, n2g_ref, cc_ref, cs_ref,
                 wfo_ref, wao_ref, wout_ref, wr_ref, br_ref, cnt_in_ref,
                 x1_ref, h2_ref, route_ref, cnt_ref):
    tm = x_ref.shape[0]
    d = x_ref.shape[1]

    @pl.when(pl.program_id(0) == 0)
    def _():
        cnt_ref[...] = cnt_in_ref[...]

    re, im = _unpack_pair(wp_ref[...])
    re, im = re.astype(BF16), im.astype(BF16)
    cc, cs = cc_ref[...], cs_ref[...]
    gd = FOURIER_GROUP_DIM
    fm = jnp.concatenate(
        [(jnp.dot(re[:, g * gd:(g + 1) * gd], cc, preferred_element_type=F32)
          + jnp.dot(im[:, g * gd:(g + 1) * gd], cs, preferred_element_type=F32)).astype(BF16)
         for g in range(FOURIER_GROUPS)], axis=1)
    y_f = jnp.dot(fm, wfo_ref[...], preferred_element_type=F32)
    y_a = jnp.dot(o_ref[...], wao_ref[...], preferred_element_type=F32)
    merged = (_sigmoid(gate_ref[:, :d].astype(F32)) * y_f
              + _sigmoid(gate_ref[:, d:].astype(F32)) * y_a).astype(BF16)
    x1 = x_ref[...] + g1_ref[...] * jnp.dot(merged, wout_ref[...], preferred_element_type=F32)
    x1_ref[...] = x1
    ms = jnp.mean(x1 * x1, axis=-1, keepdims=True)
    h2 = (x1 * lax.rsqrt(ms + EPS) * n2g_ref[...]) * (1.0 + sc2_ref[...]) + sh2_ref[...]
    half = d // 2
    _store_tile_rows(h2_ref, _pack_pair(h2[:, :half], h2[:, half:]))

    h_hi = h2.astype(BF16)
    h_lo = (h2 - h_hi.astype(F32)).astype(BF16)
    wr = wr_ref[...]
    r_hi = jnp.dot(h_hi, wr, preferred_element_type=F32)
    r_lo = jnp.dot(h_lo, wr, preferred_element_type=F32)
    logits = ((r_hi[:, :ROUTE_LANES] + r_hi[:, ROUTE_LANES:])
              + (r_lo[:, :ROUTE_LANES] + r_lo[:, ROUTE_LANES:])) + br_ref[...]
    lane = lax.broadcasted_iota(I32, (tm, ROUTE_LANES), 1)
    big = jnp.int32(ROUTE_LANES)
    is_g = lane < N_GROUPS
    gl = jnp.where(is_g, logits, NEG)
    gmax = jnp.max(gl, axis=-1, keepdims=True)
    gidx = jnp.min(jnp.where(gl == gmax, lane, big), axis=-1, keepdims=True)
    p_g = 1.0 / jnp.sum(jnp.where(is_g, jnp.exp(gl - gmax), 0.0), axis=-1, keepdims=True)
    in_grp = (lane >= N_GROUPS) & (lane < N_GROUPS + N_EXPERTS) & (
        lax.shift_right_logical(lane - N_GROUPS, jnp.full(lane.shape, 3, I32)) == gidx)
    el = jnp.where(in_grp, logits, NEG)
    e1v = jnp.max(el, axis=-1, keepdims=True)
    e1i = jnp.min(jnp.where(el == e1v, lane, big), axis=-1, keepdims=True)
    el2 = jnp.where(lane == e1i, NEG, el)
    e2v = jnp.max(el2, axis=-1, keepdims=True)
    e2i = jnp.min(jnp.where(el2 == e2v, lane, big), axis=-1, keepdims=True)
    t = jnp.exp(e2v - e1v)
    w1 = p_g / (1.0 + t)
    w2 = w1 * t
    sel1, sel2 = lane == e1i, lane == e2i
    member = jnp.where(sel1 | sel2, 1.0, 0.0)
    rr = lax.broadcasted_iota(I32, (tm, tm), 0)
    cc_i = lax.broadcasted_iota(I32, (tm, tm), 1)
    tri = jnp.where(cc_i < rr, 1.0, 0.0).astype(BF16)
    prefix = jnp.dot(tri, member.astype(BF16), preferred_element_type=F32) + cnt_ref[...]
    rank1 = jnp.sum(jnp.where(sel1, prefix, 0.0), axis=-1, keepdims=True)
    rank2 = jnp.sum(jnp.where(sel2, prefix, 0.0), axis=-1, keepdims=True)
    cnt_ref[...] = cnt_ref[...] + jnp.sum(member, axis=0, keepdims=True)
    key_scale = float(1 << RANK_BITS)
    key1 = (e1i - N_GROUPS).astype(F32) * key_scale + rank1
    key2 = (e2i - N_GROUPS).astype(F32) * key_scale + rank2
    route = jnp.zeros((tm, ROUTE_LANES), F32)
    for k, val in enumerate((key1, key2, w1, w2)):
        route = jnp.where(lane == k, val, route)
    route_ref[...] = route


def _post(wp, attn, gates, x, g1, sh2, sc2, norm2_g, cc, cs, wfo_b, wao_b, wout_b, wr, br,
          counts_in):
    b, s, d = x.shape
    t = b * s
    tm = ROW_TILE
    per_b_tiles = s // tm
    flat = lambda a: a.reshape(t, a.shape[-1])
    row = lambda w: pl.BlockSpec((tm, w), lambda i: (i, 0))
    per_b = pl.BlockSpec((None, 1, d), lambda i: (i // per_b_tiles, 0, 0))
    const = lambda a: _resident(a.shape, lambda i: (0,) * a.ndim)
    fw = wp.shape[-1]
    aw = attn.shape[-1]
    return pl.pallas_call(
        _post_kernel,
        grid=(t // tm,),
        in_specs=[row(fw), row(aw), row(2 * d), row(d), per_b, per_b, per_b,
                  const(norm2_g), const(cc), const(cs), const(wfo_b), const(wao_b), const(wout_b),
                  const(wr), const(br), const(counts_in)],
        out_specs=[row(d), pl.BlockSpec((tm * ROW_SUB, LANES), lambda i: (i, 0)), row(ROUTE_LANES),
                   pl.BlockSpec((1, ROUTE_LANES), lambda i: (0, 0))],
        out_shape=[jax.ShapeDtypeStruct((t, d), F32), jax.ShapeDtypeStruct((t * ROW_SUB, LANES), I32),
                   jax.ShapeDtypeStruct((t, ROUTE_LANES), F32), jax.ShapeDtypeStruct((1, ROUTE_LANES), F32)],
        compiler_params=_cparams(("arbitrary",)),
        name="post_router",
    )(flat(wp), flat(attn), flat(gates), flat(x), g1, sh2, sc2, norm2_g, cc, cs, wfo_b, wao_b, wout_b, wr, br,
      counts_in)


def _sortrows_kernel(dest_ref, pstart_ref, pend_ref, *refs, tm, rows, tile_starts):
    n_groups = len(tile_starts)
    h_refs = refs[:n_groups]
    xs_hbm, stage, zbuf, sem, zsem = refs[n_groups:]
    j = pl.program_id(0)
    n = pl.num_programs(0)
    slot = j % 2
    blk_sub = rows * ROW_SUB

    def wait_slot(sl):
        nsub = TOP_K * tm * ROW_SUB
        pltpu.make_async_copy(stage.at[0, pl.ds(0, nsub), :], xs_hbm.at[pl.ds(0, nsub), :], sem.at[sl]).wait()

    @pl.when(j == 0)
    def _():
        zbuf[...] = jnp.zeros_like(zbuf)

        def zero_block(off):
            return pltpu.make_async_copy(zbuf, xs_hbm.at[pl.ds(pl.multiple_of(off * ROW_SUB, blk_sub), blk_sub), :],
                                         zsem)

        def per_expert(fn):
            def body(e, carry):
                @pl.when(pend_ref[e] > pstart_ref[e])
                def _():
                    fn(zero_block(pend_ref[e] - rows))
                return carry
            lax.fori_loop(0, N_EXPERTS, body, 0)

        n_used = pend_ref[N_EXPERTS - 1] // rows
        n_blocks = xs_hbm.shape[0] // blk_sub

        def tail(fn):
            lax.fori_loop(n_used, n_blocks, lambda b, c: (fn(zero_block(b * rows)), c)[1], 0)

        per_expert(lambda cp: cp.start())
        tail(lambda cp: cp.start())
        per_expert(lambda cp: cp.wait())
        tail(lambda cp: cp.wait())

    @pl.when(j >= 2)
    def _():
        wait_slot(slot)

    tile = h_refs[0][...]
    for g in range(1, n_groups):
        tile = jnp.where(j >= tile_starts[g], h_refs[g][...], tile)
    stage[slot, pl.ds(0, tm * ROW_SUB), :] = tile

    def body(gi, carry):
        r0 = gi * DMA_GROUP
        base = (j * tm + r0) * TOP_K
        dests = [dest_ref[base + q] for q in range(DMA_GROUP * TOP_K)]
        for q, dest in enumerate(dests):
            pltpu.make_async_copy(_tile_row(stage.at[slot], r0 + q // TOP_K), _tile_row(xs_hbm, dest),
                                  sem.at[slot]).start(priority=q % DMA_QUEUES)
        return carry

    lax.fori_loop(0, tm // DMA_GROUP, body, 0)

    @pl.when(j == n - 1)
    def _():
        @pl.when(j >= 1)
        def _():
            wait_slot(1 - slot)

        wait_slot(slot)


def _sortrows(h2_groups, dest, pad_start, pad_end, n_rows):
    tm = SORT_TILE
    tiles = [h.shape[0] // (ROW_SUB * tm) for h in h2_groups]
    starts = [sum(tiles[:g]) for g in range(len(tiles))]
    spec = lambda g: pl.BlockSpec((tm * ROW_SUB, LANES),
                                  lambda j, *_: (jnp.clip(j - starts[g], 0, tiles[g] - 1), 0))
    return pl.pallas_call(
        functools.partial(_sortrows_kernel, tm=tm, rows=MOE_ROWS, tile_starts=tuple(starts)),
        grid_spec=pltpu.PrefetchScalarGridSpec(
            num_scalar_prefetch=3,
            grid=(sum(tiles),),
            in_specs=[spec(g) for g in range(len(tiles))],
            out_specs=pl.BlockSpec(memory_space=pl.ANY),
            scratch_shapes=[pltpu.VMEM((2, TOP_K * tm * ROW_SUB, LANES), I32),
                            pltpu.VMEM((MOE_ROWS * ROW_SUB, LANES), I32),
                            pltpu.SemaphoreType.DMA((2,)), pltpu.SemaphoreType.DMA(())]),
        out_shape=jax.ShapeDtypeStruct((n_rows * ROW_SUB, LANES), I32),
        compiler_params=_cparams(("arbitrary",), disable_bounds_checks=True, has_side_effects=True),
        name="sort_rows",
    )(dest, pad_start, pad_end, *h2_groups)


def _expert_kernel(pos_ref, seq_ref, nused_ref, xs_ref, w1_hbm, w3_hbm, w2_hbm, y_ref,
                   w1f, w3f, w2f, wsem, w1b, w3b, w2b, *, rows):
    j = pl.program_id(0)
    weights = ((w1_hbm, w1f), (w3_hbm, w3f), (w2_hbm, w2f))

    def start_weights(k, buf):
        e = seq_ref[k]
        for src, dst in weights:
            n = src.shape[1] // WEIGHT_CHUNKS
            for c in range(WEIGHT_CHUNKS):
                pltpu.make_async_copy(src.at[e, pl.ds(c * n, n), :], dst.at[buf, pl.ds(c * n, n), :],
                                      wsem.at[buf]).start(priority=1)

    def wait_weights(k, buf):
        e = seq_ref[k]
        for src, dst in weights:
            pltpu.make_async_copy(src.at[e], dst.at[buf], wsem.at[buf]).wait()

    @pl.when(j == 0)
    def _():
        start_weights(0, 0)

    pos = pos_ref[j]
    pos_prev = pos_ref[jnp.maximum(j - 1, 0)]

    @pl.when((j == 0) | (pos != pos_prev))
    def _():
        buf = pos % 2
        wait_weights(pos, buf)

        @pl.when(pos + 1 < nused_ref[1])
        def _():
            start_weights(pos + 1, 1 - buf)

        w1b[...] = w1f[buf].astype(BF16)
        w3b[...] = w3f[buf].astype(BF16)
        w2b[...] = w2f[buf].astype(BF16)

    @pl.when(j < nused_ref[0])
    def _():
        hi, lo = _unpack_pair(_load_tile_rows(xs_ref, 0, rows))
        xb = jnp.concatenate([hi.astype(BF16), lo.astype(BF16)], axis=1)
        a = jnp.dot(xb, w1b[...], preferred_element_type=F32)
        g = jnp.dot(xb, w3b[...], preferred_element_type=F32)
        hid = (a * _sigmoid(a) * g).astype(BF16)
        y = jnp.dot(hid, w2b[...], preferred_element_type=F32)
        half = y.shape[1] // 2
        _store_tile_rows(y_ref, _pack_pair(y[:, :half], y[:, half:]))

    @pl.when(j >= nused_ref[0])
    def _():
        y_ref[...] = jnp.zeros_like(y_ref)


def _experts(xs, block_pos, expert_seq, n_used, w1, w3, w2):
    rows = MOE_ROWS
    blk = rows * ROW_SUB
    d, f = w1.shape[1], w1.shape[2]
    hbm = pl.BlockSpec(memory_space=pl.ANY)
    return pl.pallas_call(
        functools.partial(_expert_kernel, rows=rows),
        grid_spec=pltpu.PrefetchScalarGridSpec(
            num_scalar_prefetch=3,
            grid=(xs.shape[0] // blk,),
            in_specs=[pl.BlockSpec((blk, LANES), lambda j, bp, es, nu: (jnp.minimum(j, nu[0] - 1), 0)),
                      hbm, hbm, hbm],
            out_specs=pl.BlockSpec((blk, LANES), lambda j, *_: (j, 0)),
            scratch_shapes=[pltpu.VMEM((2, d, f), F32), pltpu.VMEM((2, d, f), F32), pltpu.VMEM((2, f, d), F32),
                            pltpu.SemaphoreType.DMA((2,)),
                            pltpu.VMEM((d, f), BF16), pltpu.VMEM((d, f), BF16), pltpu.VMEM((f, d), BF16)]),
        out_shape=jax.ShapeDtypeStruct(xs.shape, I32),
        compiler_params=_cparams(("arbitrary",)),
        name="experts",
    )(block_pos, expert_seq, n_used, xs, w1, w3, w2)


def _combine_kernel(dest_ref, y_hbm, x1_ref, route_ref, g2_ref, o_ref, ybuf, sem, *, tm):
    i = pl.program_id(0)
    n = pl.num_programs(0)
    slot = i % 2

    def start_gather(tile, sl):
        def body(gi, carry):
            r0 = gi * DMA_GROUP
            base = (tile * tm + r0) * TOP_K
            dests = [dest_ref[base + q] for q in range(DMA_GROUP * TOP_K)]
            for q, dest in enumerate(dests):
                pltpu.make_async_copy(_tile_row(y_hbm, dest),
                                      _tile_row(ybuf.at[sl], (q % TOP_K) * tm + r0 + q // TOP_K),
                                      sem.at[sl]).start(priority=q % DMA_QUEUES)
            return carry

        lax.fori_loop(0, tm // DMA_GROUP, body, 0)

    @pl.when(i == 0)
    def _():
        start_gather(0, 0)

    @pl.when(i + 1 < n)
    def _():
        start_gather(i + 1, 1 - slot)

    pltpu.make_async_copy(y_hbm.at[pl.ds(0, TOP_K * tm * ROW_SUB), :], ybuf.at[slot], sem.at[slot]).wait()
    route = route_ref[...]
    w1 = route[:, 2:3]
    w2 = route[:, 3:4]
    hi1, lo1 = _unpack_pair(_load_tile_rows(ybuf, 0, tm, lead=(slot,)))
    hi2, lo2 = _unpack_pair(_load_tile_rows(ybuf, tm, tm, lead=(slot,)))
    half = o_ref.shape[1] // 2
    o_ref[:, :half] = x1_ref[:, :half] + g2_ref[:, :half] * (w1 * hi1 + w2 * hi2)
    o_ref[:, half:] = x1_ref[:, half:] + g2_ref[:, half:] * (w1 * lo1 + w2 * lo2)


def _combine(y, dest, x1, route, g2, s):
    t, d = x1.shape
    tm = ROW_TILE
    per_b_tiles = s // tm
    return pl.pallas_call(
        functools.partial(_combine_kernel, tm=tm),
        grid_spec=pltpu.PrefetchScalarGridSpec(
            num_scalar_prefetch=1,
            grid=(t // tm,),
            in_specs=[pl.BlockSpec(memory_space=pl.ANY),
                      pl.BlockSpec((tm, d), lambda i, ds: (i, 0)),
                      pl.BlockSpec((tm, ROUTE_LANES), lambda i, ds: (i, 0)),
                      pl.BlockSpec((None, 1, d), lambda i, ds: (i // per_b_tiles, 0, 0))],
            out_specs=pl.BlockSpec((tm, d), lambda i, ds: (i, 0)),
            scratch_shapes=[pltpu.VMEM((2, TOP_K * tm * ROW_SUB, LANES), I32), pltpu.SemaphoreType.DMA((2,))]),
        out_shape=jax.ShapeDtypeStruct((t, d), F32),
        compiler_params=_cparams(("arbitrary",), disable_bounds_checks=True),
        name="combine",
    )(dest, y, x1, route, g2)


def _dispatch_plan(route_keys, counts, t):
    rows = MOE_ROWS
    keys = route_keys.astype(I32)
    expert = lax.shift_right_logical(keys, jnp.full(keys.shape, RANK_BITS, I32))
    rank = keys & ((1 << RANK_BITS) - 1)
    cnt = counts[0, N_GROUPS:N_GROUPS + N_EXPERTS].astype(I32)
    padded = (cnt + rows - 1) // rows * rows
    pad_end = jnp.cumsum(padded)
    pad_start = pad_end - padded
    onehot = expert[..., None] == jnp.arange(N_EXPERTS, dtype=I32)
    dest = (jnp.sum(jnp.where(onehot, pad_start, 0), axis=-1) + rank).reshape(t * TOP_K)
    n_blocks = (t * TOP_K + N_EXPERTS * (rows - 1) + rows - 1) // rows
    n_used = pad_end[-1:] // rows
    blk = jnp.minimum(jnp.arange(n_blocks, dtype=I32), n_used - 1) * rows
    block_e = jnp.sum((pad_end[None, :] <= blk[:, None]).astype(I32), axis=1)
    owns = cnt > 0
    expert_seq = jnp.argsort(jnp.logical_not(owns), stable=True).astype(I32)
    seq_index = jnp.cumsum(owns.astype(I32)) - 1
    block_pos = jnp.sum(jnp.where(block_e[:, None] == jnp.arange(N_EXPERTS, dtype=I32), seq_index, 0), axis=1)
    used = jnp.concatenate([n_used, jnp.sum(owns.astype(I32), keepdims=True)]).astype(I32)
    return dest, pad_start, pad_end, block_pos.astype(I32), expert_seq, used, n_blocks * rows


def _layer(xs, mods, p):
    d = xs[0].shape[-1]
    t_all = sum(x.shape[0] * x.shape[1] for x in xs)
    counts = jnp.zeros((1, ROUTE_LANES), F32)
    rope = _rope_tables(max(x.shape[1] for x in xs))
    per_group, h2_groups, tok_off = [], [], 0
    for x, mod in zip(xs, mods):
        b, s, _ = x.shape
        sh1, sc1, g1, sh2, sc2, g2 = [m.reshape(b, 1, d) for m in jnp.split(mod, 6, axis=-1)]
        u, qkv, gates = _inproj(x, sh1, sc1, p["norm1_g"], p["q_norm_g"], p["k_norm_g"], p["w_in"], rope)
        wp = _seq_fft(u)
        attn = _attention(qkv, p["sink"])
        x1, h2, route, counts = _post(wp, attn, gates, x, g1, sh2, sc2, p["norm2_g"], p["cc"], p["cs"],
                                      p["w_fourier_out"], p["w_attn_out"], p["w_out"],
                                      p["w_router"], p["b_router"], counts)
        per_group.append((x1, route, g2, tok_off))
        h2_groups.append(h2)
        tok_off += b * s
    route_keys = jnp.concatenate([r[:, 0:TOP_K] for _, r, _, _ in per_group], axis=0)
    dest, pad_start, pad_end, block_pos, expert_seq, n_used, n_rows = _dispatch_plan(route_keys, counts, t_all)
    xs_sorted = _sortrows(h2_groups, dest, pad_start, pad_end, n_rows)
    y = _experts(xs_sorted, block_pos, expert_seq, n_used, p["w1"], p["w3"], p["w2"])
    outs = []
    for x, (x1, route, g2, off) in zip(xs, per_group):
        b, s, _ = x.shape
        dest_g = lax.slice(dest, (off * TOP_K,), ((off + b * s) * TOP_K,))
        outs.append(_combine(y, dest_g, x1, route, g2, s).reshape(b, s, d))
    return outs


def _channel_dft():
    n = FOURIER_GROUP_DIM
    k = np.arange(n, dtype=np.int64)
    ang = ((k[:, None] * k[None, :]) % n) * (2.0 * math.pi / n)
    return (jnp.asarray((np.cos(ang) * n ** -0.5).astype(BF16)),
            jnp.asarray((np.sin(ang) * n ** -0.5).astype(BF16)))


def kernel(x_prompt, x_sample, c_prompt, c_sample, w_ada, b_ada, norm1_g, w_in, q_norm_g, k_norm_g, sink,
           w_fourier_out, w_attn_out, w_out, norm2_g, w_group, b_group, w_expert, b_expert, w1, w3, w2):
    depth = w_ada.shape[0]
    d = x_prompt.shape[-1]
    bp = c_prompt.shape[0]
    bs = c_sample.shape[0]
    cc, cs = _channel_dft()
    xp, xs = x_prompt, x_sample
    for l in range(depth):
        c_all = jnp.concatenate([c_prompt, c_sample, jnp.zeros((8 - (bp + bs) % 8, d), F32)], axis=0)
        mod = _adaln(c_all, w_ada[l], b_ada[l])
        pad = ROUTE_LANES - N_GROUPS - N_EXPERTS
        w_router = jnp.concatenate([w_group[l], w_expert[l], jnp.zeros((d, pad), F32)], axis=1)
        w_router_hi = w_router.astype(BF16)
        p = {
            "norm1_g": norm1_g[l].reshape(1, d), "norm2_g": norm2_g[l].reshape(1, d),
            "q_norm_g": q_norm_g[l].reshape(1, HEAD_DIM), "k_norm_g": k_norm_g[l].reshape(1, HEAD_DIM),
            "sink": sink[l], "w_in": w_in[l].astype(BF16),
            "w_fourier_out": w_fourier_out[l].astype(BF16), "w_attn_out": w_attn_out[l].astype(BF16),
            "w_out": w_out[l].astype(BF16), "cc": cc, "cs": cs,
            "w_router": jnp.concatenate(
                [w_router_hi, (w_router - w_router_hi.astype(F32)).astype(BF16)], axis=1),
            "b_router": jnp.concatenate([b_group[l], b_expert[l], jnp.zeros((pad,), F32)]).reshape(1, ROUTE_LANES),
            "w1": w1[l], "w3": w3[l], "w2": w2[l],
        }
        xp, xs = _layer([xp, xs], [mod[:bp], mod[bp:bp + bs]], p)
    return xp, xs
```

```python
import functools
import math

import jax
import jax.numpy as jnp
import numpy as np
from jax import lax
from jax.experimental import pallas as pl
from jax.experimental.pallas import tpu as pltpu

F32 = jnp.float32
BF16 = jnp.bfloat16
I32 = jnp.int32

HEAD_DIM = 128
N_KV_HEADS = 2
Q_PER_KV = 4
N_HEADS = N_KV_HEADS * Q_PER_KV
ROT_DIM = 32
ROPE_THETA = 500000.0
WINDOW = 128
FOURIER_GROUPS = 4
FOURIER_GROUP_DIM = 256
N_GROUPS = 8
EXPERTS_PER_GROUP = 8
N_EXPERTS = N_GROUPS * EXPERTS_PER_GROUP
TOP_K = 2
EPS = 1e-6

LANES = 128
SUBLANES = 8
V7X_VMEM_LIMIT = 56 * 1024 * 1024

ROW_TILE = 256
SORT_TILE = 512
ATTN_TILE = 1024
FFT_N1 = 128
FFT_UNROLL = 16
FFT_PITCH_PAD = SUBLANES
MOE_ROWS = 512
ROUTE_LANES = 128
RANK_BITS = 17
DMA_GROUP = 8
WEIGHT_CHUNKS = 8
DMA_QUEUES = 2
NEG = -1e30


def _cparams(sem, **kw):
    return pltpu.CompilerParams(dimension_semantics=sem, vmem_limit_bytes=V7X_VMEM_LIMIT, **kw)


def _resident(shape, index_map):
    return pl.BlockSpec(shape, index_map, pipeline_mode=pl.Buffered(1))


def _sigmoid(x):
    return 0.5 * jnp.tanh(0.5 * x) + 0.5


def _adaln_kernel(c_ref, w_ref, b_ref, o_ref):
    c = c_ref[...]
    s = c * _sigmoid(c)
    o_ref[...] = jnp.dot(s, w_ref[...], precision=lax.Precision.HIGHEST,
                         preferred_element_type=F32) + b_ref[...]


def _adaln(c, w_ada, b_ada):
    r, d = c.shape
    n = w_ada.shape[1]
    tn = 1024
    return pl.pallas_call(
        _adaln_kernel,
        grid=(n // tn,),
        in_specs=[pl.BlockSpec((r, d), lambda j: (0, 0)),
                  pl.BlockSpec((d, tn), lambda j: (0, j)),
                  pl.BlockSpec((1, tn), lambda j: (0, j))],
        out_specs=pl.BlockSpec((r, tn), lambda j: (0, j)),
        out_shape=jax.ShapeDtypeStruct((r, n), F32),
        compiler_params=_cparams(("parallel",)),
        name="adaln",
    )(c, w_ada, b_ada.reshape(1, n))


def _rope_tables(s):
    half = ROT_DIM // 2
    inv_freq = ROPE_THETA ** (-jnp.arange(0, ROT_DIM, 2, dtype=F32) / ROT_DIM)
    ang = jnp.arange(s, dtype=F32)[:, None] * inv_freq[None, :]
    cos, sin = jnp.cos(ang), jnp.sin(ang)
    pad = jnp.zeros((s, LANES - ROT_DIM), F32)
    cos_f = jnp.concatenate([cos, cos, jnp.ones((s, LANES - ROT_DIM), F32)], axis=1)
    sin_up = jnp.concatenate([-sin, jnp.zeros((s, half), F32), pad], axis=1)
    sin_dn = jnp.concatenate([jnp.zeros((s, half), F32), sin, pad], axis=1)
    return cos_f, sin_up, sin_dn


def _inproj_kernel(x_ref, sh_ref, sc_ref, g_ref, qg_ref, kg_ref, cos_ref, sup_ref, sdn_ref, w_ref,
                   u_ref, qkv_ref, gate_ref, *, fw, aw, kvw):
    x = x_ref[...]
    ms = jnp.mean(x * x, axis=-1, keepdims=True)
    h = (x * lax.rsqrt(ms + EPS) * g_ref[...]) * (1.0 + sc_ref[...]) + sh_ref[...]
    hb = h.astype(BF16)
    chunk = 512
    for c0 in range(0, fw, chunk):
        u_ref[:, c0:c0 + chunk] = jnp.dot(hb, w_ref[:, c0:c0 + chunk], preferred_element_type=F32)
    cos_f, sin_up, sin_dn = cos_ref[...], sup_ref[...], sdn_ref[...]
    half = ROT_DIM // 2
    scale = HEAD_DIM ** -0.5
    for c0 in list(range(0, aw, chunk)) + [aw]:
        width = chunk if c0 < aw else kvw
        acc = jnp.dot(hb, w_ref[:, fw + c0:fw + c0 + width], preferred_element_type=F32)
        for hh in range(width // HEAD_DIM):
            col = c0 + hh * HEAD_DIM
            t = acc[:, hh * HEAD_DIM:(hh + 1) * HEAD_DIM]
            is_q = col < aw
            gain = qg_ref[...] if is_q else kg_ref[...]
            t = t * lax.rsqrt(jnp.mean(t * t, axis=-1, keepdims=True) + EPS) * gain
            t = (t * cos_f + pltpu.roll(t, LANES - half, axis=1) * sin_up
                 + pltpu.roll(t, half, axis=1) * sin_dn)
            if is_q:
                t = t * scale
            qkv_ref[:, col:col + HEAD_DIM] = t.astype(BF16)
    v0 = fw + aw + kvw
    qkv_ref[:, aw + kvw:aw + 2 * kvw] = jnp.dot(
        hb, w_ref[:, v0:v0 + kvw], preferred_element_type=F32).astype(BF16)
    g0 = v0 + kvw
    gw = gate_ref.shape[-1]
    for c0 in range(0, gw, chunk):
        gate_ref[:, c0:c0 + chunk] = jnp.dot(
            hb, w_ref[:, g0 + c0:g0 + c0 + chunk], preferred_element_type=F32).astype(BF16)


def _inproj(x, shift, scale, norm_g, q_g, k_g, w_in_b, tables):
    b, s, d = x.shape
    fw = FOURIER_GROUPS * FOURIER_GROUP_DIM
    aw = N_HEADS * HEAD_DIM
    kvw = N_KV_HEADS * HEAD_DIM
    gw = 2 * d
    tm = ROW_TILE
    row = lambda w: pl.BlockSpec((None, tm, w), lambda bi, i: (bi, i, 0))
    per_b = pl.BlockSpec((None, 1, d), lambda bi, i: (bi, 0, 0))
    const = lambda w: pl.BlockSpec((1, w), lambda bi, i: (0, 0))
    tab = pl.BlockSpec((tm, LANES), lambda bi, i: (i, 0))
    return pl.pallas_call(
        functools.partial(_inproj_kernel, fw=fw, aw=aw, kvw=kvw),
        grid=(b, s // tm),
        in_specs=[row(d), per_b, per_b, const(d), const(HEAD_DIM), const(HEAD_DIM), tab, tab, tab,
                  _resident(w_in_b.shape, lambda bi, i: (0, 0))],
        out_specs=[row(fw), row(aw + 2 * kvw), row(gw)],
        out_shape=[jax.ShapeDtypeStruct((b, s, fw), F32),
                   jax.ShapeDtypeStruct((b, s, aw + 2 * kvw), BF16),
                   jax.ShapeDtypeStruct((b, s, gw), BF16)],
        compiler_params=_cparams(("parallel", "parallel")),
        name="inproj",
    )(x, shift, scale, norm_g, q_g, k_g, *tables, w_in_b)


def _pack_pair(a, b):
    ab = lax.bitcast_convert_type(a.astype(BF16).astype(F32), I32)
    bb = lax.bitcast_convert_type(b.astype(BF16).astype(F32), I32)
    return ab | lax.shift_right_logical(bb, jnp.full(bb.shape, 16, I32))


def _unpack_pair(p):
    hi = lax.bitcast_convert_type(p & jnp.int32(-65536), F32)
    lo = lax.bitcast_convert_type(lax.shift_left(p, jnp.full(p.shape, 16, I32)), F32)
    return hi, lo


def _fft_tables(s):
    n1 = FFT_N1
    n2 = s // n1
    k1 = np.arange(n1, dtype=np.int64)
    nn = (n2 * np.arange(n1, dtype=np.int64))[None, None, :] + np.arange(n2, dtype=np.int64)[:, None, None]
    ang = ((k1[None, :, None] * nn) % s) * (2.0 * math.pi / s)
    sc1 = n1 ** -0.5
    g = np.concatenate([np.cos(ang) * sc1, -np.sin(ang) * sc1], axis=1).astype(BF16)
    k2 = np.arange(n2, dtype=np.int64)
    ang2 = ((k2[:, None] * k2[None, :]) % n2) * (2.0 * math.pi / n2)
    c2, s2 = np.cos(ang2) * n2 ** -0.5, np.sin(ang2) * n2 ** -0.5
    f2 = np.concatenate([np.concatenate([c2, s2], axis=1),
                         np.concatenate([-s2, c2], axis=1)], axis=0).astype(BF16)
    return jnp.asarray(g), jnp.asarray(f2)


def _fft_kernel(u_ref, g_ref, f2_ref, o_ref, y_scr, *, n1, n2, unroll):
    pitch = n1 + FFT_PITCH_PAD

    def stage1(i, carry):
        for uu in range(unroll):
            m = i * unroll + uu
            xm = u_ref[pl.ds(m, n1, stride=n2), :].astype(BF16)
            y = jnp.dot(g_ref[m], xm, preferred_element_type=F32)
            y_scr[pl.ds(pl.multiple_of(m * pitch, SUBLANES), n1), :] = _pack_pair(y[:n1], y[n1:])
        return carry

    lax.fori_loop(0, n2 // unroll, stage1, 0)
    f2 = f2_ref[...]

    def stage2(i, carry):
        for uu in range(unroll):
            k1 = i * unroll + uu
            yr, yi = _unpack_pair(y_scr[pl.ds(k1, n2, stride=pitch), :])
            rhs = jnp.concatenate([yr.astype(BF16), yi.astype(BF16)], axis=0)
            z = jnp.dot(f2, rhs, preferred_element_type=F32)
            y_scr[pl.ds(k1, n2, stride=pitch), :] = _pack_pair(z[:n2], z[n2:])
        return carry

    lax.fori_loop(0, n1 // unroll, stage2, 0)

    def compact(i, carry):
        for uu in range(unroll):
            k2 = i * unroll + uu
            o_ref[pl.ds(pl.multiple_of(k2 * n1, n1), n1), :] = (
                y_scr[pl.ds(pl.multiple_of(k2 * pitch, SUBLANES), n1), :])
        return carry

    lax.fori_loop(0, n2 // unroll, compact, 0)


def _seq_fft(u):
    b, s, c = u.shape
    n1 = FFT_N1
    n2 = s // n1
    g, f2 = _fft_tables(s)
    unroll = min(FFT_UNROLL, n2)
    blk = pl.BlockSpec((None, s, LANES), lambda bi, j: (bi, 0, j))
    return pl.pallas_call(
        functools.partial(_fft_kernel, n1=n1, n2=n2, unroll=unroll),
        grid=(b, c // LANES),
        in_specs=[blk, _resident(g.shape, lambda bi, j: (0, 0, 0)), _resident(f2.shape, lambda bi, j: (0, 0))],
        out_specs=blk,
        out_shape=jax.ShapeDtypeStruct((b, s, c), I32),
        scratch_shapes=[pltpu.VMEM((n2 * (n1 + FFT_PITCH_PAD), LANES), I32)],
        compiler_params=_cparams(("parallel", "parallel")),
        name="seq_fft",
    )(u, g, f2)


def _attn_kernel(sink_ref, q_ref, kp_ref, kc_ref, kn_ref, vp_ref, vc_ref, vn_ref, o_ref,
                 s_scr, p_scr, inv_scr, *, tq):
    i = pl.program_id(1)
    first = i == 0
    last = i == pl.num_programs(1) - 1
    nsub = tq // WINDOW
    nq = Q_PER_KV * WINDOW
    nk = 3 * WINDOW
    c = lax.broadcasted_iota(I32, (nk, nq), 0)
    r = lax.broadcasted_iota(I32, (nk, nq), 1) & (WINDOW - 1)
    band = ((c >= r) & (c < WINDOW)) | ((c >= WINDOW) & (c < 2 * WINDOW)) | ((c >= 2 * WINDOW) & (c - 2 * WINDOW <= r))
    pairs = [(h, sb) for h in range(N_KV_HEADS) for sb in range(nsub)]
    kcat, vt = [], []
    for h in range(N_KV_HEADS):
        hs = slice(h * HEAD_DIM, (h + 1) * HEAD_DIM)
        kcat.append(jnp.concatenate([kp_ref[:, hs], kc_ref[:, hs], kn_ref[:, hs]], axis=0))
        vcat = jnp.concatenate([vp_ref[:, hs], vc_ref[:, hs], vn_ref[:, hs]], axis=0)
        vt.append(vcat.astype(F32).T.astype(BF16))

    for n, (h, sb) in enumerate(pairs):
        q4 = jnp.concatenate(
            [q_ref[sb * WINDOW:(sb + 1) * WINDOW,
                   (h * Q_PER_KV + g) * HEAD_DIM:(h * Q_PER_KV + g + 1) * HEAD_DIM]
             for g in range(Q_PER_KV)], axis=0)
        kw = kcat[h][sb * WINDOW:(sb + 3) * WINDOW]
        s_scr[n] = lax.dot_general(kw, q4, (((1,), (1,)), ((), ())), preferred_element_type=F32)

    for n, (h, sb) in enumerate(pairs):
        sc = jnp.where(band, s_scr[n], NEG)
        if sb == 0:
            sc = jnp.where((c < WINDOW) & first, NEG, sc)
        if sb == nsub - 1:
            sc = jnp.where((c >= 2 * WINDOW) & last, NEG, sc)
        sink = jnp.concatenate(
            [jnp.full((1, WINDOW), sink_ref[h * Q_PER_KV + g], F32) for g in range(Q_PER_KV)], axis=1)
        m = jnp.maximum(jnp.max(sc, axis=0, keepdims=True), sink)
        p = jnp.exp(sc - m)
        denom = jnp.sum(p, axis=0, keepdims=True) + jnp.exp(sink - m)
        p_scr[n] = p.astype(BF16)
        inv_scr[pl.ds(n, 1), :] = 1.0 / denom

    for n, (h, sb) in enumerate(pairs):
        ot = jnp.dot(vt[h][:, sb * WINDOW:(sb + 3) * WINDOW], p_scr[n], preferred_element_type=F32)
        o = (ot * inv_scr[pl.ds(n, 1), :]).T
        for g in range(Q_PER_KV):
            col = (h * Q_PER_KV + g) * HEAD_DIM
            o_ref[sb * WINDOW:(sb + 1) * WINDOW, col:col + HEAD_DIM] = (
                o[g * WINDOW:(g + 1) * WINDOW].astype(BF16))


def _attention(qkv, sink):
    b, s, _ = qkv.shape
    aw = N_HEADS * HEAD_DIM
    kvw = N_KV_HEADS * HEAD_DIM
    tq = min(ATTN_TILE, s)
    per = tq // WINDOW
    nblk = s // WINDOW
    kcol = aw // kvw
    vcol = kcol + 1
    halo = lambda col, off: pl.BlockSpec(
        (None, WINDOW, kvw),
        lambda bi, i, sk: (bi, jnp.clip(i * per + off, 0, nblk - 1), col))
    main = lambda col: pl.BlockSpec((None, tq, kvw), lambda bi, i, sk: (bi, i, col))
    return pl.pallas_call(
        functools.partial(_attn_kernel, tq=tq),
        grid_spec=pltpu.PrefetchScalarGridSpec(
            num_scalar_prefetch=1,
            grid=(b, s // tq),
            in_specs=[pl.BlockSpec((None, tq, aw), lambda bi, i, sk: (bi, i, 0)),
                      halo(kcol, -1), main(kcol), halo(kcol, per),
                      halo(vcol, -1), main(vcol), halo(vcol, per)],
            out_specs=pl.BlockSpec((None, tq, aw), lambda bi, i, sk: (bi, i, 0)),
            scratch_shapes=[pltpu.VMEM((N_KV_HEADS * per, 3 * WINDOW, Q_PER_KV * WINDOW), F32),
                            pltpu.VMEM((N_KV_HEADS * per, 3 * WINDOW, Q_PER_KV * WINDOW), BF16),
                            pltpu.VMEM((N_KV_HEADS * per, Q_PER_KV * WINDOW), F32)]),
        out_shape=jax.ShapeDtypeStruct((b, s, aw), BF16),
        compiler_params=_cparams(("parallel", "parallel")),
        name="attention",
    )(sink, qkv, qkv, qkv, qkv, qkv, qkv, qkv)


ROW_WORDS = 1024
ROW_SUB = ROW_WORDS // LANES


def _store_tile_rows(ref, val, base=0, lead=()):
    r = val.shape[0]
    for j in range(ROW_SUB):
        ref[lead + (pl.ds(base * ROW_SUB + j, r, stride=ROW_SUB), slice(None))] = val[:, j * LANES:(j + 1) * LANES]


def _load_tile_rows(ref, base, r, lead=()):
    return jnp.concatenate(
        [ref[lead + (pl.ds(base * ROW_SUB + j, r, stride=ROW_SUB), slice(None))] for j in range(ROW_SUB)], axis=1)


def _tile_row(ref, row):
    return ref.at[pl.ds(pl.multiple_of(row * ROW_SUB, ROW_SUB), ROW_SUB), :]


def _post_kernel(wp_ref, o_ref, gate_ref, x_ref, g1_ref, sh2_ref, sc2_ref, n2g_ref, cc_ref, cs_ref,
                 wfo_ref, wao_ref, wout_ref, wr_ref, br_ref, cnt_in_ref,
                 x1_ref, h2_ref, route_ref, cnt_ref):
    tm = x_ref.shape[0]
    d = x_ref.shape[1]

    @pl.when(pl.program_id(0) == 0)
    def _():
        cnt_ref[...] = cnt_in_ref[...]

    re, im = _unpack_pair(wp_ref[...])
    re, im = re.astype(BF16), im.astype(BF16)
    cc, cs = cc_ref[...], cs_ref[...]
    gd = FOURIER_GROUP_DIM
    fm = jnp.concatenate(
        [(jnp.dot(re[:, g * gd:(g + 1) * gd], cc, preferred_element_type=F32)
          + jnp.dot(im[:, g * gd:(g + 1) * gd], cs, preferred_element_type=F32)).astype(BF16)
         for g in range(FOURIER_GROUPS)], axis=1)
    y_f = jnp.dot(fm, wfo_ref[...], preferred_element_type=F32)
    y_a = jnp.dot(o_ref[...], wao_ref[...], preferred_element_type=F32)
    merged = (_sigmoid(gate_ref[:, :d].astype(F32)) * y_f
              + _sigmoid(gate_ref[:, d:].astype(F32)) * y_a).astype(BF16)
    x1 = x_ref[...] + g1_ref[...] * jnp.dot(merged, wout_ref[...], preferred_element_type=F32)
    x1_ref[...] = x1
    ms = jnp.mean(x1 * x1, axis=-1, keepdims=True)
    h2 = (x1 * lax.rsqrt(ms + EPS) * n2g_ref[...]) * (1.0 + sc2_ref[...]) + sh2_ref[...]
    half = d // 2
    _store_tile_rows(h2_ref, _pack_pair(h2[:, :half], h2[:, half:]))

    h_hi = h2.astype(BF16)
    h_lo = (h2 - h_hi.astype(F32)).astype(BF16)
    wr = wr_ref[...]
    r_hi = jnp.dot(h_hi, wr, preferred_element_type=F32)
    r_lo = jnp.dot(h_lo, wr, preferred_element_type=F32)
    logits = ((r_hi[:, :ROUTE_LANES] + r_hi[:, ROUTE_LANES:])
              + (r_lo[:, :ROUTE_LANES] + r_lo[:, ROUTE_LANES:])) + br_ref[...]
    lane = lax.broadcasted_iota(I32, (tm, ROUTE_LANES), 1)
    big = jnp.int32(ROUTE_LANES)
    is_g = lane < N_GROUPS
    gl = jnp.where(is_g, logits, NEG)
    gmax = jnp.max(gl, axis=-1, keepdims=True)
    gidx = jnp.min(jnp.where(gl == gmax, lane, big), axis=-1, keepdims=True)
    p_g = 1.0 / jnp.sum(jnp.where(is_g, jnp.exp(gl - gmax), 0.0), axis=-1, keepdims=True)
    in_grp = (lane >= N_GROUPS) & (lane < N_GROUPS + N_EXPERTS) & (
        lax.shift_right_logical(lane - N_GROUPS, jnp.full(lane.shape, 3, I32)) == gidx)
    el = jnp.where(in_grp, logits, NEG)
    e1v = jnp.max(el, axis=-1, keepdims=True)
    e1i = jnp.min(jnp.where(el == e1v, lane, big), axis=-1, keepdims=True)
    el2 = jnp.where(lane == e1i, NEG, el)
    e2v = jnp.max(el2, axis=-1, keepdims=True)
    e2i = jnp.min(jnp.where(el2 == e2v, lane, big), axis=-1, keepdims=True)
    t = jnp.exp(e2v - e1v)
    w1 = p_g / (1.0 + t)
    w2 = w1 * t
    sel1, sel2 = lane == e1i, lane == e2i
    member = jnp.where(sel1 | sel2, 1.0, 0.0)
    rr = lax.broadcasted_iota(I32, (tm, tm), 0)
    cc_i = lax.broadcasted_iota(I32, (tm, tm), 1)
    tri = jnp.where(cc_i < rr, 1.0, 0.0).astype(BF16)
    prefix = jnp.dot(tri, member.astype(BF16), preferred_element_type=F32) + cnt_ref[...]
    rank1 = jnp.sum(jnp.where(sel1, prefix, 0.0), axis=-1, keepdims=True)
    rank2 = jnp.sum(jnp.where(sel2, prefix, 0.0), axis=-1, keepdims=True)
    cnt_ref[...] = cnt_ref[...] + jnp.sum(member, axis=0, keepdims=True)
    key_scale = float(1 << RANK_BITS)
    key1 = (e1i - N_GROUPS).astype(F32) * key_scale + rank1
    key2 = (e2i - N_GROUPS).astype(F32) * key_scale + rank2
    route = jnp.zeros((tm, ROUTE_LANES), F32)
    for k, val in enumerate((key1, key2, w1, w2)):
        route = jnp.where(lane == k, val, route)
    route_ref[...] = route


def _post(wp, attn, gates, x, g1, sh2, sc2, norm2_g, cc, cs, wfo_b, wao_b, wout_b, wr, br,
          counts_in):
    b, s, d = x.shape
    t = b * s
    tm = ROW_TILE
    per_b_tiles = s // tm
    flat = lambda a: a.reshape(t, a.shape[-1])
    row = lambda w: pl.BlockSpec((tm, w), lambda i: (i, 0))
    per_b = pl.BlockSpec((None, 1, d), lambda i: (i // per_b_tiles, 0, 0))
    const = lambda a: _resident(a.shape, lambda i: (0,) * a.ndim)
    fw = wp.shape[-1]
    aw = attn.shape[-1]
    return pl.pallas_call(
        _post_kernel,
        grid=(t // tm,),
        in_specs=[row(fw), row(aw), row(2 * d), row(d), per_b, per_b, per_b,
                  const(norm2_g), const(cc), const(cs), const(wfo_b), const(wao_b), const(wout_b),
                  const(wr), const(br), const(counts_in)],
        out_specs=[row(d), pl.BlockSpec((tm * ROW_SUB, LANES), lambda i: (i, 0)), row(ROUTE_LANES),
                   pl.BlockSpec((1, ROUTE_LANES), lambda i: (0, 0))],
        out_shape=[jax.ShapeDtypeStruct((t, d), F32), jax.ShapeDtypeStruct((t * ROW_SUB, LANES), I32),
                   jax.ShapeDtypeStruct((t, ROUTE_LANES), F32), jax.ShapeDtypeStruct((1, ROUTE_LANES), F32)],
        compiler_params=_cparams(("arbitrary",)),
        name="post_router",
    )(flat(wp), flat(attn), flat(gates), flat(x), g1, sh2, sc2, norm2_g, cc, cs, wfo_b, wao_b, wout_b, wr, br,
      counts_in)


def _sortrows_kernel(dest_ref, pstart_ref, pend_ref, *refs, tm, rows, tile_starts):
    n_groups = len(tile_starts)
    h_refs = refs[:n_groups]
    xs_hbm, stage, zbuf, sem, zsem = refs[n_groups:]
    j = pl.program_id(0)
    n = pl.num_programs(0)
    slot = j % 2
    blk_sub = rows * ROW_SUB

    def wait_slot(sl):
        nsub = TOP_K * tm * ROW_SUB
        pltpu.make_async_copy(stage.at[0, pl.ds(0, nsub), :], xs_hbm.at[pl.ds(0, nsub), :], sem.at[sl]).wait()

    @pl.when(j == 0)
    def _():
        zbuf[...] = jnp.zeros_like(zbuf)

        def zero_block(off):
            return pltpu.make_async_copy(zbuf, xs_hbm.at[pl.ds(pl.multiple_of(off * ROW_SUB, blk_sub), blk_sub), :],
                                         zsem)

        def per_expert(fn):
            def body(e, carry):
                @pl.when(pend_ref[e] > pstart_ref[e])
                def _():
                    fn(zero_block(pend_ref[e] - rows))
                return carry
            lax.fori_loop(0, N_EXPERTS, body, 0)

        n_used = pend_ref[N_EXPERTS - 1] // rows
        n_blocks = xs_hbm.shape[0] // blk_sub

        def tail(fn):
            lax.fori_loop(n_used, n_blocks, lambda b, c: (fn(zero_block(b * rows)), c)[1], 0)

        per_expert(lambda cp: cp.start())
        tail(lambda cp: cp.start())
        per_expert(lambda cp: cp.wait())
        tail(lambda cp: cp.wait())

    @pl.when(j >= 2)
    def _():
        wait_slot(slot)

    tile = h_refs[0][...]
    for g in range(1, n_groups):
        tile = jnp.where(j >= tile_starts[g], h_refs[g][...], tile)
    stage[slot, pl.ds(0, tm * ROW_SUB), :] = tile

    def body(gi, carry):
        r0 = gi * DMA_GROUP
        base = (j * tm + r0) * TOP_K
        dests = [dest_ref[base + q] for q in range(DMA_GROUP * TOP_K)]
        for q, dest in enumerate(dests):
            pltpu.make_async_copy(_tile_row(stage.at[slot], r0 + q // TOP_K), _tile_row(xs_hbm, dest),
                                  sem.at[slot]).start(priority=q % DMA_QUEUES)
        return carry

    lax.fori_loop(0, tm // DMA_GROUP, body, 0)

    @pl.when(j == n - 1)
    def _():
        @pl.when(j >= 1)
        def _():
            wait_slot(1 - slot)

        wait_slot(slot)


def _sortrows(h2_groups, dest, pad_start, pad_end, n_rows):
    tm = SORT_TILE
    tiles = [h.shape[0] // (ROW_SUB * tm) for h in h2_groups]
    starts = [sum(tiles[:g]) for g in range(len(tiles))]
    spec = lambda g: pl.BlockSpec((tm * ROW_SUB, LANES),
                                  lambda j, *_: (jnp.clip(j - starts[g], 0, tiles[g] - 1), 0))
    return pl.pallas_call(
        functools.partial(_sortrows_kernel, tm=tm, rows=MOE_ROWS, tile_starts=tuple(starts)),
        grid_spec=pltpu.PrefetchScalarGridSpec(
            num_scalar_prefetch=3,
            grid=(sum(tiles),),
            in_specs=[spec(g) for g in range(len(tiles))],
            out_specs=pl.BlockSpec(memory_space=pl.ANY),
            scratch_shapes=[pltpu.VMEM((2, TOP_K * tm * ROW_SUB, LANES), I32),
                            pltpu.VMEM((MOE_ROWS * ROW_SUB, LANES), I32),
                            pltpu.SemaphoreType.DMA((2,)), pltpu.SemaphoreType.DMA(())]),
        out_shape=jax.ShapeDtypeStruct((n_rows * ROW_SUB, LANES), I32),
        compiler_params=_cparams(("arbitrary",), disable_bounds_checks=True, has_side_effects=True),
        name="sort_rows",
    )(dest, pad_start, pad_end, *h2_groups)


def _expert_kernel(pos_ref, seq_ref, nused_ref, xs_ref, w1_hbm, w3_hbm, w2_hbm, y_ref,
                   w1f, w3f, w2f, wsem, w1b, w3b, w2b, *, rows):
    j = pl.program_id(0)
    weights = ((w1_hbm, w1f), (w3_hbm, w3f), (w2_hbm, w2f))

    def start_weights(k, buf):
        e = seq_ref[k]
        for src, dst in weights:
            n = src.shape[1] // WEIGHT_CHUNKS
            for c in range(WEIGHT_CHUNKS):
                pltpu.make_async_copy(src.at[e, pl.ds(c * n, n), :], dst.at[buf, pl.ds(c * n, n), :],
                                      wsem.at[buf]).start(priority=1)

    def wait_weights(k, buf):
        e = seq_ref[k]
        for src, dst in weights:
            pltpu.make_async_copy(src.at[e], dst.at[buf], wsem.at[buf]).wait()

    @pl.when(j == 0)
    def _():
        start_weights(0, 0)

    pos = pos_ref[j]
    pos_prev = pos_ref[jnp.maximum(j - 1, 0)]

    @pl.when((j == 0) | (pos != pos_prev))
    def _():
        buf = pos % 2
        wait_weights(pos, buf)

        @pl.when(pos + 1 < nused_ref[1])
        def _():
            start_weights(pos + 1, 1 - buf)

        w1b[...] = w1f[buf].astype(BF16)
        w3b[...] = w3f[buf].astype(BF16)
        w2b[...] = w2f[buf].astype(BF16)

    @pl.when(j < nused_ref[0])
    def _():
        hi, lo = _unpack_pair(_load_tile_rows(xs_ref, 0, rows))
        xb = jnp.concatenate([hi.astype(BF16), lo.astype(BF16)], axis=1)
        a = jnp.dot(xb, w1b[...], preferred_element_type=F32)
        g = jnp.dot(xb, w3b[...], preferred_element_type=F32)
        hid = (a * _sigmoid(a) * g).astype(BF16)
        y = jnp.dot(hid, w2b[...], preferred_element_type=F32)
        half = y.shape[1] // 2
        _store_tile_rows(y_ref, _pack_pair(y[:, :half], y[:, half:]))

    @pl.when(j >= nused_ref[0])
    def _():
        y_ref[...] = jnp.zeros_like(y_ref)


def _experts(xs, block_pos, expert_seq, n_used, w1, w3, w2):
    rows = MOE_ROWS
    blk = rows * ROW_SUB
    d, f = w1.shape[1], w1.shape[2]
    hbm = pl.BlockSpec(memory_space=pl.ANY)
    return pl.pallas_call(
        functools.partial(_expert_kernel, rows=rows),
        grid_spec=pltpu.PrefetchScalarGridSpec(
            num_scalar_prefetch=3,
            grid=(xs.shape[0] // blk,),
            in_specs=[pl.BlockSpec((blk, LANES), lambda j, bp, es, nu: (jnp.minimum(j, nu[0] - 1), 0)),
                      hbm, hbm, hbm],
            out_specs=pl.BlockSpec((blk, LANES), lambda j, *_: (j, 0)),
            scratch_shapes=[pltpu.VMEM((2, d, f), F32), pltpu.VMEM((2, d, f), F32), pltpu.VMEM((2, f, d), F32),
                            pltpu.SemaphoreType.DMA((2,)),
                            pltpu.VMEM((d, f), BF16), pltpu.VMEM((d, f), BF16), pltpu.VMEM((f, d), BF16)]),
        out_shape=jax.ShapeDtypeStruct(xs.shape, I32),
        compiler_params=_cparams(("arbitrary",)),
        name="experts",
    )(block_pos, expert_seq, n_used, xs, w1, w3, w2)


def _combine_kernel(dest_ref, y_hbm, x1_ref, route_ref, g2_ref, o_ref, ybuf, sem, *, tm):
    i = pl.program_id(0)
    n = pl.num_programs(0)
    slot = i % 2

    def start_gather(tile, sl):
        def body(gi, carry):
            r0 = gi * DMA_GROUP
            base = (tile * tm + r0) * TOP_K
            dests = [dest_ref[base + q] for q in range(DMA_GROUP * TOP_K)]
            for q, dest in enumerate(dests):
                pltpu.make_async_copy(_tile_row(y_hbm, dest),
                                      _tile_row(ybuf.at[sl], (q % TOP_K) * tm + r0 + q // TOP_K),
                                      sem.at[sl]).start(priority=q % DMA_QUEUES)
            return carry

        lax.fori_loop(0, tm // DMA_GROUP, body, 0)

    @pl.when(i == 0)
    def _():
        start_gather(0, 0)

    @pl.when(i + 1 < n)
    def _():
        start_gather(i + 1, 1 - slot)

    pltpu.make_async_copy(y_hbm.at[pl.ds(0, TOP_K * tm * ROW_SUB), :], ybuf.at[slot], sem.at[slot]).wait()
    route = route_ref[...]
    w1 = route[:, 2:3]
    w2 = route[:, 3:4]
    hi1, lo1 = _unpack_pair(_load_tile_rows(ybuf, 0, tm, lead=(slot,)))
    hi2, lo2 = _unpack_pair(_load_tile_rows(ybuf, tm, tm, lead=(slot,)))
    half = o_ref.shape[1] // 2
    o_ref[:, :half] = x1_ref[:, :half] + g2_ref[:, :half] * (w1 * hi1 + w2 * hi2)
    o_ref[:, half:] = x1_ref[:, half:] + g2_ref[:, half:] * (w1 * lo1 + w2 * lo2)


def _combine(y, dest, x1, route, g2, s):
    t, d = x1.shape
    tm = ROW_TILE
    per_b_tiles = s // tm
    return pl.pallas_call(
        functools.partial(_combine_kernel, tm=tm),
        grid_spec=pltpu.PrefetchScalarGridSpec(
            num_scalar_prefetch=1,
            grid=(t // tm,),
            in_specs=[pl.BlockSpec(memory_space=pl.ANY),
                      pl.BlockSpec((tm, d), lambda i, ds: (i, 0)),
                      pl.BlockSpec((tm, ROUTE_LANES), lambda i, ds: (i, 0)),
                      pl.BlockSpec((None, 1, d), lambda i, ds: (i // per_b_tiles, 0, 0))],
            out_specs=pl.BlockSpec((tm, d), lambda i, ds: (i, 0)),
            scratch_shapes=[pltpu.VMEM((2, TOP_K * tm * ROW_SUB, LANES), I32), pltpu.SemaphoreType.DMA((2,))]),
        out_shape=jax.ShapeDtypeStruct((t, d), F32),
        compiler_params=_cparams(("arbitrary",), disable_bounds_checks=True),
        name="combine",
    )(dest, y, x1, route, g2)


def _dispatch_plan(route_keys, counts, t):
    rows = MOE_ROWS
    keys = route_keys.astype(I32)
    expert = lax.shift_right_logical(keys, jnp.full(keys.shape, RANK_BITS, I32))
    rank = keys & ((1 << RANK_BITS) - 1)
    cnt = counts[0, N_GROUPS:N_GROUPS + N_EXPERTS].astype(I32)
    padded = (cnt + rows - 1) // rows * rows
    pad_end = jnp.cumsum(padded)
    pad_start = pad_end - padded
    onehot = expert[..., None] == jnp.arange(N_EXPERTS, dtype=I32)
    dest = (jnp.sum(jnp.where(onehot, pad_start, 0), axis=-1) + rank).reshape(t * TOP_K)
    n_blocks = (t * TOP_K + N_EXPERTS * (rows - 1) + rows - 1) // rows
    n_used = pad_end[-1:] // rows
    blk = jnp.minimum(jnp.arange(n_blocks, dtype=I32), n_used - 1) * rows
    block_e = jnp.sum((pad_end[None, :] <= blk[:, None]).astype(I32), axis=1)
    owns = cnt > 0
    expert_seq = jnp.argsort(jnp.logical_not(owns), stable=True).astype(I32)
    seq_index = jnp.cumsum(owns.astype(I32)) - 1
    block_pos = jnp.sum(jnp.where(block_e[:, None] == jnp.arange(N_EXPERTS, dtype=I32), seq_index, 0), axis=1)
    used = jnp.concatenate([n_used, jnp.sum(owns.astype(I32), keepdims=True)]).astype(I32)
    return dest, pad_start, pad_end, block_pos.astype(I32), expert_seq, used, n_blocks * rows


def _layer(xs, mods, p):
    d = xs[0].shape[-1]
    t_all = sum(x.shape[0] * x.shape[1] for x in xs)
    counts = jnp.zeros((1, ROUTE_LANES), F32)
    rope = _rope_tables(max(x.shape[1] for x in xs))
    per_group, h2_groups, tok_off = [], [], 0
    for x, mod in zip(xs, mods):
        b, s, _ = x.shape
        sh1, sc1, g1, sh2, sc2, g2 = [m.reshape(b, 1, d) for m in jnp.split(mod, 6, axis=-1)]
        u, qkv, gates = _inproj(x, sh1, sc1, p["norm1_g"], p["q_norm_g"], p["k_norm_g"], p["w_in"], rope)
        wp = _seq_fft(u)
        attn = _attention(qkv, p["sink"])
        x1, h2, route, counts = _post(wp, attn, gates, x, g1, sh2, sc2, p["norm2_g"], p["cc"], p["cs"],
                                      p["w_fourier_out"], p["w_attn_out"], p["w_out"],
                                      p["w_router"], p["b_router"], counts)
        per_group.append((x1, route, g2, tok_off))
        h2_groups.append(h2)
        tok_off += b * s
    route_keys = jnp.concatenate([r[:, 0:TOP_K] for _, r, _, _ in per_group], axis=0)
    dest, pad_start, pad_end, block_pos, expert_seq, n_used, n_rows = _dispatch_plan(route_keys, counts, t_all)
    xs_sorted = _sortrows(h2_groups, dest, pad_start, pad_end, n_rows)
    y = _experts(xs_sorted, block_pos, expert_seq, n_used, p["w1"], p["w3"], p["w2"])
    outs = []
    for x, (x1, route, g2, off) in zip(xs, per_group):
        b, s, _ = x.shape
        dest_g = lax.slice(dest, (off * TOP_K,), ((off + b * s) * TOP_K,))
        outs.append(_combine(y, dest_g, x1, route, g2, s).reshape(b, s, d))
    return outs


def _channel_dft():
    n = FOURIER_GROUP_DIM
    k = np.arange(n, dtype=np.int64)
    ang = ((k[:, None] * k[None, :]) % n) * (2.0 * math.pi / n)
    return (jnp.asarray((np.cos(ang) * n ** -0.5).astype(BF16)),
            jnp.asarray((np.sin(ang) * n ** -0.5).astype(BF16)))


def kernel(x_prompt, x_sample, c_prompt, c_sample, w_ada, b_ada, norm1_g, w_in, q_norm_g, k_norm_g, sink,
           w_fourier_out, w_attn_out, w_out, norm2_g, w_group, b_group, w_expert, b_expert, w1, w3, w2):
    depth = w_ada.shape[0]
    d = x_prompt.shape[-1]
    bp = c_prompt.shape[0]
    bs = c_sample.shape[0]
    assert d == 2 * ROW_WORDS and w_in.shape[-1] == (FOURIER_GROUPS * FOURIER_GROUP_DIM
                                                     + (N_HEADS + 2 * N_KV_HEADS) * HEAD_DIM + 2 * d)
    assert w_group.shape[-1] == N_GROUPS and w_expert.shape[-1] == N_EXPERTS
    for x in (x_prompt, x_sample):
        seq = x.shape[1]
        assert seq % max(ATTN_TILE, FFT_N1 * FFT_UNROLL) == 0 and (x.shape[0] * seq) % SORT_TILE == 0
    cc, cs = _channel_dft()
    xp, xs = x_prompt, x_sample
    for l in range(depth):
        c_all = jnp.concatenate([c_prompt, c_sample, jnp.zeros((8 - (bp + bs) % 8, d), F32)], axis=0)
        mod = _adaln(c_all, w_ada[l], b_ada[l])
        pad = ROUTE_LANES - N_GROUPS - N_EXPERTS
        w_router = jnp.concatenate([w_group[l], w_expert[l], jnp.zeros((d, pad), F32)], axis=1)
        w_router_hi = w_router.astype(BF16)
        p = {
            "norm1_g": norm1_g[l].reshape(1, d), "norm2_g": norm2_g[l].reshape(1, d),
            "q_norm_g": q_norm_g[l].reshape(1, HEAD_DIM), "k_norm_g": k_norm_g[l].reshape(1, HEAD_DIM),
            "sink": sink[l], "w_in": w_in[l].astype(BF16),
            "w_fourier_out": w_fourier_out[l].astype(BF16), "w_attn_out": w_attn_out[l].astype(BF16),
            "w_out": w_out[l].astype(BF16), "cc": cc, "cs": cs,
            "w_router": jnp.concatenate(
                [w_router_hi, (w_router - w_router_hi.astype(F32)).astype(BF16)], axis=1),
            "b_router": jnp.concatenate([b_group[l], b_expert[l], jnp.zeros((pad,), F32)]).reshape(1, ROUTE_LANES),
            "w1": w1[l], "w3": w3[l], "w2": w2[l],
        }
        xp, xs = _layer([xp, xs], [mod[:bp], mod[bp:bp + bs]], p)
    return xp, xs
```
